```python
import math
import jax, jax.numpy as jnp
from jax import lax
import numpy as np

D_MODEL = 1024
BATCH = 4
SEQ = 8192
DEPTH = 1
DEC_BATCH = 32
DEC_SEQ = 1
PAST_LEN = 16384
PAGE_SIZE = 128

HEAD_DIM = 64
ROPE_DIMS = HEAD_DIM // 4
ROPE_THETA = 500000.0
NSA_HEADS = 8
NSA_KV_HEADS = 2
NSA_GROUP = NSA_HEADS // NSA_KV_HEADS
CMP_BLOCK = 32
CMP_STRIDE = 16
CMP_RATIO = CMP_BLOCK // CMP_STRIDE
SEL_BLOCK = 64
SEL_TOPN = 16
WINDOW = 512
NSA_QB = 128
MOBA_HEADS = 8
MOBA_BLOCK = 256
MOBA_TOPK = 3
MOBA_QB = 32
N_EXPERTS = 64
TOP_K = 8
N_GROUPS = 8
TOPK_GROUPS = 4
D_EXPERT = 256
D_SHARED = 256
ROUTED_SCALE = 2.5
ALPHA = (2 * DEPTH) ** 0.25
BETA = (8 * DEPTH) ** -0.25
LN_EPS = 1e-5

NSA_Q = NSA_HEADS * HEAD_DIM
NSA_KV = NSA_KV_HEADS * HEAD_DIM
MOBA_W = MOBA_HEADS * HEAD_DIM
D_IN = NSA_Q + 6 * NSA_KV + 3 * NSA_HEADS + 3 * MOBA_W + 2 * D_MODEL

kernel_name = "nsa_moba_gated_moe_decoder_step"


def _qblock(T, pref):
    return pref if T % pref == 0 else T


def _rope(x, pos):
    half = ROPE_DIMS // 2
    inv = ROPE_THETA ** (-jnp.arange(half, dtype=jnp.float32) / half)
    ang = pos.astype(jnp.float32)[:, None] * inv
    cos = jnp.cos(ang)[:, None, :].astype(x.dtype)
    sin = jnp.sin(ang)[:, None, :].astype(x.dtype)
    x1, x2, rest = x[..., :half], x[..., half:ROPE_DIMS], x[..., ROPE_DIMS:]
    return jnp.concatenate([x1 * cos - x2 * sin, x1 * sin + x2 * cos, rest], axis=-1)


def _layer_norm(x, g, b):
    xf = x.astype(jnp.float32)
    mu = jnp.mean(xf, -1, keepdims=True)
    var = jnp.mean(jnp.square(xf - mu), -1, keepdims=True)
    return ((xf - mu) * lax.rsqrt(var + LN_EPS) * g + b).astype(x.dtype)


def _masked_softmax(s, mask):
    s = jnp.where(mask, s, -jnp.inf)
    m = jnp.max(s, axis=-1, keepdims=True)
    e = jnp.exp(s - jnp.where(jnp.isfinite(m), m, 0.0))
    d = jnp.sum(e, axis=-1, keepdims=True)
    return e / jnp.where(d > 0, d, 1.0)


_gather_blocks = jax.vmap(jax.vmap(lambda blk, ix: blk[ix]))


def _compress(k, w1, w2, pe):
    B, Tk, G, hd = k.shape
    n_chunk = Tk // CMP_STRIDE
    nc = n_chunk - CMP_RATIO + 1
    ch = k[:, :n_chunk * CMP_STRIDE].reshape(B, n_chunk, CMP_STRIDE, G, hd)
    h = jnp.einsum('bcsgd,sde->bcge', ch[:, 0:nc] + pe[0][:, None, :], w1[0])
    for r in range(1, CMP_RATIO):
        h = h + jnp.einsum('bcsgd,sde->bcge', ch[:, r:r + nc] + pe[r][:, None, :], w1[r])
    return jnp.einsum('bcge,ef->bcgf', jax.nn.silu(h), w2)


def _cmp_to_sel(nc, ns):
    cs = jnp.arange(nc) * CMP_STRIDE
    ss = jnp.arange(ns) * SEL_BLOCK
    return ((cs[:, None] < ss[None] + SEL_BLOCK) & (cs[:, None] + CMP_BLOCK > ss[None])).astype(jnp.float32)


def _nsa(q, kc_raw, vc_raw, ks, vs, kw, vw, gates, qpos0, ck_w1, ck_w2, ck_pe, cv_w1, cv_w2, cv_pe):
    B, Tq, H, hd = q.shape
    G = NSA_KV_HEADS
    Tk = ks.shape[1]
    scale = hd ** -0.5
    kc = _compress(kc_raw, ck_w1, ck_w2, ck_pe)
    vc = _compress(vc_raw, cv_w1, cv_w2, cv_pe)
    nc = kc.shape[1]
    ns = -(-Tk // SEL_BLOCK)
    pad = ns * SEL_BLOCK - Tk
    padw = ((0, 0), (0, pad), (0, 0), (0, 0))
    ksb = jnp.pad(ks, padw).reshape(B, ns, SEL_BLOCK, G, hd).transpose(0, 3, 1, 2, 4)
    vsb = jnp.pad(vs, padw).reshape(B, ns, SEL_BLOCK, G, hd).transpose(0, 3, 1, 2, 4)
    c2s = _cmp_to_sel(nc, ns)
    n_sel = min(SEL_TOPN, ns)
    q_rot = _rope(q, qpos0 + jnp.arange(Tq))
    qb = _qblock(Tq, NSA_QB)
    cpos_end = jnp.arange(nc) * CMP_STRIDE + CMP_BLOCK - 1
    jsel = jnp.arange(ns)

    def body(i):
        s0 = i * qb
        pos = qpos0 + s0 + jnp.arange(qb)
        qr = lax.dynamic_slice_in_dim(q, s0, qb, 1).reshape(B, qb, G, NSA_GROUP, hd)
        qo = lax.dynamic_slice_in_dim(q_rot, s0, qb, 1).reshape(B, qb, G, NSA_GROUP, hd)
        g = lax.dynamic_slice_in_dim(gates, s0, qb, 1).reshape(B, qb, G, NSA_GROUP, 3)
        s_c = jnp.einsum('bqghd,bcgd->bgqhc', qr, kc, preferred_element_type=jnp.float32) * scale
        p_c = _masked_softmax(s_c, (cpos_end[None, :] <= pos[:, None])[None, None, :, None, :])
        o_c = jnp.einsum('bgqhc,bcgd->bqghd', p_c.astype(vc.dtype), vc)
        imp = jnp.einsum('bgqc,cn->bgqn', p_c.sum(3), c2s)
        jq = (pos // SEL_BLOCK)[:, None]
        forced = (jsel == 0) | (jsel == jq) | (jsel == jq - 1)
        score = jnp.where(jsel <= jq, jnp.where(forced, jnp.inf, imp), -jnp.inf)
        top_s, idx = lax.top_k(score, n_sel)
        kpos = idx[..., None] * SEL_BLOCK + jnp.arange(SEL_BLOCK)
        m_s = (top_s > -jnp.inf)[..., None] & (kpos <= pos[:, None, None])
        k_g = _gather_blocks(ksb, idx).reshape(B, G, qb, n_sel * SEL_BLOCK, hd)
        v_g = _gather_blocks(vsb, idx).reshape(B, G, qb, n_sel * SEL_BLOCK, hd)
        s_s = jnp.einsum('bqghd,bgqkd->bgqhk', qo, k_g, preferred_element_type=jnp.float32) * scale
        p_s = _masked_softmax(s_s, m_s.reshape(B, G, qb, 1, n_sel * SEL_BLOCK))
        o_s = jnp.einsum('bgqhk,bgqkd->bqghd', p_s.astype(v_g.dtype), v_g)
        kwb = lax.dynamic_slice_in_dim(kw, s0, qb + WINDOW, 1)
        vwb = lax.dynamic_slice_in_dim(vw, s0, qb + WINDOW, 1)
        wpos = qpos0 - WINDOW + s0 + jnp.arange(qb + WINDOW)
        m_w = (wpos[None] <= pos[:, None]) & (wpos[None] > pos[:, None] - WINDOW) & (wpos[None] >= 0)
        s_w = jnp.einsum('bqghd,bkgd->bgqhk', qo, kwb, preferred_element_type=jnp.float32) * scale
        p_w = _masked_softmax(s_w, m_w[None, None, :, None, :])
        o_w = jnp.einsum('bgqhk,bkgd->bqghd', p_w.astype(vwb.dtype), vwb)
        gs = jax.nn.sigmoid(g.astype(jnp.float32)).astype(q.dtype)
        o = gs[..., 0:1] * o_c + gs[..., 1:2] * o_s + gs[..., 2:3] * o_w
        return o.reshape(B, qb, H * hd)

    out = lax.map(body, jnp.arange(Tq // qb))
    return out.transpose(1, 0, 2, 3).reshape(B, Tq, H * hd)


def _moba(q, k, v, qpos0):
    B, Tq, H, hd = q.shape
    Tk = k.shape[1]
    scale = hd ** -0.5
    nb = -(-Tk // MOBA_BLOCK)
    padw = ((0, 0), (0, nb * MOBA_BLOCK - Tk), (0, 0), (0, 0))
    kb = jnp.pad(k, padw).reshape(B, nb, MOBA_BLOCK, H, hd)
    vb = jnp.pad(v, padw).reshape(B, nb, MOBA_BLOCK, H, hd)
    kmean = jnp.mean(kb.astype(jnp.float32), axis=2)
    kbt = kb.transpose(0, 3, 1, 2, 4)
    vbt = vb.transpose(0, 3, 1, 2, 4)
    n_top = min(MOBA_TOPK, nb)
    jb = jnp.arange(nb)
    qb = _qblock(Tq, MOBA_QB)

    def body(i):
        s0 = i * qb
        pos = qpos0 + s0 + jnp.arange(qb)
        qq = lax.dynamic_slice_in_dim(q, s0, qb, 1)
        jq = pos // MOBA_BLOCK
        gsc = jnp.einsum('bqhd,bnhd->bhqn', qq.astype(jnp.float32), kmean)
        gsc = jnp.where(jb[None] < jq[:, None], gsc, -jnp.inf)
        top_s, top_i = lax.top_k(gsc, n_top)
        idx = jnp.concatenate([top_i, jnp.broadcast_to(jq[:, None], (B, H, qb, 1))], -1)
        ok = jnp.concatenate([top_s > -jnp.inf, jnp.ones((B, H, qb, 1), bool)], -1)
        kpos = idx[..., None] * MOBA_BLOCK + jnp.arange(MOBA_BLOCK)
        mask = ok[..., None] & (kpos <= pos[:, None, None])
        kg = _gather_blocks(kbt, idx).reshape(B, H, qb, (n_top + 1) * MOBA_BLOCK, hd)
        vg = _gather_blocks(vbt, idx).reshape(B, H, qb, (n_top + 1) * MOBA_BLOCK, hd)
        s = jnp.einsum('bqhd,bhqkd->bhqk', qq, kg, preferred_element_type=jnp.float32) * scale
        p = _masked_softmax(s, mask.reshape(B, H, qb, (n_top + 1) * MOBA_BLOCK))
        o = jnp.einsum('bhqk,bhqkd->bqhd', p.astype(vg.dtype), vg)
        return o.reshape(B, qb, H * hd)

    out = lax.map(body, jnp.arange(Tq // qb))
    return out.transpose(1, 0, 2, 3).reshape(B, Tq, H * hd)


def _route(h, w_router, b_router):
    N = h.shape[0]
    s = jax.nn.sigmoid(jnp.einsum('nd,de->ne', h, w_router, preferred_element_type=jnp.float32))
    biased = s + b_router.astype(jnp.float32)
    per = N_EXPERTS // N_GROUPS
    gscore = lax.top_k(biased.reshape(N, N_GROUPS, per), 2)[0].sum(-1)
    _, gidx = lax.top_k(gscore, TOPK_GROUPS)
    gmask = jnp.any(gidx[:, :, None] == jnp.arange(N_GROUPS), axis=1)
    masked = jnp.where(jnp.repeat(gmask, per, axis=1), biased, -jnp.inf)
    _, topi = lax.top_k(masked, TOP_K)
    w = jnp.take_along_axis(s, topi, 1)
    w = w / jnp.sum(w, -1, keepdims=True) * ROUTED_SCALE
    return topi, w


def _routed_experts(h, topi, topw, w_gate, w_up, w_down):
    N, D = h.shape
    A = N * TOP_K
    blk = min(128, max(8, A // N_EXPERTS))
    flat_e = topi.reshape(-1)
    order = jnp.argsort(flat_e)
    e_sorted = flat_e[order]
    tok = (order // TOP_K).astype(jnp.int32)
    counts = jnp.zeros((N_EXPERTS,), jnp.int32).at[flat_e].add(1)
    padded = (counts + blk - 1) // blk * blk
    end_pad = jnp.cumsum(padded)
    start_pad = end_pad - padded
    start = jnp.cumsum(counts) - counts
    dest = start_pad[e_sorted] + jnp.arange(A) - start[e_sorted]
    nblk = -(-A // blk) + N_EXPERTS
    L = nblk * blk
    row_tok = jnp.full((L,), N, jnp.int32).at[dest].set(tok)
    row_w = jnp.zeros((L,), jnp.float32).at[dest].set(topw.reshape(-1)[order])
    blk_e = jnp.minimum(jnp.searchsorted(end_pad, jnp.arange(nblk) * blk, side='right'), N_EXPERTS - 1)
    h_ext = jnp.concatenate([h, jnp.zeros((1, D), h.dtype)], 0)

    def body(args):
        rows, e = args
        xb = h_ext[rows]
        return (jax.nn.silu(xb @ w_gate[e]) * (xb @ w_up[e])) @ w_down[e]

    y = lax.map(body, (row_tok.reshape(nblk, blk), blk_e))
    out = jnp.zeros((N + 1, D), jnp.float32).at[row_tok].add(y.reshape(L, D).astype(jnp.float32) * row_w[:, None])
    return out[:N].astype(h.dtype)


def _moe_ffn(v, w_router, b_router, w_exp_gate, w_exp_up, w_exp_down, w_sh_gate, w_sh_up, w_sh_down):
    B, T, D = v.shape
    h = v.reshape(B * T, D)
    topi, topw = _route(h, w_router, b_router)
    y = _routed_experts(h, topi, topw, w_exp_gate, w_exp_up, w_exp_down)
    y = y + (jax.nn.silu(h @ w_sh_gate) * (h @ w_sh_up)) @ w_sh_down
    return y.reshape(B, T, D)


def _decoder_layer(x, c, qpos0, past_nsa, past_moba, win_buf,
                   w_ada, b_ada, w_in, cmp_k_w1, cmp_k_w2, cmp_k_pe, cmp_v_w1, cmp_v_w2, cmp_v_pe,
                   w_nsa_out, w_moba_out, w_o, ln1_g, ln1_b, w_router, b_router,
                   w_exp_gate, w_exp_up, w_exp_down, w_sh_gate, w_sh_up, w_sh_down, ln2_g, ln2_b):
    B, T, D = x.shape
    G, hd = NSA_KV_HEADS, HEAD_DIM
    mod = jnp.einsum('bd,de->be', jax.nn.silu(c), w_ada) + b_ada
    sh1, sc1, g1, sh2, sc2, g2 = [m[:, None, :] for m in jnp.split(mod, 6, axis=-1)]
    u = x * (1 + sc1) + sh1
    proj = jnp.einsum('btd,de->bte', u, w_in)
    sizes = (NSA_Q, 6 * NSA_KV, 3 * NSA_HEADS, MOBA_W, MOBA_W, MOBA_W, D_MODEL, D_MODEL)
    offs, acc = [], 0
    for sz in sizes[:-1]:
        acc += sz
        offs.append(acc)
    q_a, kv_a, gate_a, q_b, k_b, v_b, mg_a, mg_b = jnp.split(proj, offs, axis=-1)
    pos = qpos0 + jnp.arange(T)
    q_a = q_a.reshape(B, T, NSA_HEADS, hd)
    kv_a = kv_a.reshape(B, T, 6, G, hd)
    k_c, v_c, k_s, v_s, k_w, v_w = [kv_a[:, :, j] for j in range(6)]
    k_s = _rope(k_s, pos)
    k_w = _rope(k_w, pos)
    q_b = _rope(q_b.reshape(B, T, MOBA_HEADS, hd), pos)
    k_b = _rope(k_b.reshape(B, T, MOBA_HEADS, hd), pos)
    v_b = v_b.reshape(B, T, MOBA_HEADS, hd)
    new_nsa = jnp.stack([k_c, v_c, k_s, v_s], axis=2)
    new_moba = jnp.stack([k_b, v_b], axis=2)
    win_new = jnp.stack([k_w, v_w], axis=2)
    if past_nsa is None:
        full_nsa, full_moba = new_nsa, new_moba
        n_keep = min(WINDOW, T)
        win_prev = jnp.zeros((B, WINDOW) + win_new.shape[2:], win_new.dtype)
    else:
        full_nsa = jnp.concatenate([past_nsa, new_nsa], axis=1)
        full_moba = jnp.concatenate([past_moba, new_moba], axis=1)
        n_keep = win_buf.shape[1]
        win_prev = jnp.pad(win_buf, ((0, 0), (WINDOW - n_keep, 0), (0, 0), (0, 0), (0, 0)))
    win_all = jnp.concatenate([win_prev, win_new], axis=1)
    new_win = win_all[:, win_all.shape[1] - n_keep:]
    o_a = _nsa(q_a, full_nsa[:, :, 0], full_nsa[:, :, 1], full_nsa[:, :, 2], full_nsa[:, :, 3],
               win_all[:, :, 0], win_all[:, :, 1], gate_a.reshape(B, T, NSA_HEADS, 3), qpos0,
               cmp_k_w1, cmp_k_w2, cmp_k_pe, cmp_v_w1, cmp_v_w2, cmp_v_pe)
    o_b = _moba(q_b, full_moba[:, :, 0], full_moba[:, :, 1], qpos0)
    y_a = jnp.einsum('bte,ed->btd', o_a, w_nsa_out)
    y_b = jnp.einsum('bte,ed->btd', o_b, w_moba_out)
    mix = jnp.einsum('btd,de->bte', jax.nn.sigmoid(mg_a) * y_a + jax.nn.sigmoid(mg_b) * y_b, w_o)
    x = _layer_norm(ALPHA * x + g1 * mix, ln1_g, ln1_b)
    v = x * (1 + sc2) + sh2
    f = _moe_ffn(v, w_router, b_router, w_exp_gate, w_exp_up, w_exp_down, w_sh_gate, w_sh_up, w_sh_down)
    x = _layer_norm(ALPHA * x + g2 * f, ln2_g, ln2_b)
    return x, new_nsa, new_moba, new_win


def setup_inputs(seed: int = 0) -> dict:
    key = jax.random.key(seed)
    ks = jax.random.split(key, 40)
    f32 = jnp.float32
    L, D, hd, G = DEPTH, D_MODEL, HEAD_DIM, NSA_KV_HEADS
    n_pages = PAST_LEN // PAGE_SIZE
    n_used = DEC_BATCH * n_pages
    n_pool = n_used + n_used // 4
    win_buf = min(WINDOW, PAST_LEN)

    def nrm(k, shape, s):
        return jax.random.normal(k, shape, f32) * s

    page_table = jax.random.permutation(ks[0], n_pool)[:n_used].reshape(DEC_BATCH, n_pages).astype(jnp.int32)
    return {
        "x_prompt": nrm(ks[1], (BATCH, SEQ, D), 1.0),
        "x_sample": nrm(ks[2], (DEC_BATCH, DEC_SEQ, D), 1.0),
        "cache_nsa": nrm(ks[3], (L, n_pool, PAGE_SIZE, 4, G, hd), 1.0),
        "cache_moba": nrm(ks[4], (L, n_pool, PAGE_SIZE, 2, MOBA_HEADS, hd), 1.0),
        "state_nsa_win": nrm(ks[5], (L, DEC_BATCH, win_buf, 2, G, hd), 1.0),
        "page_table": page_table,
        "c_prompt": nrm(ks[6], (BATCH, D), 1.0),
        "c_sample": nrm(ks[7], (DEC_BATCH, D), 1.0),
        "w_ada": nrm(ks[8], (L, D, 6 * D), D ** -0.5),
        "b_ada": nrm(ks[9], (L, 6 * D), 0.02),
        "w_in": nrm(ks[10], (L, D, D_IN), D ** -0.5),
        "cmp_k_w1": nrm(ks[11], (L, CMP_RATIO, CMP_STRIDE, hd, hd), (CMP_BLOCK * hd) ** -0.5),
        "cmp_k_w2": nrm(ks[12], (L, hd, hd), hd ** -0.5),
        "cmp_k_pe": nrm(ks[13], (L, CMP_RATIO, CMP_STRIDE, hd), 0.1),
        "cmp_v_w1": nrm(ks[14], (L, CMP_RATIO, CMP_STRIDE, hd, hd), (CMP_BLOCK * hd) ** -0.5),
        "cmp_v_w2": nrm(ks[15], (L, hd, hd), hd ** -0.5),
        "cmp_v_pe": nrm(ks[16], (L, CMP_RATIO, CMP_STRIDE, hd), 0.1),
        "w_nsa_out": nrm(ks[17], (L, NSA_Q, D), NSA_Q ** -0.5),
        "w_moba_out": nrm(ks[18], (L, MOBA_W, D), MOBA_W ** -0.5),
        "w_o": nrm(ks[19], (L, D, D), BETA * D ** -0.5),
        "ln1_g": 1.0 + nrm(ks[20], (L, D), 0.02),
        "ln1_b": nrm(ks[21], (L, D), 0.02),
        "w_router": nrm(ks[22], (L, D, N_EXPERTS), D ** -0.5),
        "b_router": nrm(ks[23], (L, N_EXPERTS), 0.01),
        "w_exp_gate": nrm(ks[24], (L, N_EXPERTS, D, D_EXPERT), D ** -0.5),
        "w_exp_up": nrm(ks[25], (L, N_EXPERTS, D, D_EXPERT), D ** -0.5),
        "w_exp_down": nrm(ks[26], (L, N_EXPERTS, D_EXPERT, D), BETA * D_EXPERT ** -0.5),
        "w_sh_gate": nrm(ks[27], (L, D, D_SHARED), D ** -0.5),
        "w_sh_up": nrm(ks[28], (L, D, D_SHARED), D ** -0.5),
        "w_sh_down": nrm(ks[29], (L, D_SHARED, D), BETA * D_SHARED ** -0.5),
        "ln2_g": 1.0 + nrm(ks[30], (L, D), 0.02),
        "ln2_b": nrm(ks[31], (L, D), 0.02),
    }


def reference(x_prompt, x_sample, cache_nsa, cache_moba, state_nsa_win, page_table, c_prompt, c_sample,
              w_ada, b_ada, w_in, cmp_k_w1, cmp_k_w2, cmp_k_pe, cmp_v_w1, cmp_v_w2, cmp_v_pe,
              w_nsa_out, w_moba_out, w_o, ln1_g, ln1_b, w_router, b_router,
              w_exp_gate, w_exp_up, w_exp_down, w_sh_gate, w_sh_up, w_sh_down, ln2_g, ln2_b):
    params = (w_ada, b_ada, w_in, cmp_k_w1, cmp_k_w2, cmp_k_pe, cmp_v_w1, cmp_v_w2, cmp_v_pe,
              w_nsa_out, w_moba_out, w_o, ln1_g, ln1_b, w_router, b_router,
              w_exp_gate, w_exp_up, w_exp_down, w_sh_gate, w_sh_up, w_sh_down, ln2_g, ln2_b)
    n_dec, n_pages = page_table.shape
    past_len = n_pages * cache_nsa.shape[2]
    xp, xs = x_prompt, x_sample
    nsa_p, nsa_s, moba_p, moba_s, win_p, win_s = [], [], [], [], [], []
    for l in range(DEPTH):
        lp = [p[l] for p in params]
        xp, a_p, b_p, w_p = _decoder_layer(xp, c_prompt, 0, None, None, None, *lp)
        past_nsa = cache_nsa[l][page_table].reshape((n_dec, past_len) + cache_nsa.shape[3:])
        past_moba = cache_moba[l][page_table].reshape((n_dec, past_len) + cache_moba.shape[3:])
        xs, a_s, b_s, w_s = _decoder_layer(xs, c_sample, past_len, past_nsa, past_moba, state_nsa_win[l], *lp)
        nsa_p.append(a_p)
        nsa_s.append(a_s)
        moba_p.append(b_p)
        moba_s.append(b_s)
        win_p.append(w_p)
        win_s.append(w_s)
    return (xp, xs, jnp.stack(nsa_p), jnp.stack(nsa_s), jnp.stack(moba_p), jnp.stack(moba_s),
            jnp.stack(win_p), jnp.stack(win_s))
```

```python
import functools

import jax
import jax.numpy as jnp
from jax import lax
from jax.experimental import pallas as pl
from jax.experimental.pallas import tpu as pltpu

F32 = jnp.float32
BF16 = jnp.bfloat16
I32 = jnp.int32
MXU_DTYPE = jnp.bfloat16

HEAD_DIM = 64
ROPE_DIMS = HEAD_DIM // 4
ROPE_HALF = ROPE_DIMS // 2
ROPE_THETA = 500000.0
NSA_HEADS = 8
NSA_KV_HEADS = 2
NSA_GROUP = NSA_HEADS // NSA_KV_HEADS
CMP_BLOCK = 32
CMP_STRIDE = 16
CMP_RATIO = CMP_BLOCK // CMP_STRIDE
SEL_BLOCK = 64
SEL_TOPN = 16
WINDOW = 512
MOBA_HEADS = 8
MOBA_BLOCK = 256
MOBA_TOPK = 3
N_EXPERTS = 64
TOP_K = 8
N_GROUPS = 8
TOPK_GROUPS = 4
ROUTED_SCALE = 2.5
LN_EPS = 1e-5

LANES = 128
NSA_Q = NSA_HEADS * HEAD_DIM
NSA_KV = NSA_KV_HEADS * HEAD_DIM
MOBA_W = MOBA_HEADS * HEAD_DIM
NSA_ROW = 4 * NSA_KV
MOBA_ROW = 2 * MOBA_W
GATE_PAD = LANES

NEG = -1e30
BIG = 3e38
TOKEN_TILE = 256
NSA_QB = 128
NSA_KC = 512
EXPERT_BLK = 256
PAGES_PER_STEP = 16


def _sigmoid(x):
    return 1.0 / (1.0 + jnp.exp(-x))


def _silu(x):
    return x * _sigmoid(x)


def _mm(a, b):
    return jnp.dot(a.astype(MXU_DTYPE), b.astype(MXU_DTYPE), preferred_element_type=F32)


def _mm_nt(a, b):
    return lax.dot_general(a.astype(MXU_DTYPE), b.astype(MXU_DTYPE), (((1,), (1,)), ((), ())),
                           preferred_element_type=F32)


def _iota(shape, axis):
    return lax.broadcasted_iota(I32, shape, axis)


def _cparams(sem, vmem_mb=None):
    kw = dict(dimension_semantics=sem)
    if vmem_mb is not None:
        kw["vmem_limit_bytes"] = vmem_mb << 20
    return pltpu.CompilerParams(**kw)


def _masked_softmax(s, mask):
    m = jnp.max(jnp.where(mask, s, NEG), axis=-1, keepdims=True)
    e = jnp.where(mask, jnp.exp(s - m), 0.0)
    d = jnp.sum(e, axis=-1, keepdims=True)
    return e / jnp.where(d > 0, d, 1.0)


def _layer_norm(z, g, b):
    mu = jnp.mean(z, axis=-1, keepdims=True)
    zc = z - mu
    var = jnp.mean(zc * zc, axis=-1, keepdims=True)
    return zc * lax.rsqrt(var + LN_EPS) * g + b


def _take_first_max(score, idx_f, n_f, axis):
    m = jnp.max(score, axis=axis, keepdims=True)
    first = jnp.min(jnp.where(score == m, idx_f, n_f), axis=axis, keepdims=True)
    return m, first, idx_f == first


def _ada_kernel(c_ref, w_ref, b_ref, o_ref):
    a = _silu(c_ref[...])
    o_ref[...] = _mm(a, w_ref[...]) + b_ref[...]


def _ada(c_all, w_ada, b_ada):
    r, d = c_all.shape
    e6 = w_ada.shape[1]
    tn = 1024
    return pl.pallas_call(
        _ada_kernel,
        grid=(e6 // tn,),
        in_specs=[pl.BlockSpec((r, d), lambda j: (0, 0)),
                  pl.BlockSpec((d, tn), lambda j: (0, j)),
                  pl.BlockSpec((1, tn), lambda j: (0, j))],
        out_specs=pl.BlockSpec((r, tn), lambda j: (0, j)),
        out_shape=jax.ShapeDtypeStruct((r, e6), F32),
        compiler_params=_cparams(("arbitrary",)),
        name="ada",
    )(c_all, w_ada, b_ada.reshape(1, e6))


_C_QA = 0
_C_KVA = _C_QA + NSA_Q
_C_QB = _C_KVA + 6 * NSA_KV
_C_KB = _C_QB + MOBA_W
_C_VB = _C_KB + MOBA_W
_C_GATE = _C_VB + MOBA_W
_C_END = _C_GATE + GATE_PAD


def _rope(x, cs, s1, s2):
    parts = []
    for j in range(x.shape[1] // LANES):
        xj = x[:, j * LANES:(j + 1) * LANES]
        parts.append(xj * cs + pltpu.roll(xj, ROPE_HALF, 1) * s1 + pltpu.roll(xj, LANES - ROPE_HALF, 1) * s2)
    return parts[0] if len(parts) == 1 else jnp.concatenate(parts, axis=1)


def _inproj_kernel(x_ref, sc_ref, sh_ref, w_ref, cs_ref, s1_ref, s2_ref,
                   qu_ref, qr_ref, qb_ref, nsa_ref, katt_ref, win_ref, moba_ref, kvb_ref, g_ref, *km_ref):
    u = x_ref[...] * (1.0 + sc_ref[...]) + sh_ref[...]
    r = _mm(u, w_ref[...])
    cs, s1, s2 = cs_ref[...], s1_ref[...], s2_ref[...]
    rope = lambda v: _rope(v, cs, s1, s2)
    qa = r[:, _C_QA:_C_KVA]
    qu_ref[...] = qa.astype(BF16)
    qr_ref[...] = rope(qa).astype(BF16)
    o = _C_KVA
    kc_vc = r[:, o:o + 2 * NSA_KV]
    ks = rope(r[:, o + 2 * NSA_KV:o + 3 * NSA_KV])
    vs = r[:, o + 3 * NSA_KV:o + 4 * NSA_KV]
    kw = rope(r[:, o + 4 * NSA_KV:o + 5 * NSA_KV])
    vw = r[:, o + 5 * NSA_KV:o + 6 * NSA_KV]
    nsa_ref[...] = jnp.concatenate([kc_vc, ks, vs], axis=1)
    win_ref[...] = jnp.concatenate([kw, vw], axis=1)
    katt_ref[...] = jnp.concatenate([ks, vs, kw, vw], axis=1).astype(BF16)
    qb_ref[...] = rope(r[:, _C_QB:_C_KB]).astype(BF16)
    kb = rope(r[:, _C_KB:_C_VB])
    mb = jnp.concatenate([kb, r[:, _C_VB:_C_GATE]], axis=1)
    moba_ref[...] = mb
    kvb_ref[...] = mb.astype(BF16)
    g_ref[...] = r[:, _C_GATE:_C_END]
    if km_ref:
        km_ref[0][...] = jnp.sum(kb, axis=0, keepdims=True) * (1.0 / kb.shape[0])


def _inproj(x, sc, sh, w_r, rope_tabs, *, tm, per_token_mod, rows_per_batch, with_kmean):
    n, d = x.shape
    nt = n // tm
    tiles_per_batch = rows_per_batch // tm
    row = lambda i: (i, 0)
    if per_token_mod:
        mod_spec = pl.BlockSpec((tm, d), row)
    else:
        mod_spec = pl.BlockSpec((None, 1, d), lambda i: (i // tiles_per_batch, 0, 0))
    tab_spec = pl.BlockSpec((tm, LANES), lambda i: (i % tiles_per_batch, 0))
    outs = [("qu", NSA_Q, BF16), ("qr", NSA_Q, BF16), ("qb", MOBA_W, BF16), ("nsa", NSA_ROW, F32),
            ("katt", 4 * NSA_KV, BF16), ("win", 2 * NSA_KV, F32), ("moba", MOBA_ROW, F32),
            ("kvb", MOBA_ROW, BF16), ("gate", GATE_PAD, F32)]
    out_shape = [jax.ShapeDtypeStruct((n, w), dt) for _, w, dt in outs]
    out_specs = [pl.BlockSpec((tm, w), row) for _, w, _ in outs]
    if with_kmean:
        assert tm == MOBA_BLOCK
        out_shape.append(jax.ShapeDtypeStruct((nt, 1, MOBA_W), F32))
        out_specs.append(pl.BlockSpec((None, 1, MOBA_W), lambda i: (i, 0, 0)))
    res = pl.pallas_call(
        _inproj_kernel,
        grid=(nt,),
        in_specs=[pl.BlockSpec((tm, d), row), mod_spec, mod_spec,
                  pl.BlockSpec(w_r.shape, lambda i: (0, 0)), tab_spec, tab_spec, tab_spec],
        out_specs=out_specs,
        out_shape=out_shape,
        compiler_params=_cparams(("parallel",), 48),
        name="inproj",
    )(x, sc, sh, w_r, *rope_tabs)
    return dict(zip([o[0] for o in outs] + (["kmean"] if with_kmean else []), res))


def _rope_tables(pos):
    inv = ROPE_THETA ** (-jnp.arange(ROPE_HALF, dtype=F32) / ROPE_HALF)
    ang = pos.astype(F32)[:, None] * inv
    cos, sin = jnp.cos(ang), jnp.sin(ang)
    rows = pos.shape[0]
    one = jnp.ones((rows, HEAD_DIM - ROPE_DIMS), F32)
    zero = jnp.zeros((rows, HEAD_DIM - ROPE_DIMS), F32)
    zh = jnp.zeros((rows, ROPE_HALF), F32)
    cs = jnp.concatenate([cos, cos, one], axis=1)
    s1 = jnp.concatenate([zh, sin, zero], axis=1)
    s2 = jnp.concatenate([-sin, zh, zero], axis=1)
    rep = LANES // HEAD_DIM
    return tuple(jnp.tile(t, (1, rep)) for t in (cs, s1, s2))


def _reorder_w_in(w_in):
    d = w_in.shape[0]
    sizes = (NSA_Q, 6 * NSA_KV, 3 * NSA_HEADS, MOBA_W, MOBA_W, MOBA_W, d, d)
    offs = [0]
    for s in sizes:
        offs.append(offs[-1] + s)
    q_a, kv_a, gate, q_b, k_b, v_b, mg_a, mg_b = [w_in[:, offs[i]:offs[i + 1]] for i in range(8)]
    gate = jnp.pad(gate, ((0, 0), (0, GATE_PAD - 3 * NSA_HEADS)))
    w_r = jnp.concatenate([q_a, kv_a, q_b, k_b, v_b, gate], axis=1).astype(MXU_DTYPE)
    w_mg = jnp.concatenate([mg_a, mg_b], axis=1).astype(MXU_DTYPE)
    return w_r, w_mg


_CHUNK_W = CMP_STRIDE * NSA_ROW


def _cmp1_kernel(*refs, n_src, n_prefetch=0):
    refs = refs[n_prefetch:]
    x_refs = refs[:n_src]
    wk_ref, wv_ref, pek_ref, pev_ref, o_ref = refs[n_src:]

    def gather(off):
        cols = []
        for s in range(CMP_STRIDE):
            lo = s * NSA_ROW + off
            pieces = [xr[:, lo:lo + NSA_KV] for xr in x_refs]
            cols.append(pieces[0] if n_src == 1 else jnp.concatenate(pieces, axis=0))
        return jnp.concatenate(cols, axis=1)

    xk = gather(0)
    xv = gather(NSA_KV)
    outs = []
    for r in range(CMP_RATIO):
        outs.append(_mm(xk + pek_ref[r], wk_ref[r]))
        outs.append(_mm(xv + pev_ref[r], wv_ref[r]))
    o_ref[...] = jnp.concatenate(outs, axis=1)


def _cmp2_kernel(a_ref, w2k_ref, w2v_ref, o_ref):
    a = a_ref[...]
    n = a.shape[0]
    w = NSA_KV
    hk = a[:, 0:w] + pltpu.roll(a[:, 2 * w:3 * w], n - 1, 0)
    hv = a[:, w:2 * w] + pltpu.roll(a[:, 3 * w:4 * w], n - 1, 0)
    o_ref[...] = jnp.concatenate([_mm(_silu(hk), w2k_ref[...]), _mm(_silu(hv), w2v_ref[...])], axis=1)


def _blockdiag(w, reps):
    k, n = w.shape
    out = jnp.zeros((reps * k, reps * n), w.dtype)
    for g in range(reps):
        out = out.at[g * k:(g + 1) * k, g * n:(g + 1) * n].set(w)
    return out


def _cmp_weights(w1, w2, pe):
    g = NSA_KV_HEADS
    w1s = jnp.stack([jnp.concatenate([_blockdiag(w1[r, s], g) for s in range(CMP_STRIDE)], axis=0)
                     for r in range(CMP_RATIO)]).astype(MXU_DTYPE)
    pes = jnp.tile(pe[:, :, None, :], (1, 1, g, 1)).reshape(CMP_RATIO, 1, CMP_STRIDE * NSA_KV)
    return w1s, _blockdiag(w2, g).astype(MXU_DTYPE), pes


def _cmp_stage2(a, w2k, w2v):
    b, nch, _ = a.shape
    return pl.pallas_call(
        _cmp2_kernel,
        grid=(b,),
        in_specs=[pl.BlockSpec((None, nch, 4 * NSA_KV), lambda i: (i, 0, 0)),
                  pl.BlockSpec(w2k.shape, lambda i: (0, 0)), pl.BlockSpec(w2v.shape, lambda i: (0, 0))],
        out_specs=pl.BlockSpec((None, nch, 2 * NSA_KV), lambda i: (i, 0, 0)),
        out_shape=jax.ShapeDtypeStruct((b, nch, 2 * NSA_KV), F32),
        compiler_params=_cparams(("parallel",)),
        name="cmp2",
    )(a, w2k, w2v)


def _cmp_prompt(nsa_rows, cw):
    b, t, _ = nsa_rows.shape
    nch = t // CMP_STRIDE
    tc = min(128, nch)
    x = nsa_rows.reshape(b, nch, _CHUNK_W)
    wk, w2k, pek, wv, w2v, pev = cw
    full = lambda a: pl.BlockSpec(a.shape, lambda i, j: (0,) * a.ndim)
    a = pl.pallas_call(
        functools.partial(_cmp1_kernel, n_src=1),
        grid=(b, nch // tc),
        in_specs=[pl.BlockSpec((None, tc, _CHUNK_W), lambda i, j: (i, j, 0)),
                  full(wk), full(wv), full(pek), full(pev)],
        out_specs=pl.BlockSpec((None, tc, 4 * NSA_KV), lambda i, j: (i, j, 0)),
        out_shape=jax.ShapeDtypeStruct((b, nch, 4 * NSA_KV), F32),
        compiler_params=_cparams(("parallel", "parallel"), 48),
        name="cmp1_prompt",
    )(x, wk, wv, pek, pev)
    return _cmp_stage2(a, w2k, w2v)


def _cmp_decode(cache_pages, page_table, cw):
    n_pool, page, _ = cache_pages.shape
    db, n_pages = page_table.shape
    cpp = page // CMP_STRIDE
    pps = PAGES_PER_STEP
    x = cache_pages.reshape(n_pool, cpp, _CHUNK_W)
    wk, w2k, pek, wv, w2v, pev = cw
    nch = n_pages * cpp
    full = lambda a: pl.BlockSpec(a.shape, lambda i, j, pt: (0,) * a.ndim)
    page_specs = [pl.BlockSpec((None, cpp, _CHUNK_W), functools.partial(
        lambda i, j, pt, p: (pt[i, j * pps + p], 0, 0), p=p)) for p in range(pps)]
    a = pl.pallas_call(
        functools.partial(_cmp1_kernel, n_src=pps, n_prefetch=1),
        grid_spec=pltpu.PrefetchScalarGridSpec(
            num_scalar_prefetch=1,
            grid=(db, n_pages // pps),
            in_specs=page_specs + [full(wk), full(wv), full(pek), full(pev)],
            out_specs=pl.BlockSpec((None, pps * cpp, 4 * NSA_KV), lambda i, j, pt: (i, j, 0)),
        ),
        out_shape=jax.ShapeDtypeStruct((db, nch, 4 * NSA_KV), F32),
        compiler_params=_cparams(("parallel", "parallel"), 48),
        name="cmp1_decode",
    )(page_table, *([x] * pps), wk, wv, pek, pev)
    return _cmp_stage2(a, w2k, w2v)


def _cmp_to_sel(nch, ns_pad):
    cs = jnp.arange(nch) * CMP_STRIDE
    ss = jnp.arange(ns_pad) * SEL_BLOCK
    return ((cs[:, None] < ss[None] + SEL_BLOCK) & (cs[:, None] + CMP_BLOCK > ss[None])).astype(F32)


def _nsa_prompt_kernel(qu_ref, qr_ref, g_ref, kcv_ref, c2s_ref, katt_ref, o_ref, *, qb, kc):
    s0 = pl.program_id(1) * qb
    tk = katt_ref.shape[0]
    nch = kcv_ref.shape[0]
    ns = c2s_ref.shape[1]
    n_sel = min(SEL_TOPN, ns)
    scale = HEAD_DIM ** -0.5
    hg = NSA_GROUP
    pos_q = s0 + _iota((qb, 1), 0)
    pos_rows = jnp.concatenate([pos_q] * hg, axis=0)
    gates = _sigmoid(g_ref[...])
    cend = _iota((1, nch), 1) * CMP_STRIDE + (CMP_BLOCK - 1)
    jsel = _iota((1, ns), 1)
    jsel_f = jsel.astype(F32)
    jq = lax.shift_right_logical(pos_q, SEL_BLOCK.bit_length() - 1)
    forced = (jsel == 0) | (jsel == jq) | (jsel == jq - 1)
    wlen = qb + WINDOW
    wstart = pl.multiple_of(jnp.clip(s0 - WINDOW, 0, tk - wlen), qb)
    wpos = wstart + _iota((1, wlen), 1)
    mask_w = (wpos <= pos_rows) & (wpos > pos_rows - WINDOW)
    n_kc = s0 // kc + 1
    out_heads = []
    for g in range(NSA_KV_HEADS):
        heads = [g * hg + h for h in range(hg)]
        qu = jnp.concatenate([qu_ref[:, h * HEAD_DIM:(h + 1) * HEAD_DIM] for h in heads], axis=0)
        qr = jnp.concatenate([qr_ref[:, h * HEAD_DIM:(h + 1) * HEAD_DIM] for h in heads], axis=0)
        glo = g * HEAD_DIM
        kcg = kcv_ref[:, glo:glo + HEAD_DIM]
        vcg = kcv_ref[:, NSA_KV + glo:NSA_KV + glo + HEAD_DIM]
        p_c = _masked_softmax(_mm_nt(qu, kcg) * scale, cend <= pos_rows)
        o_c = _mm(p_c, vcg)
        p_sum = p_c[0:qb]
        for h in range(1, hg):
            p_sum = p_sum + p_c[h * qb:(h + 1) * qb]
        imp = _mm(p_sum, c2s_ref[...])
        score = jnp.where(jsel <= jq, jnp.where(forced, BIG, imp), -1.0)
        sel = jnp.zeros((qb, ns), F32)
        for _ in range(n_sel):
            m, _, pick = _take_first_max(score, jsel_f, float(ns), 1)
            sel = jnp.where(pick & (m >= 0.0), 1.0, sel)
            score = jnp.where(pick, -2.0, score)
        sel_b = sel.astype(BF16)

        def sel_step(c, carry, qr=qr, glo=glo, sel_b=sel_b):
            m_i, l_i, acc = carry
            k0 = pl.multiple_of(c * kc, kc)
            kk = katt_ref[pl.ds(k0, kc), glo:glo + HEAD_DIM]
            vv = katt_ref[pl.ds(k0, kc), NSA_KV + glo:NSA_KV + glo + HEAD_DIM]
            s = _mm_nt(qr, kk) * scale
            kpos = k0 + _iota((1, kc), 1)
            expand = (_iota((ns, 1), 0) == lax.shift_right_logical(kpos, SEL_BLOCK.bit_length() - 1)).astype(BF16)
            chosen = jnp.dot(sel_b, expand, preferred_element_type=F32)
            chosen = jnp.where(kpos <= pos_q, chosen, 0.0)
            mask = jnp.concatenate([chosen] * hg, axis=0) > 0.5
            m_new = jnp.maximum(m_i, jnp.max(jnp.where(mask, s, NEG), axis=1, keepdims=True))
            alpha = jnp.exp(m_i - m_new)
            p = jnp.where(mask, jnp.exp(s - m_new), 0.0)
            l_new = alpha * l_i + jnp.sum(p, axis=1, keepdims=True)
            return m_new, l_new, alpha * acc + _mm(p, vv)

        init = (jnp.full((hg * qb, 1), NEG, F32), jnp.zeros((hg * qb, 1), F32),
                jnp.zeros((hg * qb, HEAD_DIM), F32))
        _, l_s, acc_s = lax.fori_loop(0, n_kc, sel_step, init)
        o_s = acc_s / jnp.where(l_s > 0, l_s, 1.0)
        kw = katt_ref[pl.ds(wstart, wlen), 2 * NSA_KV + glo:2 * NSA_KV + glo + HEAD_DIM]
        vw = katt_ref[pl.ds(wstart, wlen), 3 * NSA_KV + glo:3 * NSA_KV + glo + HEAD_DIM]
        o_w = _mm(_masked_softmax(_mm_nt(qr, kw) * scale, mask_w), vw)
        for hi, h in enumerate(heads):
            rows = slice(hi * qb, (hi + 1) * qb)
            out_heads.append(gates[:, 3 * h:3 * h + 1] * o_c[rows] + gates[:, 3 * h + 1:3 * h + 2] * o_s[rows]
                             + gates[:, 3 * h + 2:3 * h + 3] * o_w[rows])
    o_ref[...] = jnp.concatenate(out_heads, axis=1).astype(o_ref.dtype)


def _nsa_prompt(p, kcv, b, t):
    qb, kc = NSA_QB, NSA_KC
    assert t % kc == 0 and kc % qb == 0 and t >= qb + WINDOW and t % SEL_BLOCK == 0
    nch = kcv.shape[1]
    ns = t // SEL_BLOCK
    c2s = _cmp_to_sel(nch, ns).astype(BF16)
    r3 = lambda a: a.reshape(b, t, a.shape[-1])
    tile = lambda w: pl.BlockSpec((None, qb, w), lambda i, j: (i, j, 0))
    return pl.pallas_call(
        functools.partial(_nsa_prompt_kernel, qb=qb, kc=kc),
        grid=(b, t // qb),
        in_specs=[tile(NSA_Q), tile(NSA_Q), tile(GATE_PAD),
                  pl.BlockSpec((None, nch, 2 * NSA_KV), lambda i, j: (i, 0, 0)),
                  pl.BlockSpec((nch, ns), lambda i, j: (0, 0)),
                  pl.BlockSpec((None, t, 4 * NSA_KV), lambda i, j: (i, 0, 0))],
        out_specs=tile(NSA_Q),
        out_shape=jax.ShapeDtypeStruct((b, t, NSA_Q), BF16),
        compiler_params=_cparams(("parallel", "arbitrary"), 56),
        name="nsa_prompt",
    )(r3(p["qu"]), r3(p["qr"]), r3(p["gate"]), kcv, c2s, r3(p["katt"])).reshape(b * t, NSA_Q)


def _moba_prompt_kernel(q_ref, km_ref, k_ref, v_ref, o_ref, *, blk):
    jq = pl.program_id(2)
    nb = km_ref.shape[0]
    n_top = min(MOBA_TOPK, nb)
    scale = HEAD_DIM ** -0.5
    pos_q = jq * blk + _iota((blk, 1), 0)
    jb = _iota((1, nb), 1)
    jb_f = jb.astype(F32)
    outs = []
    for hh in range(LANES // HEAD_DIM):
        lo = hh * HEAD_DIM
        q = q_ref[:, lo:lo + HEAD_DIM]
        score = jnp.where(jb < jq, _mm_nt(q, km_ref[:, lo:lo + HEAD_DIM]), NEG)
        sel = jnp.zeros((blk, nb), F32)
        for _ in range(n_top):
            m, _, pick = _take_first_max(score, jb_f, float(nb), 1)
            sel = jnp.where(pick & (m > 0.5 * NEG), 1.0, sel)
            score = jnp.where(pick, 3.0 * NEG, score)
        sel_b = sel.astype(BF16)

        def update(carry, s, mask, vv):
            m_i, l_i, acc = carry
            m_new = jnp.maximum(m_i, jnp.max(jnp.where(mask, s, NEG), axis=1, keepdims=True))
            alpha = jnp.exp(m_i - m_new)
            p = jnp.where(mask, jnp.exp(s - m_new), 0.0)
            return m_new, alpha * l_i + jnp.sum(p, axis=1, keepdims=True), alpha * acc + _mm(p, vv)

        def past_step(j, carry, q=q, lo=lo, sel_b=sel_b):
            k0 = pl.multiple_of(j * blk, blk)
            kk = k_ref[pl.ds(k0, blk), lo:lo + HEAD_DIM]
            vv = v_ref[pl.ds(k0, blk), lo:lo + HEAD_DIM]
            onehot = jnp.broadcast_to(_iota((nb, 1), 0) == j, (nb, blk)).astype(BF16)
            chosen = jnp.dot(sel_b, onehot, preferred_element_type=F32)
            return update(carry, _mm_nt(q, kk) * scale, chosen > 0.5, vv)

        init = (jnp.full((blk, 1), NEG, F32), jnp.zeros((blk, 1), F32), jnp.zeros((blk, HEAD_DIM), F32))
        carry = lax.fori_loop(0, jq, past_step, init)
        k0 = pl.multiple_of(jq * blk, blk)
        kk = k_ref[pl.ds(k0, blk), lo:lo + HEAD_DIM]
        vv = v_ref[pl.ds(k0, blk), lo:lo + HEAD_DIM]
        kpos = jq * blk + _iota((1, blk), 1)
        _, l_f, acc = update(carry, _mm_nt(q, kk) * scale, kpos <= pos_q, vv)
        outs.append(acc / jnp.where(l_f > 0, l_f, 1.0))
    o_ref[...] = jnp.concatenate(outs, axis=1).astype(o_ref.dtype)


def _moba_prompt(p, b, t):
    blk = MOBA_BLOCK
    assert t % blk == 0
    nb = t // blk
    npair = MOBA_W // LANES
    q = p["qb"].reshape(b, t, MOBA_W)
    kv = p["kvb"].reshape(b, t, MOBA_ROW)
    km = p["kmean"].reshape(b, nb, MOBA_W)
    return pl.pallas_call(
        functools.partial(_moba_prompt_kernel, blk=blk),
        grid=(b, npair, nb),
        in_specs=[pl.BlockSpec((None, blk, LANES), lambda i, hp, j: (i, j, hp)),
                  pl.BlockSpec((None, nb, LANES), lambda i, hp, j: (i, 0, hp)),
                  pl.BlockSpec((None, t, LANES), lambda i, hp, j: (i, 0, hp)),
                  pl.BlockSpec((None, t, LANES), lambda i, hp, j: (i, 0, npair + hp))],
        out_specs=pl.BlockSpec((None, blk, LANES), lambda i, hp, j: (i, j, hp)),
        out_shape=jax.ShapeDtypeStruct((b, t, MOBA_W), BF16),
        compiler_params=_cparams(("parallel", "parallel", "arbitrary")),
        name="moba_prompt",
    )(q, km, kv, kv).reshape(b * t, MOBA_W)


def _group_rows(n_rows=NSA_HEADS):
    return _iota((n_rows, 1), 0) < NSA_GROUP


def _nsa_dec_cmp_kernel(qu_ref, kcv_ref, c2s_ref, oc_ref, idx_ref, *, pos):
    assert NSA_KV_HEADS == 2
    q = qu_ref[...]
    nch = kcv_ref.shape[0]
    ns = c2s_ref.shape[1]
    n_sel = min(SEL_TOPN, -(-(pos + 1) // SEL_BLOCK))
    scale = HEAD_DIM ** -0.5
    g0 = _group_rows()
    hd = HEAD_DIM
    kcv = kcv_ref[...]
    s = jnp.where(g0, _mm_nt(q, kcv[:, 0:hd]), _mm_nt(q, kcv[:, hd:2 * hd])) * scale
    cend = _iota((1, nch), 1) * CMP_STRIDE + (CMP_BLOCK - 1)
    p = _masked_softmax(s, cend <= pos)
    oc_ref[...] = jnp.where(g0, _mm(p, kcv[:, NSA_KV:NSA_KV + hd]), _mm(p, kcv[:, NSA_KV + hd:NSA_KV + 2 * hd]))
    p0 = jnp.sum(jnp.where(g0, p, 0.0), axis=0, keepdims=True)
    p1 = jnp.sum(jnp.where(g0, 0.0, p), axis=0, keepdims=True)
    imp = _mm(jnp.where(g0, p0, p1), c2s_ref[...])
    jsel = _iota((1, ns), 1)
    jsel_f = jsel.astype(F32)
    jq = pos // SEL_BLOCK
    forced = (jsel == 0) | (jsel == jq) | (jsel == jq - 1)
    score = jnp.where(jsel <= jq, jnp.where(forced, BIG, imp), -1.0)
    lane = _iota((1, LANES), 1)
    idx = jnp.full((NSA_HEADS, LANES), -1, I32)
    for it in range(n_sel):
        m, first, pick = _take_first_max(score, jsel_f, float(ns), 1)
        idx = jnp.where(lane == it, jnp.where(m >= 0.0, first, -1.0).astype(I32), idx)
        score = jnp.where(pick, -2.0, score)
    idx_ref[...] = idx


def _attend_with_new(q, keys, vals, valid, k_new, v_new, new_valid, scale):
    s = _mm_nt(q, keys) * scale
    qf = q.astype(MXU_DTYPE).astype(F32)
    s_new = jnp.sum(qf * k_new.astype(MXU_DTYPE).astype(F32), axis=1, keepdims=True) * scale
    m = jnp.maximum(jnp.max(jnp.where(valid, s, NEG), axis=1, keepdims=True), jnp.where(new_valid, s_new, NEG))
    p = jnp.where(valid, jnp.exp(s - m), 0.0)
    p_new = jnp.where(new_valid, jnp.exp(s_new - m), 0.0)
    d = jnp.sum(p, axis=1, keepdims=True) + p_new
    d = jnp.where(d > 0, d, 1.0)
    return _mm(p / d, vals) + (p_new / d).astype(MXU_DTYPE).astype(F32) * v_new.astype(MXU_DTYPE).astype(F32)


def _nsa_dec_att_kernel(idx_ref, pt_ref, qr_ref, gate_ref, oc_ref, new_ref, win_ref, *rest, pos, n_sel, past_blocks):
    del pt_ref
    blk_refs, o_ref = rest[:-1], rest[-1]
    b = pl.program_id(0)
    q = qr_ref[...]
    scale = HEAD_DIM ** -0.5
    hd = HEAD_DIM
    g0 = _group_rows()
    new = new_ref[...]
    n_win = win_ref.shape[0]
    wpos = pos - WINDOW + (WINDOW - n_win) + _iota((1, n_win), 1)
    valid_w = (wpos > pos - WINDOW) & (wpos >= 0)
    blk_of_key = lax.shift_right_logical(_iota((1, n_sel * SEL_BLOCK), 1), SEL_BLOCK.bit_length() - 1)
    o_s, o_w = [], []
    for g in range(NSA_KV_HEADS):
        glo = g * hd
        refs = blk_refs[g * n_sel:(g + 1) * n_sel]
        keys = jnp.concatenate([r[:, 2 * NSA_KV + glo:2 * NSA_KV + glo + hd] for r in refs], axis=0)
        vals = jnp.concatenate([r[:, 3 * NSA_KV + glo:3 * NSA_KV + glo + hd] for r in refs], axis=0)
        valid = jnp.zeros((1, n_sel * SEL_BLOCK), F32)
        new_valid = jnp.zeros((1, 1), F32)
        for j in range(n_sel):
            bj = idx_ref[b, g, j]
            valid = jnp.where(blk_of_key == j, ((bj >= 0) & (bj < past_blocks)).astype(F32), valid)
            new_valid = jnp.maximum(new_valid, (bj == past_blocks).astype(F32))
        o_s.append(_attend_with_new(q, keys, vals, valid > 0.5, new[:, glo:glo + hd],
                                    new[:, NSA_KV + glo:NSA_KV + glo + hd], new_valid > 0.5, scale))
        o_w.append(_attend_with_new(q, win_ref[:, glo:glo + hd], win_ref[:, NSA_KV + glo:NSA_KV + glo + hd], valid_w,
                                    new[:, 2 * NSA_KV + glo:2 * NSA_KV + glo + hd],
                                    new[:, 3 * NSA_KV + glo:3 * NSA_KV + glo + hd], True, scale))
    gs = _sigmoid(gate_ref[...])
    o_ref[...] = (gs[:, 0:1] * oc_ref[...] + gs[:, 1:2] * jnp.where(g0, o_s[0], o_s[1])
                  + gs[:, 2:3] * jnp.where(g0, o_w[0], o_w[1]))


def _nsa_decode_select(p, kcv, pos):
    db = kcv.shape[0]
    nch = kcv.shape[1]
    ns = pos // SEL_BLOCK + 1
    ns_pad = -(-ns // LANES) * LANES
    n_sel = min(SEL_TOPN, ns)
    c2s = _cmp_to_sel(nch, ns_pad).astype(BF16)
    per_q = lambda w: pl.BlockSpec((None, NSA_HEADS, w), lambda i: (i, 0, 0))
    o_c, idx = pl.pallas_call(
        functools.partial(_nsa_dec_cmp_kernel, pos=pos),
        grid=(db,),
        in_specs=[per_q(HEAD_DIM), pl.BlockSpec((None, nch, 2 * NSA_KV), lambda i: (i, 0, 0)),
                  pl.BlockSpec((nch, ns_pad), lambda i: (0, 0))],
        out_specs=[per_q(HEAD_DIM), per_q(LANES)],
        out_shape=[jax.ShapeDtypeStruct((db, NSA_HEADS, HEAD_DIM), F32),
                   jax.ShapeDtypeStruct((db, NSA_HEADS, LANES), I32)],
        compiler_params=_cparams(("parallel",)),
        name="nsa_dec_cmp",
    )(p["qu"].reshape(db, NSA_HEADS, HEAD_DIM), kcv, c2s)
    return o_c, idx[:, ::NSA_GROUP, :n_sel]


def _nsa_decode(p, kcv, cache_pages, win_state, page_table, pos):
    db, n_pages = page_table.shape
    n_pool, page, _ = cache_pages.shape
    assert pos == n_pages * page and pos % SEL_BLOCK == 0 and page % SEL_BLOCK == 0
    past_blocks = pos // SEL_BLOCK
    o_c, sel_idx = _nsa_decode_select(p, kcv, pos)
    n_sel = sel_idx.shape[2]
    heads3 = lambda a: a.reshape(db, NSA_HEADS, HEAD_DIM)
    bpp = page // SEL_BLOCK
    halves = cache_pages.reshape(n_pool * bpp, SEL_BLOCK, NSA_ROW)

    def blk_map(i, ix, pt, g, j):
        bj = jnp.clip(ix[i, g, j], 0, past_blocks - 1)
        return (pt[i, bj // bpp] * bpp + bj % bpp, 0, 0)

    blk_specs = [pl.BlockSpec((None, SEL_BLOCK, NSA_ROW), functools.partial(blk_map, g=g, j=j))
                 for g in range(NSA_KV_HEADS) for j in range(n_sel)]
    per_q2 = lambda w: pl.BlockSpec((None, NSA_HEADS, w), lambda i, ix, pt: (i, 0, 0))
    n_win = win_state.shape[1]
    gate3 = p["gate"][:, :3 * NSA_HEADS].reshape(db, NSA_HEADS, 3)
    o = pl.pallas_call(
        functools.partial(_nsa_dec_att_kernel, pos=pos, n_sel=n_sel, past_blocks=past_blocks),
        grid_spec=pltpu.PrefetchScalarGridSpec(
            num_scalar_prefetch=2,
            grid=(db,),
            in_specs=[per_q2(HEAD_DIM), per_q2(3), per_q2(HEAD_DIM),
                      pl.BlockSpec((None, 1, 4 * NSA_KV), lambda i, ix, pt: (i, 0, 0)),
                      pl.BlockSpec((None, n_win, 2 * NSA_KV), lambda i, ix, pt: (i, 0, 0))] + blk_specs,
            out_specs=per_q2(HEAD_DIM),
        ),
        out_shape=jax.ShapeDtypeStruct((db, NSA_HEADS, HEAD_DIM), F32),
        compiler_params=_cparams(("arbitrary",)),
        name="nsa_dec_att",
    )(sel_idx, page_table, heads3(p["qr"]), gate3, o_c, p["katt"].reshape(db, 1, 4 * NSA_KV),
      win_state.reshape(db, n_win, 2 * NSA_KV), *([halves] * (NSA_KV_HEADS * n_sel)))
    return o.reshape(db, NSA_Q)


def _moba_dec_mean_kernel(*refs, n_src, ppb):
    x_refs, o_ref = refs[1:1 + n_src], refs[1 + n_src]
    rows = []
    for i in range(n_src // ppb):
        tot = None
        for r in x_refs[i * ppb:(i + 1) * ppb]:
            sm = jnp.sum(r[...], axis=0, keepdims=True)
            tot = sm if tot is None else tot + sm
        rows.append(tot * (1.0 / (ppb * x_refs[0].shape[0])))
    o_ref[...] = jnp.concatenate(rows, axis=0)


def _moba_dec_gate_kernel(q_ref, km_ref, idx_ref, *, jq):
    q = q_ref[...]
    nb = km_ref.shape[0]
    n_top = min(MOBA_TOPK, nb)
    head_of_lane = lax.shift_right_logical(_iota((MOBA_HEADS, MOBA_W), 1), HEAD_DIM.bit_length() - 1)
    qbd = jnp.where(head_of_lane == _iota((MOBA_HEADS, MOBA_W), 0),
                    jnp.broadcast_to(q.astype(F32), (MOBA_HEADS, MOBA_W)), 0.0)
    jb = _iota((1, nb), 1)
    jb_f = jb.astype(F32)
    score = jnp.where(jb < jq, _mm_nt(qbd, km_ref[...]), NEG)
    lane = _iota((1, LANES), 1)
    idx = jnp.full((MOBA_HEADS, LANES), -1, I32)
    for it in range(n_top):
        m, first, pick = _take_first_max(score, jb_f, float(nb), 1)
        idx = jnp.where(lane == it, jnp.where(m > 0.5 * NEG, first, -1.0).astype(I32), idx)
        score = jnp.where(pick, 3.0 * NEG, score)
    idx_ref[...] = idx


def _moba_dec_att_kernel(idx_ref, pt_ref, q_ref, kn_ref, vn_ref, *rest, n_top, ppb):
    del pt_ref
    src, o_ref = rest[:-1], rest[-1]
    b, hp = pl.program_id(0), pl.program_id(1)
    hpl = LANES // HEAD_DIM
    scale = HEAD_DIM ** -0.5
    rows = 8
    head_of_lane = lax.shift_right_logical(_iota((rows, LANES), 1), HEAD_DIM.bit_length() - 1)
    q_all = jnp.broadcast_to(q_ref[...].astype(F32), (rows, LANES))
    k_new = kn_ref[...].astype(F32)
    v_new = vn_ref[...].astype(F32)
    out = jnp.zeros((rows, LANES), F32)
    for hh in range(hpl):
        mine = head_of_lane == hh
        q = jnp.where(mine, q_all, 0.0)
        s_list, v_list, ok_list = [], [], []
        for t in range(n_top):
            base = ((hh * n_top + t) * ppb) * 2
            keys = jnp.concatenate([src[base + 2 * pg][...] for pg in range(ppb)], axis=0)
            v_list.append(jnp.concatenate([src[base + 2 * pg + 1][...] for pg in range(ppb)], axis=0))
            s_list.append(_mm_nt(q, keys) * scale)
            ok_list.append(idx_ref[b, hp * hpl + hh, t] >= 0)
        s_new = jnp.sum(q * k_new, axis=1, keepdims=True) * scale
        m = s_new
        for s, ok in zip(s_list, ok_list):
            m = jnp.maximum(m, jnp.where(ok, jnp.max(s, axis=1, keepdims=True), NEG))
        p_new = jnp.exp(s_new - m)
        p_list = [jnp.where(ok, jnp.exp(s - m), 0.0) for s, ok in zip(s_list, ok_list)]
        d = p_new
        for p in p_list:
            d = d + jnp.sum(p, axis=1, keepdims=True)
        o = (p_new / d).astype(MXU_DTYPE).astype(F32) * v_new
        for p, v in zip(p_list, v_list):
            o = o + _mm(p / d, v)
        out = jnp.where(mine, o, out)
    o_ref[...] = out[0:1]


def _moba_decode(p, cache_pages, page_table, pos):
    db, n_pages = page_table.shape
    n_pool, page, _ = cache_pages.shape
    assert MOBA_BLOCK % page == 0 and pos % MOBA_BLOCK == 0 and pos == n_pages * page
    ppb = MOBA_BLOCK // page
    nb = pos // MOBA_BLOCK
    assert nb >= MOBA_TOPK
    pps = PAGES_PER_STEP
    bps = pps // ppb
    page_specs = [pl.BlockSpec((None, page, MOBA_W), functools.partial(
        lambda i, j, pt, pg: (pt[i, j * pps + pg], 0, 0), pg=pg)) for pg in range(pps)]
    kmean = pl.pallas_call(
        functools.partial(_moba_dec_mean_kernel, n_src=pps, ppb=ppb),
        grid_spec=pltpu.PrefetchScalarGridSpec(
            num_scalar_prefetch=1,
            grid=(db, n_pages // pps),
            in_specs=page_specs,
            out_specs=pl.BlockSpec((None, bps, MOBA_W), lambda i, j, pt: (i, j, 0)),
        ),
        out_shape=jax.ShapeDtypeStruct((db, nb, MOBA_W), F32),
        compiler_params=_cparams(("parallel", "parallel")),
        name="moba_dec_mean",
    )(page_table, *([cache_pages] * pps))
    idx = pl.pallas_call(
        functools.partial(_moba_dec_gate_kernel, jq=nb),
        grid=(db,),
        in_specs=[pl.BlockSpec((None, 1, MOBA_W), lambda i: (i, 0, 0)),
                  pl.BlockSpec((None, nb, MOBA_W), lambda i: (i, 0, 0))],
        out_specs=pl.BlockSpec((None, MOBA_HEADS, LANES), lambda i: (i, 0, 0)),
        out_shape=jax.ShapeDtypeStruct((db, MOBA_HEADS, LANES), I32),
        compiler_params=_cparams(("parallel",)),
        name="moba_dec_gate",
    )(p["qb"].reshape(db, 1, MOBA_W), kmean)
    n_top = min(MOBA_TOPK, nb)
    top_idx = idx[:, :, :n_top]
    hpl = LANES // HEAD_DIM
    npair = MOBA_HEADS // hpl

    def src_map(i, hp, ix, pt, hh, t, pg, kv):
        bj = jnp.clip(ix[i, hp * hpl + hh, t], 0, nb - 1)
        return (pt[i, bj * ppb + pg], 0, kv * npair + hp)

    src_specs = [pl.BlockSpec((None, page, LANES), functools.partial(src_map, hh=hh, t=t, pg=pg, kv=kv))
                 for hh in range(hpl) for t in range(n_top) for pg in range(ppb) for kv in range(2)]
    pair = lambda off: pl.BlockSpec((None, None, 1, LANES), lambda i, hp, ix, pt: (i, off + hp, 0, 0))
    o = pl.pallas_call(
        functools.partial(_moba_dec_att_kernel, n_top=n_top, ppb=ppb),
        grid_spec=pltpu.PrefetchScalarGridSpec(
            num_scalar_prefetch=2,
            grid=(db, npair),
            in_specs=[pair(0), pair(0), pair(npair)] + src_specs,
            out_specs=pair(0),
        ),
        out_shape=jax.ShapeDtypeStruct((db, npair, 1, LANES), F32),
        compiler_params=_cparams(("arbitrary", "arbitrary")),
        name="moba_dec_att",
    )(top_idx, page_table, p["qb"].reshape(db, npair, 1, LANES), p["kvb"].reshape(db, 2 * npair, 1, LANES),
      p["kvb"].reshape(db, 2 * npair, 1, LANES), *([cache_pages] * len(src_specs)))
    return o.reshape(db, MOBA_W)


def _route_t(s_t, b_t):
    n_e, n_tok = s_t.shape
    per = n_e // N_GROUPS
    biased = s_t + b_t
    sub_f = _iota((per, 1), 0).astype(F32)
    gscore = []
    for g in range(N_GROUPS):
        x = biased[g * per:(g + 1) * per]
        m1, _, pick = _take_first_max(x, sub_f, float(per), 0)
        gscore.append(m1 + jnp.max(jnp.where(pick, NEG, x), axis=0, keepdims=True))
    gs = jnp.concatenate(gscore, axis=0)
    g_f = _iota((N_GROUPS, 1), 0).astype(F32)
    gmask = jnp.zeros((N_GROUPS, n_tok), F32)
    for _ in range(TOPK_GROUPS):
        _, _, pick = _take_first_max(gs, g_f, float(N_GROUPS), 0)
        gmask = jnp.where(pick, 1.0, gmask)
        gs = jnp.where(pick, NEG, gs)
    masked = jnp.concatenate([jnp.where(gmask[g:g + 1] > 0.5, biased[g * per:(g + 1) * per], NEG)
                              for g in range(N_GROUPS)], axis=0)
    e_f = _iota((n_e, 1), 0).astype(F32)
    ids, ws = [], []
    for _ in range(TOP_K):
        _, first, pick = _take_first_max(masked, e_f, float(n_e), 0)
        ids.append(first)
        ws.append(jnp.sum(jnp.where(pick, s_t, 0.0), axis=0, keepdims=True))
        masked = jnp.where(pick, 3.0 * NEG, masked)
    w = jnp.concatenate(ws, axis=0)
    w = w / jnp.sum(w, axis=0, keepdims=True) * ROUTED_SCALE
    return jnp.concatenate(ids, axis=0).astype(I32), w


def _merge_kernel(x_ref, oa_ref, ob_ref, sc1_ref, sh1_ref, g1_ref, sc2_ref, sh2_ref, wmg_ref, wa_ref, wb_ref,
                  wo_ref, lg_ref, lb_ref, wr_ref, br_ref, x1_ref, h_ref, ti_ref, tw_ref, *, alpha):
    x = x_ref[...]
    d = x.shape[1]
    u = x * (1.0 + sc1_ref[...]) + sh1_ref[...]
    mg = _mm(u, wmg_ref[...])
    y_a = _mm(oa_ref[...], wa_ref[...])
    y_b = _mm(ob_ref[...], wb_ref[...])
    mix = _mm(_sigmoid(mg[:, :d]) * y_a + _sigmoid(mg[:, d:]) * y_b, wo_ref[...])
    x1 = _layer_norm(alpha * x + g1_ref[...] * mix, lg_ref[...], lb_ref[...])
    x1_ref[...] = x1
    h = x1 * (1.0 + sc2_ref[...]) + sh2_ref[...]
    h_ref[...] = h
    s_t = _sigmoid(_mm_nt(wr_ref[...], h))
    ti_ref[...], tw_ref[...] = _route_t(s_t, br_ref[...])


def _merge(x, o_a, o_b, mods, w, *, tm, per_token_mod, rows_per_batch, alpha):
    n, d = x.shape
    nt = n // tm
    tiles_per_batch = rows_per_batch // tm
    row = lambda i: (i, 0)
    if per_token_mod:
        mod_spec = pl.BlockSpec((tm, d), row)
    else:
        mod_spec = pl.BlockSpec((None, 1, d), lambda i: (i // tiles_per_batch, 0, 0))
    full = lambda a: pl.BlockSpec(a.shape, lambda i: (0,) * a.ndim)
    ws = [w["w_mg"], w["w_nsa_out"], w["w_moba_out"], w["w_o"], w["ln1_g"], w["ln1_b"], w["w_router_t"], w["b_router"]]
    return pl.pallas_call(
        functools.partial(_merge_kernel, alpha=alpha),
        grid=(nt,),
        in_specs=[pl.BlockSpec((tm, d), row), pl.BlockSpec((tm, NSA_Q), row), pl.BlockSpec((tm, MOBA_W), row)]
        + [mod_spec] * 5 + [full(a) for a in ws],
        out_specs=[pl.BlockSpec((tm, d), row), pl.BlockSpec((tm, d), row),
                   pl.BlockSpec((TOP_K, tm), lambda i: (0, i)), pl.BlockSpec((TOP_K, tm), lambda i: (0, i))],
        out_shape=[jax.ShapeDtypeStruct((n, d), F32), jax.ShapeDtypeStruct((n, d), F32),
                   jax.ShapeDtypeStruct((TOP_K, n), I32), jax.ShapeDtypeStruct((TOP_K, n), F32)],
        compiler_params=_cparams(("parallel",), 48),
        name="merge",
    )(x, o_a, o_b, *mods, *ws)


def _expert_kernel(be_ref, nu_ref, x_ref, wg_ref, wu_ref, wd_ref, y_ref):
    del be_ref

    @pl.when(pl.program_id(0) < nu_ref[0])
    def _():
        x = x_ref[...]
        y_ref[...] = _mm(_silu(_mm(x, wg_ref[...])) * _mm(x, wu_ref[...]), wd_ref[...])


def _dispatch(top_i, blk):
    k, n = top_i.shape
    a = k * n
    flat_e = top_i.reshape(-1)
    order = jnp.argsort(flat_e)
    e_sorted = flat_e[order]
    tok = (order % n).astype(I32)
    bounds = jnp.searchsorted(e_sorted, jnp.arange(N_EXPERTS + 1, dtype=I32), side="left").astype(I32)
    counts = bounds[1:] - bounds[:-1]
    padded = (counts + blk - 1) // blk * blk
    end_pad = jnp.cumsum(padded)
    start_pad = end_pad - padded
    dest = start_pad[e_sorted] + jnp.arange(a, dtype=I32) - bounds[:-1][e_sorted]
    nblk = -(-a // blk) + N_EXPERTS
    row_tok = jnp.zeros((nblk * blk,), I32).at[dest].set(tok)
    blk_e = jnp.minimum(jnp.searchsorted(end_pad, jnp.arange(nblk, dtype=I32) * blk, side="right"),
                        N_EXPERTS - 1).astype(I32)
    slot = jnp.zeros((a,), I32).at[order].set(dest).reshape(k, n)
    n_used = (end_pad[-1] // blk).astype(I32).reshape(1)
    return row_tok, blk_e, slot, n_used, nblk


def _experts(h, top_i, top_w, w):
    n, d = h.shape
    blk = min(EXPERT_BLK, max(8, (TOP_K * n) // N_EXPERTS))
    row_tok, blk_e, slot, n_used, nblk = _dispatch(top_i, blk)
    x_sorted = jnp.take(h, row_tok, axis=0)
    de = w["w_exp_gate"].shape[2]
    y = pl.pallas_call(
        _expert_kernel,
        grid_spec=pltpu.PrefetchScalarGridSpec(
            num_scalar_prefetch=2,
            grid=(nblk,),
            in_specs=[pl.BlockSpec((blk, d), lambda i, be, nu: (i, 0)),
                      pl.BlockSpec((None, d, de), lambda i, be, nu: (be[i], 0, 0)),
                      pl.BlockSpec((None, d, de), lambda i, be, nu: (be[i], 0, 0)),
                      pl.BlockSpec((None, de, d), lambda i, be, nu: (be[i], 0, 0))],
            out_specs=pl.BlockSpec((blk, d), lambda i, be, nu: (i, 0)),
        ),
        out_shape=jax.ShapeDtypeStruct((nblk * blk, d), F32),
        compiler_params=_cparams(("arbitrary",), 48),
        name="experts",
    )(blk_e, n_used, x_sorted, w["w_exp_gate"], w["w_exp_up"], w["w_exp_down"])
    routed = jnp.zeros((n, d), F32)
    for k in range(TOP_K):
        routed = routed + jnp.take(y, slot[k], axis=0) * top_w[k][:, None]
    return routed


def _final_kernel(x1_ref, h_ref, r_ref, g2_ref, wg_ref, wu_ref, wd_ref, lg_ref, lb_ref, o_ref, *, alpha):
    h = h_ref[...]
    f = r_ref[...] + _mm(_silu(_mm(h, wg_ref[...])) * _mm(h, wu_ref[...]), wd_ref[...])
    o_ref[...] = _layer_norm(alpha * x1_ref[...] + g2_ref[...] * f, lg_ref[...], lb_ref[...])


def _final(x1, h, routed, g2, w, *, tm, per_token_mod, rows_per_batch, alpha):
    n, d = x1.shape
    tiles_per_batch = rows_per_batch // tm
    row = lambda i: (i, 0)
    tile = pl.BlockSpec((tm, d), row)
    mod_spec = tile if per_token_mod else pl.BlockSpec((None, 1, d), lambda i: (i // tiles_per_batch, 0, 0))
    full = lambda a: pl.BlockSpec(a.shape, lambda i: (0,) * a.ndim)
    ws = [w["w_sh_gate"], w["w_sh_up"], w["w_sh_down"], w["ln2_g"], w["ln2_b"]]
    return pl.pallas_call(
        functools.partial(_final_kernel, alpha=alpha),
        grid=(n // tm,),
        in_specs=[tile, tile, tile, mod_spec] + [full(a) for a in ws],
        out_specs=tile,
        out_shape=jax.ShapeDtypeStruct((n, d), F32),
        compiler_params=_cparams(("parallel",)),
        name="final",
    )(x1, h, routed, g2, *ws)


def _prep_weights(lp):
    (w_ada, b_ada, w_in, cmp_k_w1, cmp_k_w2, cmp_k_pe, cmp_v_w1, cmp_v_w2, cmp_v_pe, w_nsa_out, w_moba_out, w_o,
     ln1_g, ln1_b, w_router, b_router, w_exp_gate, w_exp_up, w_exp_down, w_sh_gate, w_sh_up, w_sh_down,
     ln2_g, ln2_b) = lp
    c = lambda a: a.astype(MXU_DTYPE)
    row = lambda a: a.reshape(1, -1)
    w_r, w_mg = _reorder_w_in(w_in)
    wk, w2k, pek = _cmp_weights(cmp_k_w1, cmp_k_w2, cmp_k_pe)
    wv, w2v, pev = _cmp_weights(cmp_v_w1, cmp_v_w2, cmp_v_pe)
    return dict(w_ada=w_ada, b_ada=b_ada, w_r=w_r, w_mg=w_mg, cmp=(wk, w2k, pek, wv, w2v, pev),
                w_nsa_out=c(w_nsa_out), w_moba_out=c(w_moba_out), w_o=c(w_o), ln1_g=row(ln1_g), ln1_b=row(ln1_b),
                w_router_t=c(w_router.T), b_router=b_router.reshape(-1, 1),
                w_exp_gate=c(w_exp_gate), w_exp_up=c(w_exp_up), w_exp_down=c(w_exp_down),
                w_sh_gate=c(w_sh_gate), w_sh_up=c(w_sh_up), w_sh_down=c(w_sh_down), ln2_g=row(ln2_g), ln2_b=row(ln2_b))


def _token_tail(x, o_a, o_b, mods, w, *, tm, per_token_mod, rows_per_batch, alpha):
    sc1, sh1, g1, sh2, sc2, g2 = mods
    kw = dict(tm=tm, per_token_mod=per_token_mod, rows_per_batch=rows_per_batch, alpha=alpha)
    x1, h, top_i, top_w = _merge(x, o_a, o_b, (sc1, sh1, g1, sc2, sh2), w, **kw)
    routed = _experts(h, top_i, top_w, w)
    return _final(x1, h, routed, g2, w, **kw)


def _layer(xp, xs, c_all, cache_nsa_l, cache_moba_l, win_state_l, page_table, w, alpha):
    b, t, d = xp.shape
    db, ts, _ = xs.shape
    assert ts == 1
    page = cache_nsa_l.shape[1]
    pos = page_table.shape[1] * page
    assert win_state_l.shape[1] == WINDOW and t >= WINDOW
    mod = _ada(c_all, w["w_ada"], w["b_ada"])
    pieces = [mod[:, i * d:(i + 1) * d] for i in range(6)]
    mods_p = [m[:b].reshape(b, 1, d) for m in pieces]
    mods_s = [m[b:b + db] for m in pieces]
    order = lambda m: (m[1], m[0], m[2], m[3], m[4], m[5])
    mods_p, mods_s = order(mods_p), order(mods_s)

    tm = TOKEN_TILE
    xp2 = xp.reshape(b * t, d)
    pp = _inproj(xp2, mods_p[0], mods_p[1], w["w_r"], _rope_tables(jnp.arange(t)), tm=tm, per_token_mod=False,
                 rows_per_batch=t, with_kmean=True)
    kcv_p = _cmp_prompt(pp["nsa"].reshape(b, t, NSA_ROW), w["cmp"])
    oa_p = _nsa_prompt(pp, kcv_p, b, t)
    ob_p = _moba_prompt(pp, b, t)
    yp = _token_tail(xp2, oa_p, ob_p, mods_p, w, tm=tm, per_token_mod=False, rows_per_batch=t, alpha=alpha)

    xs2 = xs.reshape(db, d)
    ps = _inproj(xs2, mods_s[0], mods_s[1], w["w_r"], _rope_tables(jnp.full((db,), pos)), tm=db, per_token_mod=True,
                 rows_per_batch=db, with_kmean=False)
    nsa_pages = cache_nsa_l.reshape(cache_nsa_l.shape[0], page, NSA_ROW)
    moba_pages = cache_moba_l.reshape(cache_moba_l.shape[0], page, MOBA_ROW)
    kcv_s = _cmp_decode(nsa_pages, page_table, w["cmp"])
    oa_s = _nsa_decode(ps, kcv_s, nsa_pages, win_state_l, page_table, pos)
    ob_s = _moba_decode(ps, moba_pages, page_table, pos)
    ys = _token_tail(xs2, oa_s, ob_s, mods_s, w, tm=db, per_token_mod=True, rows_per_batch=db, alpha=alpha)

    g, hd = NSA_KV_HEADS, HEAD_DIM
    win_p = pp["win"].reshape(b, t, 2, g, hd)[:, t - WINDOW:]
    win_s = jnp.concatenate([win_state_l[:, 1:], ps["win"].reshape(db, 1, 2, g, hd)], axis=1)
    return (yp.reshape(b, t, d), ys.reshape(db, 1, d),
            pp["nsa"].reshape(b, t, 4, g, hd), ps["nsa"].reshape(db, 1, 4, g, hd),
            pp["moba"].reshape(b, t, 2, MOBA_HEADS, hd), ps["moba"].reshape(db, 1, 2, MOBA_HEADS, hd), win_p, win_s)


def kernel(x_prompt, x_sample, cache_nsa, cache_moba, state_nsa_win, page_table, c_prompt, c_sample, w_ada, b_ada,
           w_in, cmp_k_w1, cmp_k_w2, cmp_k_pe, cmp_v_w1, cmp_v_w2, cmp_v_pe, w_nsa_out, w_moba_out, w_o, ln1_g,
           ln1_b, w_router, b_router, w_exp_gate, w_exp_up, w_exp_down, w_sh_gate, w_sh_up, w_sh_down, ln2_g,
           ln2_b):
    params = (w_ada, b_ada, w_in, cmp_k_w1, cmp_k_w2, cmp_k_pe, cmp_v_w1, cmp_v_w2, cmp_v_pe, w_nsa_out, w_moba_out,
              w_o, ln1_g, ln1_b, w_router, b_router, w_exp_gate, w_exp_up, w_exp_down, w_sh_gate, w_sh_up,
              w_sh_down, ln2_g, ln2_b)
    depth = w_ada.shape[0]
    alpha = (2 * depth) ** 0.25
    b, db = x_prompt.shape[0], x_sample.shape[0]
    rows = -(-(b + db) // 8) * 8
    c_all = jnp.pad(jnp.concatenate([c_prompt, c_sample], axis=0), ((0, rows - b - db), (0, 0)))
    xp, xs = x_prompt, x_sample
    outs = [[] for _ in range(6)]
    for l in range(depth):
        w = _prep_weights([p[l] for p in params])
        res = _layer(xp, xs, c_all, cache_nsa[l], cache_moba[l], state_nsa_win[l], page_table, w, alpha)
        xp, xs = res[0], res[1]
        for acc, r in zip(outs, res[2:]):
            acc.append(r)
    return (xp, xs) + tuple(jnp.stack(o) for o in outs)
```

```python
import functools

import jax
import jax.numpy as jnp
from jax import lax
from jax.experimental import pallas as pl
from jax.experimental.pallas import tpu as pltpu

F32 = jnp.float32
BF16 = jnp.bfloat16
I32 = jnp.int32
MXU_DTYPE = jnp.bfloat16

HEAD_DIM = 64
ROPE_DIMS = HEAD_DIM // 4
ROPE_HALF = ROPE_DIMS // 2
ROPE_THETA = 500000.0
NSA_HEADS = 8
NSA_KV_HEADS = 2
NSA_GROUP = NSA_HEADS // NSA_KV_HEADS
CMP_BLOCK = 32
CMP_STRIDE = 16
CMP_RATIO = CMP_BLOCK // CMP_STRIDE
SEL_BLOCK = 64
SEL_TOPN = 16
WINDOW = 512
MOBA_HEADS = 8
MOBA_BLOCK = 256
MOBA_TOPK = 3
N_EXPERTS = 64
TOP_K = 8
N_GROUPS = 8
TOPK_GROUPS = 4
ROUTED_SCALE = 2.5
LN_EPS = 1e-5

LANES = 128
NSA_Q = NSA_HEADS * HEAD_DIM
NSA_KV = NSA_KV_HEADS * HEAD_DIM
MOBA_W = MOBA_HEADS * HEAD_DIM
NSA_ROW = 4 * NSA_KV
MOBA_ROW = 2 * MOBA_W
GATE_PAD = LANES

NEG = -1e30
BIG = 3e38
TOKEN_TILE = 256
NSA_QB = 128
NSA_KC = 512
EXPERT_BLK = 256
PAGES_PER_STEP = 16


def _sigmoid(x):
    return 1.0 / (1.0 + jnp.exp(-x))


def _silu(x):
    return x * _sigmoid(x)


def _mm(a, b):
    return jnp.dot(a.astype(MXU_DTYPE), b.astype(MXU_DTYPE), preferred_element_type=F32)


def _mm_nt(a, b):
    return lax.dot_general(a.astype(MXU_DTYPE), b.astype(MXU_DTYPE), (((1,), (1,)), ((), ())),
                           preferred_element_type=F32)


def _iota(shape, axis):
    return lax.broadcasted_iota(I32, shape, axis)


def _cparams(sem, vmem_mb=None):
    kw = dict(dimension_semantics=sem)
    if vmem_mb is not None:
        kw["vmem_limit_bytes"] = vmem_mb << 20
    return pltpu.CompilerParams(**kw)


def _masked_softmax(s, mask):
    m = jnp.max(jnp.where(mask, s, NEG), axis=-1, keepdims=True)
    e = jnp.where(mask, jnp.exp(s - m), 0.0)
    d = jnp.sum(e, axis=-1, keepdims=True)
    return e / jnp.where(d > 0, d, 1.0)


def _layer_norm(z, g, b):
    mu = jnp.mean(z, axis=-1, keepdims=True)
    zc = z - mu
    var = jnp.mean(zc * zc, axis=-1, keepdims=True)
    return zc * lax.rsqrt(var + LN_EPS) * g + b


def _take_first_max(score, idx_f, n_f, axis):
    m = jnp.max(score, axis=axis, keepdims=True)
    first = jnp.min(jnp.where(score == m, idx_f, n_f), axis=axis, keepdims=True)
    return m, first, idx_f == first


def _ada_kernel(c_ref, w_ref, b_ref, o_ref):
    a = _silu(c_ref[...])
    o_ref[...] = _mm(a, w_ref[...]) + b_ref[...]


def _ada(c_all, w_ada, b_ada):
    r, d = c_all.shape
    e6 = w_ada.shape[1]
    tn = 1024
    return pl.pallas_call(
        _ada_kernel,
        grid=(e6 // tn,),
        in_specs=[pl.BlockSpec((r, d), lambda j: (0, 0)),
                  pl.BlockSpec((d, tn), lambda j: (0, j)),
                  pl.BlockSpec((1, tn), lambda j: (0, j))],
        out_specs=pl.BlockSpec((r, tn), lambda j: (0, j)),
        out_shape=jax.ShapeDtypeStruct((r, e6), F32),
        compiler_params=_cparams(("arbitrary",)),
        name="ada",
    )(c_all, w_ada, b_ada.reshape(1, e6))


_C_QA = 0
_C_KVA = _C_QA + NSA_Q
_C_QB = _C_KVA + 6 * NSA_KV
_C_KB = _C_QB + MOBA_W
_C_VB = _C_KB + MOBA_W
_C_GATE = _C_VB + MOBA_W
_C_END = _C_GATE + GATE_PAD


def _rope(x, cs, s1, s2):
    parts = []
    for j in range(x.shape[1] // LANES):
        xj = x[:, j * LANES:(j + 1) * LANES]
        parts.append(xj * cs + pltpu.roll(xj, ROPE_HALF, 1) * s1 + pltpu.roll(xj, LANES - ROPE_HALF, 1) * s2)
    return parts[0] if len(parts) == 1 else jnp.concatenate(parts, axis=1)


def _inproj_kernel(x_ref, sc_ref, sh_ref, w_ref, cs_ref, s1_ref, s2_ref,
                   qu_ref, qr_ref, qb_ref, nsa_ref, katt_ref, win_ref, moba_ref, kvb_ref, g_ref, *km_ref):
    u = x_ref[...] * (1.0 + sc_ref[...]) + sh_ref[...]
    r = _mm(u, w_ref[...])
    cs, s1, s2 = cs_ref[...], s1_ref[...], s2_ref[...]
    rope = lambda v: _rope(v, cs, s1, s2)
    qa = r[:, _C_QA:_C_KVA]
    qu_ref[...] = qa.astype(BF16)
    qr_ref[...] = rope(qa).astype(BF16)
    o = _C_KVA
    kc_vc = r[:, o:o + 2 * NSA_KV]
    ks = rope(r[:, o + 2 * NSA_KV:o + 3 * NSA_KV])
    vs = r[:, o + 3 * NSA_KV:o + 4 * NSA_KV]
    kw = rope(r[:, o + 4 * NSA_KV:o + 5 * NSA_KV])
    vw = r[:, o + 5 * NSA_KV:o + 6 * NSA_KV]
    nsa_ref[...] = jnp.concatenate([kc_vc, ks, vs], axis=1)
    win_ref[...] = jnp.concatenate([kw, vw], axis=1)
    katt_ref[...] = jnp.concatenate([ks, vs, kw, vw], axis=1).astype(BF16)
    qb_ref[...] = rope(r[:, _C_QB:_C_KB]).astype(BF16)
    kb = rope(r[:, _C_KB:_C_VB])
    mb = jnp.concatenate([kb, r[:, _C_VB:_C_GATE]], axis=1)
    moba_ref[...] = mb
    kvb_ref[...] = mb.astype(BF16)
    g_ref[...] = r[:, _C_GATE:_C_END]
    if km_ref:
        km_ref[0][...] = jnp.sum(kb, axis=0, keepdims=True) * (1.0 / kb.shape[0])


def _inproj(x, sc, sh, w_r, rope_tabs, *, tm, per_token_mod, rows_per_batch, with_kmean):
    n, d = x.shape
    nt = n // tm
    tiles_per_batch = rows_per_batch // tm
    row = lambda i: (i, 0)
    if per_token_mod:
        mod_spec = pl.BlockSpec((tm, d), row)
    else:
        mod_spec = pl.BlockSpec((None, 1, d), lambda i: (i // tiles_per_batch, 0, 0))
    tab_spec = pl.BlockSpec((tm, LANES), lambda i: (i % tiles_per_batch, 0))
    outs = [("qu", NSA_Q, BF16), ("qr", NSA_Q, BF16), ("qb", MOBA_W, BF16), ("nsa", NSA_ROW, F32),
            ("katt", 4 * NSA_KV, BF16), ("win", 2 * NSA_KV, F32), ("moba", MOBA_ROW, F32),
            ("kvb", MOBA_ROW, BF16), ("gate", GATE_PAD, F32)]
    out_shape = [jax.ShapeDtypeStruct((n, w), dt) for _, w, dt in outs]
    out_specs = [pl.BlockSpec((tm, w), row) for _, w, _ in outs]
    if with_kmean:
        assert tm == MOBA_BLOCK
        out_shape.append(jax.ShapeDtypeStruct((nt, 1, MOBA_W), F32))
        out_specs.append(pl.BlockSpec((None, 1, MOBA_W), lambda i: (i, 0, 0)))
    res = pl.pallas_call(
        _inproj_kernel,
        grid=(nt,),
        in_specs=[pl.BlockSpec((tm, d), row), mod_spec, mod_spec,
                  pl.BlockSpec(w_r.shape, lambda i: (0, 0)), tab_spec, tab_spec, tab_spec],
        out_specs=out_specs,
        out_shape=out_shape,
        compiler_params=_cparams(("parallel",), 48),
        name="inproj",
    )(x, sc, sh, w_r, *rope_tabs)
    return dict(zip([o[0] for o in outs] + (["kmean"] if with_kmean else []), res))


def _rope_tables(pos):
    inv = ROPE_THETA ** (-jnp.arange(ROPE_HALF, dtype=F32) / ROPE_HALF)
    ang = pos.astype(F32)[:, None] * inv
    cos, sin = jnp.cos(ang), jnp.sin(ang)
    rows = pos.shape[0]
    one = jnp.ones((rows, HEAD_DIM - ROPE_DIMS), F32)
    zero = jnp.zeros((rows, HEAD_DIM - ROPE_DIMS), F32)
    zh = jnp.zeros((rows, ROPE_HALF), F32)
    cs = jnp.concatenate([cos, cos, one], axis=1)
    s1 = jnp.concatenate([zh, sin, zero], axis=1)
    s2 = jnp.concatenate([-sin, zh, zero], axis=1)
    rep = LANES // HEAD_DIM
    return tuple(jnp.tile(t, (1, rep)) for t in (cs, s1, s2))


def _reorder_w_in(w_in):
    d = w_in.shape[0]
    sizes = (NSA_Q, 6 * NSA_KV, 3 * NSA_HEADS, MOBA_W, MOBA_W, MOBA_W, d, d)
    offs = [0]
    for s in sizes:
        offs.append(offs[-1] + s)
    q_a, kv_a, gate, q_b, k_b, v_b, mg_a, mg_b = [w_in[:, offs[i]:offs[i + 1]] for i in range(8)]
    gate = jnp.pad(gate, ((0, 0), (0, GATE_PAD - 3 * NSA_HEADS)))
    w_r = jnp.concatenate([q_a, kv_a, q_b, k_b, v_b, gate], axis=1).astype(MXU_DTYPE)
    w_mg = jnp.concatenate([mg_a, mg_b], axis=1).astype(MXU_DTYPE)
    return w_r, w_mg


_CHUNK_W = CMP_STRIDE * NSA_ROW


def _cmp1_kernel(*refs, n_src, n_prefetch=0):
    refs = refs[n_prefetch:]
    x_refs = refs[:n_src]
    wk_ref, wv_ref, pek_ref, pev_ref, o_ref = refs[n_src:]

    def gather(off):
        cols = []
        for s in range(CMP_STRIDE):
            lo = s * NSA_ROW + off
            pieces = [xr[:, lo:lo + NSA_KV] for xr in x_refs]
            cols.append(pieces[0] if n_src == 1 else jnp.concatenate(pieces, axis=0))
        return jnp.concatenate(cols, axis=1)

    xk = gather(0)
    xv = gather(NSA_KV)
    outs = []
    for r in range(CMP_RATIO):
        outs.append(_mm(xk + pek_ref[r], wk_ref[r]))
        outs.append(_mm(xv + pev_ref[r], wv_ref[r]))
    o_ref[...] = jnp.concatenate(outs, axis=1)


def _cmp2_kernel(a_ref, w2k_ref, w2v_ref, o_ref):
    a = a_ref[...]
    n = a.shape[0]
    w = NSA_KV
    hk = a[:, 0:w] + pltpu.roll(a[:, 2 * w:3 * w], n - 1, 0)
    hv = a[:, w:2 * w] + pltpu.roll(a[:, 3 * w:4 * w], n - 1, 0)
    o_ref[...] = jnp.concatenate([_mm(_silu(hk), w2k_ref[...]), _mm(_silu(hv), w2v_ref[...])], axis=1)


def _blockdiag(w, reps):
    k, n = w.shape
    out = jnp.zeros((reps * k, reps * n), w.dtype)
    for g in range(reps):
        out = out.at[g * k:(g + 1) * k, g * n:(g + 1) * n].set(w)
    return out


def _cmp_weights(w1, w2, pe):
    g = NSA_KV_HEADS
    w1s = jnp.stack([jnp.concatenate([_blockdiag(w1[r, s], g) for s in range(CMP_STRIDE)], axis=0)
                     for r in range(CMP_RATIO)]).astype(MXU_DTYPE)
    pes = jnp.tile(pe[:, :, None, :], (1, 1, g, 1)).reshape(CMP_RATIO, 1, CMP_STRIDE * NSA_KV)
    return w1s, _blockdiag(w2, g).astype(MXU_DTYPE), pes


def _cmp_stage2(a, w2k, w2v):
    b, nch, _ = a.shape
    return pl.pallas_call(
        _cmp2_kernel,
        grid=(b,),
        in_specs=[pl.BlockSpec((None, nch, 4 * NSA_KV), lambda i: (i, 0, 0)),
                  pl.BlockSpec(w2k.shape, lambda i: (0, 0)), pl.BlockSpec(w2v.shape, lambda i: (0, 0))],
        out_specs=pl.BlockSpec((None, nch, 2 * NSA_KV), lambda i: (i, 0, 0)),
        out_shape=jax.ShapeDtypeStruct((b, nch, 2 * NSA_KV), F32),
        compiler_params=_cparams(("parallel",)),
        name="cmp2",
    )(a, w2k, w2v)


def _cmp_prompt(nsa_rows, cw):
    b, t, _ = nsa_rows.shape
    nch = t // CMP_STRIDE
    tc = min(128, nch)
    x = nsa_rows.reshape(b, nch, _CHUNK_W)
    wk, w2k, pek, wv, w2v, pev = cw
    full = lambda a: pl.BlockSpec(a.shape, lambda i, j: (0,) * a.ndim)
    a = pl.pallas_call(
        functools.partial(_cmp1_kernel, n_src=1),
        grid=(b, nch // tc),
        in_specs=[pl.BlockSpec((None, tc, _CHUNK_W), lambda i, j: (i, j, 0)),
                  full(wk), full(wv), full(pek), full(pev)],
        out_specs=pl.BlockSpec((None, tc, 4 * NSA_KV), lambda i, j: (i, j, 0)),
        out_shape=jax.ShapeDtypeStruct((b, nch, 4 * NSA_KV), F32),
        compiler_params=_cparams(("parallel", "parallel"), 48),
        name="cmp1_prompt",
    )(x, wk, wv, pek, pev)
    return _cmp_stage2(a, w2k, w2v)


def _cmp_decode(cache_pages, page_table, cw):
    n_pool, page, _ = cache_pages.shape
    db, n_pages = page_table.shape
    cpp = page // CMP_STRIDE
    pps = PAGES_PER_STEP
    x = cache_pages.reshape(n_pool, cpp, _CHUNK_W)
    wk, w2k, pek, wv, w2v, pev = cw
    nch = n_pages * cpp
    full = lambda a: pl.BlockSpec(a.shape, lambda i, j, pt: (0,) * a.ndim)
    page_specs = [pl.BlockSpec((None, cpp, _CHUNK_W), functools.partial(
        lambda i, j, pt, p: (pt[i, j * pps + p], 0, 0), p=p)) for p in range(pps)]
    a = pl.pallas_call(
        functools.partial(_cmp1_kernel, n_src=pps, n_prefetch=1),
        grid_spec=pltpu.PrefetchScalarGridSpec(
            num_scalar_prefetch=1,
            grid=(db, n_pages // pps),
            in_specs=page_specs + [full(wk), full(wv), full(pek), full(pev)],
            out_specs=pl.BlockSpec((None, pps * cpp, 4 * NSA_KV), lambda i, j, pt: (i, j, 0)),
        ),
        out_shape=jax.ShapeDtypeStruct((db, nch, 4 * NSA_KV), F32),
        compiler_params=_cparams(("parallel", "parallel"), 48),
        name="cmp1_decode",
    )(page_table, *([x] * pps), wk, wv, pek, pev)
    return _cmp_stage2(a, w2k, w2v)


def _cmp_to_sel(nch, ns_pad):
    cs = jnp.arange(nch) * CMP_STRIDE
    ss = jnp.arange(ns_pad) * SEL_BLOCK
    return ((cs[:, None] < ss[None] + SEL_BLOCK) & (cs[:, None] + CMP_BLOCK > ss[None])).astype(F32)


def _nsa_prompt_kernel(qu_ref, qr_ref, g_ref, kcv_ref, c2s_ref, katt_ref, o_ref, *, qb, kc):
    s0 = pl.program_id(1) * qb
    tk = katt_ref.shape[0]
    nch = kcv_ref.shape[0]
    ns = c2s_ref.shape[1]
    n_sel = min(SEL_TOPN, ns)
    scale = HEAD_DIM ** -0.5
    hg = NSA_GROUP
    pos_q = s0 + _iota((qb, 1), 0)
    pos_rows = jnp.concatenate([pos_q] * hg, axis=0)
    gates = _sigmoid(g_ref[...])
    cend = _iota((1, nch), 1) * CMP_STRIDE + (CMP_BLOCK - 1)
    jsel = _iota((1, ns), 1)
    jsel_f = jsel.astype(F32)
    jq = lax.shift_right_logical(pos_q, SEL_BLOCK.bit_length() - 1)
    forced = (jsel == 0) | (jsel == jq) | (jsel == jq - 1)
    wlen = qb + WINDOW
    wstart = pl.multiple_of(jnp.clip(s0 - WINDOW, 0, tk - wlen), qb)
    wpos = wstart + _iota((1, wlen), 1)
    mask_w = (wpos <= pos_rows) & (wpos > pos_rows - WINDOW)
    n_kc = s0 // kc + 1
    out_heads = []
    for g in range(NSA_KV_HEADS):
        heads = [g * hg + h for h in range(hg)]
        qu = jnp.concatenate([qu_ref[:, h * HEAD_DIM:(h + 1) * HEAD_DIM] for h in heads], axis=0)
        qr = jnp.concatenate([qr_ref[:, h * HEAD_DIM:(h + 1) * HEAD_DIM] for h in heads], axis=0)
        glo = g * HEAD_DIM
        kcg = kcv_ref[:, glo:glo + HEAD_DIM]
        vcg = kcv_ref[:, NSA_KV + glo:NSA_KV + glo + HEAD_DIM]
        p_c = _masked_softmax(_mm_nt(qu, kcg) * scale, cend <= pos_rows)
        o_c = _mm(p_c, vcg)
        p_sum = p_c[0:qb]
        for h in range(1, hg):
            p_sum = p_sum + p_c[h * qb:(h + 1) * qb]
        imp = _mm(p_sum, c2s_ref[...])
        score = jnp.where(jsel <= jq, jnp.where(forced, BIG, imp), -1.0)
        sel = jnp.zeros((qb, ns), F32)
        for _ in range(n_sel):
            m, _, pick = _take_first_max(score, jsel_f, float(ns), 1)
            sel = jnp.where(pick & (m >= 0.0), 1.0, sel)
            score = jnp.where(pick, -2.0, score)
        sel_b = sel.astype(BF16)

        def sel_step(c, carry, qr=qr, glo=glo, sel_b=sel_b):
            m_i, l_i, acc = carry
            k0 = pl.multiple_of(c * kc, kc)
            kk = katt_ref[pl.ds(k0, kc), glo:glo + HEAD_DIM]
            vv = katt_ref[pl.ds(k0, kc), NSA_KV + glo:NSA_KV + glo + HEAD_DIM]
            s = _mm_nt(qr, kk) * scale
            kpos = k0 + _iota((1, kc), 1)
            expand = (_iota((ns, 1), 0) == lax.shift_right_logical(kpos, SEL_BLOCK.bit_length() - 1)).astype(BF16)
            chosen = jnp.dot(sel_b, expand, preferred_element_type=F32)
            chosen = jnp.where(kpos <= pos_q, chosen, 0.0)
            mask = jnp.concatenate([chosen] * hg, axis=0) > 0.5
            m_new = jnp.maximum(m_i, jnp.max(jnp.where(mask, s, NEG), axis=1, keepdims=True))
            alpha = jnp.exp(m_i - m_new)
            p = jnp.where(mask, jnp.exp(s - m_new), 0.0)
            l_new = alpha * l_i + jnp.sum(p, axis=1, keepdims=True)
            return m_new, l_new, alpha * acc + _mm(p, vv)

        init = (jnp.full((hg * qb, 1), NEG, F32), jnp.zeros((hg * qb, 1), F32),
                jnp.zeros((hg * qb, HEAD_DIM), F32))
        _, l_s, acc_s = lax.fori_loop(0, n_kc, sel_step, init)
        o_s = acc_s / jnp.where(l_s > 0, l_s, 1.0)
        kw = katt_ref[pl.ds(wstart, wlen), 2 * NSA_KV + glo:2 * NSA_KV + glo + HEAD_DIM]
        vw = katt_ref[pl.ds(wstart, wlen), 3 * NSA_KV + glo:3 * NSA_KV + glo + HEAD_DIM]
        o_w = _mm(_masked_softmax(_mm_nt(qr, kw) * scale, mask_w), vw)
        for hi, h in enumerate(heads):
            rows = slice(hi * qb, (hi + 1) * qb)
            out_heads.append(gates[:, 3 * h:3 * h + 1] * o_c[rows] + gates[:, 3 * h + 1:3 * h + 2] * o_s[rows]
                             + gates[:, 3 * h + 2:3 * h + 3] * o_w[rows])
    o_ref[...] = jnp.concatenate(out_heads, axis=1).astype(o_ref.dtype)


def _nsa_prompt(p, kcv, b, t):
    qb, kc = NSA_QB, NSA_KC
    assert t % kc == 0 and kc % qb == 0 and t >= qb + WINDOW and t % SEL_BLOCK == 0
    nch = kcv.shape[1]
    ns = t // SEL_BLOCK
    c2s = _cmp_to_sel(nch, ns).astype(BF16)
    r3 = lambda a: a.reshape(b, t, a.shape[-1])
    tile = lambda w: pl.BlockSpec((None, qb, w), lambda i, j: (i, j, 0))
    return pl.pallas_call(
        functools.partial(_nsa_prompt_kernel, qb=qb, kc=kc),
        grid=(b, t // qb),
        in_specs=[tile(NSA_Q), tile(NSA_Q), tile(GATE_PAD),
                  pl.BlockSpec((None, nch, 2 * NSA_KV), lambda i, j: (i, 0, 0)),
                  pl.BlockSpec((nch, ns), lambda i, j: (0, 0)),
                  pl.BlockSpec((None, t, 4 * NSA_KV), lambda i, j: (i, 0, 0))],
        out_specs=tile(NSA_Q),
        out_shape=jax.ShapeDtypeStruct((b, t, NSA_Q), BF16),
        compiler_params=_cparams(("parallel", "arbitrary"), 56),
        name="nsa_prompt",
    )(r3(p["qu"]), r3(p["qr"]), r3(p["gate"]), kcv, c2s, r3(p["katt"])).reshape(b * t, NSA_Q)


def _moba_prompt_kernel(q_ref, km_ref, k_ref, v_ref, o_ref, *, blk):
    jq = pl.program_id(2)
    nb = km_ref.shape[0]
    n_top = min(MOBA_TOPK, nb)
    scale = HEAD_DIM ** -0.5
    pos_q = jq * blk + _iota((blk, 1), 0)
    jb = _iota((1, nb), 1)
    jb_f = jb.astype(F32)
    outs = []
    for hh in range(LANES // HEAD_DIM):
        lo = hh * HEAD_DIM
        q = q_ref[:, lo:lo + HEAD_DIM]
        score = jnp.where(jb < jq, _mm_nt(q, km_ref[:, lo:lo + HEAD_DIM]), NEG)
        sel = jnp.zeros((blk, nb), F32)
        for _ in range(n_top):
            m, _, pick = _take_first_max(score, jb_f, float(nb), 1)
            sel = jnp.where(pick & (m > 0.5 * NEG), 1.0, sel)
            score = jnp.where(pick, 3.0 * NEG, score)
        sel_b = sel.astype(BF16)

        def update(carry, s, mask, vv):
            m_i, l_i, acc = carry
            m_new = jnp.maximum(m_i, jnp.max(jnp.where(mask, s, NEG), axis=1, keepdims=True))
            alpha = jnp.exp(m_i - m_new)
            p = jnp.where(mask, jnp.exp(s - m_new), 0.0)
            return m_new, alpha * l_i + jnp.sum(p, axis=1, keepdims=True), alpha * acc + _mm(p, vv)

        def past_step(j, carry, q=q, lo=lo, sel_b=sel_b):
            k0 = pl.multiple_of(j * blk, blk)
            kk = k_ref[pl.ds(k0, blk), lo:lo + HEAD_DIM]
            vv = v_ref[pl.ds(k0, blk), lo:lo + HEAD_DIM]
            onehot = jnp.broadcast_to(_iota((nb, 1), 0) == j, (nb, blk)).astype(BF16)
            chosen = jnp.dot(sel_b, onehot, preferred_element_type=F32)
            return update(carry, _mm_nt(q, kk) * scale, chosen > 0.5, vv)

        init = (jnp.full((blk, 1), NEG, F32), jnp.zeros((blk, 1), F32), jnp.zeros((blk, HEAD_DIM), F32))
        carry = lax.fori_loop(0, jq, past_step, init)
        k0 = pl.multiple_of(jq * blk, blk)
        kk = k_ref[pl.ds(k0, blk), lo:lo + HEAD_DIM]
        vv = v_ref[pl.ds(k0, blk), lo:lo + HEAD_DIM]
        kpos = jq * blk + _iota((1, blk), 1)
        _, l_f, acc = update(carry, _mm_nt(q, kk) * scale, kpos <= pos_q, vv)
        outs.append(acc / jnp.where(l_f > 0, l_f, 1.0))
    o_ref[...] = jnp.concatenate(outs, axis=1).astype(o_ref.dtype)


def _moba_prompt(p, b, t):
    blk = MOBA_BLOCK
    assert t % blk == 0
    nb = t // blk
    npair = MOBA_W // LANES
    q = p["qb"].reshape(b, t, MOBA_W)
    kv = p["kvb"].reshape(b, t, MOBA_ROW)
    km = p["kmean"].reshape(b, nb, MOBA_W)
    return pl.pallas_call(
        functools.partial(_moba_prompt_kernel, blk=blk),
        grid=(b, npair, nb),
        in_specs=[pl.BlockSpec((None, blk, LANES), lambda i, hp, j: (i, j, hp)),
                  pl.BlockSpec((None, nb, LANES), lambda i, hp, j: (i, 0, hp)),
                  pl.BlockSpec((None, t, LANES), lambda i, hp, j: (i, 0, hp)),
                  pl.BlockSpec((None, t, LANES), lambda i, hp, j: (i, 0, npair + hp))],
        out_specs=pl.BlockSpec((None, blk, LANES), lambda i, hp, j: (i, j, hp)),
        out_shape=jax.ShapeDtypeStruct((b, t, MOBA_W), BF16),
        compiler_params=_cparams(("parallel", "parallel", "arbitrary")),
        name="moba_prompt",
    )(q, km, kv, kv).reshape(b * t, MOBA_W)


def _group_rows(n_rows=NSA_HEADS):
    return _iota((n_rows, 1), 0) < NSA_GROUP


def _nsa_dec_cmp_kernel(qu_ref, kcv_ref, c2s_ref, oc_ref, idx_ref, *, pos):
    assert NSA_KV_HEADS == 2
    q = qu_ref[...]
    nch = kcv_ref.shape[0]
    ns = c2s_ref.shape[1]
    n_sel = min(SEL_TOPN, -(-(pos + 1) // SEL_BLOCK))
    scale = HEAD_DIM ** -0.5
    g0 = _group_rows()
    hd = HEAD_DIM
    kcv = kcv_ref[...]
    s = jnp.where(g0, _mm_nt(q, kcv[:, 0:hd]), _mm_nt(q, kcv[:, hd:2 * hd])) * scale
    cend = _iota((1, nch), 1) * CMP_STRIDE + (CMP_BLOCK - 1)
    p = _masked_softmax(s, cend <= pos)
    oc_ref[...] = jnp.where(g0, _mm(p, kcv[:, NSA_KV:NSA_KV + hd]), _mm(p, kcv[:, NSA_KV + hd:NSA_KV + 2 * hd]))
    p0 = jnp.sum(jnp.where(g0, p, 0.0), axis=0, keepdims=True)
    p1 = jnp.sum(jnp.where(g0, 0.0, p), axis=0, keepdims=True)
    imp = _mm(jnp.where(g0, p0, p1), c2s_ref[...])
    jsel = _iota((1, ns), 1)
    jsel_f = jsel.astype(F32)
    jq = pos // SEL_BLOCK
    forced = (jsel == 0) | (jsel == jq) | (jsel == jq - 1)
    score = jnp.where(jsel <= jq, jnp.where(forced, BIG, imp), -1.0)
    lane = _iota((1, LANES), 1)
    idx = jnp.full((NSA_HEADS, LANES), -1, I32)
    for it in range(n_sel):
        m, first, pick = _take_first_max(score, jsel_f, float(ns), 1)
        idx = jnp.where(lane == it, jnp.where(m >= 0.0, first, -1.0).astype(I32), idx)
        score = jnp.where(pick, -2.0, score)
    idx_ref[...] = idx


def _attend_with_new(q, kts, vts, valids, k_new, v_new, new_valid, scale):
    s = [_mm(q, kt) * scale for kt in kts]
    qf = q.astype(MXU_DTYPE).astype(F32)
    s_new = jnp.sum(qf * k_new.astype(MXU_DTYPE).astype(F32), axis=1, keepdims=True) * scale
    m = jnp.where(new_valid, s_new, NEG)
    for sj, vj in zip(s, valids):
        m = jnp.maximum(m, jnp.max(jnp.where(vj > 0.5, sj, NEG), axis=1, keepdims=True))
    p = [jnp.where(vj > 0.5, jnp.exp(sj - m), 0.0) for sj, vj in zip(s, valids)]
    p_new = jnp.where(new_valid, jnp.exp(s_new - m), 0.0)
    d = p_new
    for pj in p:
        d = d + jnp.sum(pj, axis=1, keepdims=True)
    d = jnp.where(d > 0, d, 1.0)
    o = (p_new / d).astype(MXU_DTYPE).astype(F32) * v_new.astype(MXU_DTYPE).astype(F32)
    for pj, vt in zip(p, vts):
        o = o + _mm_nt(pj / d, vt)
    return o


def _nsa_dec_att_kernel(idx_ref, pt_ref, qr_ref, gate_ref, oc_ref, new_ref, win_ref, *rest, pos, n_sel, past_blocks):
    del pt_ref
    blk_refs, o_ref = rest[:-1], rest[-1]
    b = pl.program_id(0)
    q = qr_ref[...]
    scale = HEAD_DIM ** -0.5
    hd = HEAD_DIM
    g0 = _group_rows()
    new = new_ref[...]
    n_win = win_ref.shape[-1]
    wpos = pos - n_win + _iota((1, n_win), 1)
    valid_w = jnp.where((wpos > pos - WINDOW) & (wpos >= 0), 1.0, 0.0)
    page = blk_refs[0].shape[-1]
    bpp = page // SEL_BLOCK
    blk_of_lane = lax.shift_right_logical(_iota((1, page), 1), SEL_BLOCK.bit_length() - 1)
    o_s, o_w = [], []
    for g in range(NSA_KV_HEADS):
        glo = g * hd
        kts, vts, valids = [], [], []
        new_valid = jnp.zeros((1, 1), F32)
        for j in range(n_sel):
            bj = idx_ref[b, g, j]
            kts.append(blk_refs[2 * (g * n_sel + j)][...])
            vts.append(blk_refs[2 * (g * n_sel + j) + 1][...])
            in_past = jnp.where((bj >= 0) & (bj < past_blocks), 1.0, 0.0)
            valids.append(jnp.where(blk_of_lane == bj % bpp, in_past, 0.0))
            new_valid = jnp.maximum(new_valid, jnp.where(bj == past_blocks, 1.0, 0.0))
        o_s.append(_attend_with_new(q, kts, vts, valids, new[:, glo:glo + hd],
                                    new[:, NSA_KV + glo:NSA_KV + glo + hd], new_valid > 0.5, scale))
        o_w.append(_attend_with_new(q, [win_ref[0, g]], [win_ref[1, g]], [valid_w],
                                    new[:, 2 * NSA_KV + glo:2 * NSA_KV + glo + hd],
                                    new[:, 3 * NSA_KV + glo:3 * NSA_KV + glo + hd], True, scale))
    gs = _sigmoid(gate_ref[...])
    o_ref[...] = (gs[:, 0:1] * oc_ref[...] + gs[:, 1:2] * jnp.where(g0, o_s[0], o_s[1])
                  + gs[:, 2:3] * jnp.where(g0, o_w[0], o_w[1]))


def _nsa_decode_select(p, kcv, pos):
    db = kcv.shape[0]
    nch = kcv.shape[1]
    ns = pos // SEL_BLOCK + 1
    ns_pad = -(-ns // LANES) * LANES
    n_sel = min(SEL_TOPN, ns)
    c2s = _cmp_to_sel(nch, ns_pad).astype(BF16)
    per_q = lambda w: pl.BlockSpec((None, NSA_HEADS, w), lambda i: (i, 0, 0))
    o_c, idx = pl.pallas_call(
        functools.partial(_nsa_dec_cmp_kernel, pos=pos),
        grid=(db,),
        in_specs=[per_q(HEAD_DIM), pl.BlockSpec((None, nch, 2 * NSA_KV), lambda i: (i, 0, 0)),
                  pl.BlockSpec((nch, ns_pad), lambda i: (0, 0))],
        out_specs=[per_q(HEAD_DIM), per_q(LANES)],
        out_shape=[jax.ShapeDtypeStruct((db, NSA_HEADS, HEAD_DIM), F32),
                   jax.ShapeDtypeStruct((db, NSA_HEADS, LANES), I32)],
        compiler_params=_cparams(("parallel",)),
        name="nsa_dec_cmp",
    )(p["qu"].reshape(db, NSA_HEADS, HEAD_DIM), kcv, c2s)
    return o_c, idx[:, ::NSA_GROUP, :n_sel]


def _nsa_decode(p, kcv, cache_t, win_t, page_table, pos):
    db, n_pages = page_table.shape
    page = cache_t.shape[-1]
    assert pos == n_pages * page and pos % SEL_BLOCK == 0 and page % SEL_BLOCK == 0
    past_blocks = pos // SEL_BLOCK
    o_c, sel_idx = _nsa_decode_select(p, kcv, pos)
    n_sel = sel_idx.shape[2]
    heads3 = lambda a: a.reshape(db, NSA_HEADS, HEAD_DIM)
    bpp = page // SEL_BLOCK

    def blk_map(i, ix, pt, g, j, part):
        bj = jnp.clip(ix[i, g, j], 0, past_blocks - 1)
        return (pt[i, bj // bpp], part, g, 0, 0)

    k_sel, v_sel = 2, 3
    blk_specs = [pl.BlockSpec((None, None, None, HEAD_DIM, page), functools.partial(blk_map, g=g, j=j, part=part))
                 for g in range(NSA_KV_HEADS) for j in range(n_sel) for part in (k_sel, v_sel)]
    per_q2 = lambda w: pl.BlockSpec((None, NSA_HEADS, w), lambda i, ix, pt: (i, 0, 0))
    n_win = win_t.shape[-1]
    gate3 = p["gate"][:, :3 * NSA_HEADS].reshape(db, NSA_HEADS, 3)
    o = pl.pallas_call(
        functools.partial(_nsa_dec_att_kernel, pos=pos, n_sel=n_sel, past_blocks=past_blocks),
        grid_spec=pltpu.PrefetchScalarGridSpec(
            num_scalar_prefetch=2,
            grid=(db,),
            in_specs=[per_q2(HEAD_DIM), per_q2(3), per_q2(HEAD_DIM),
                      pl.BlockSpec((None, 1, 4 * NSA_KV), lambda i, ix, pt: (i, 0, 0)),
                      pl.BlockSpec((None, 2, NSA_KV_HEADS, HEAD_DIM, n_win), lambda i, ix, pt: (i, 0, 0, 0, 0))]
            + blk_specs,
            out_specs=per_q2(HEAD_DIM),
        ),
        out_shape=jax.ShapeDtypeStruct((db, NSA_HEADS, HEAD_DIM), F32),
        compiler_params=_cparams(("arbitrary",)),
        name="nsa_dec_att",
    )(sel_idx, page_table, heads3(p["qr"]), gate3, o_c, p["katt"].reshape(db, 1, 4 * NSA_KV), win_t,
      *([cache_t] * len(blk_specs)))
    return o.reshape(db, NSA_Q)


def _moba_dec_mean_kernel(*refs, n_src, ppb):
    x_refs, o_ref = refs[1:1 + n_src], refs[1 + n_src]
    j = pl.program_id(1)
    nb = o_ref.shape[1]
    page = x_refs[0].shape[-1]
    bps = n_src // ppb

    @pl.when(j == 0)
    def _():
        o_ref[...] = jnp.zeros(o_ref.shape, F32)

    lane = _iota((1, nb), 1)
    acc = o_ref[...]
    for i in range(bps):
        tot = x_refs[i * ppb][...]
        for r in x_refs[i * ppb + 1:(i + 1) * ppb]:
            tot = tot + r[...]
        col = jnp.sum(tot.reshape(MOBA_W, page), axis=1, keepdims=True) * (1.0 / (ppb * page))
        acc = jnp.where(lane == j * bps + i, col, acc)
    o_ref[...] = acc


def _moba_dec_gate_kernel(q_ref, km_ref, idx_ref, *, jq):
    q = q_ref[...]
    nb = km_ref.shape[1]
    n_top = min(MOBA_TOPK, nb)
    head_of_lane = lax.shift_right_logical(_iota((MOBA_HEADS, MOBA_W), 1), HEAD_DIM.bit_length() - 1)
    qbd = jnp.where(head_of_lane == _iota((MOBA_HEADS, MOBA_W), 0),
                    jnp.broadcast_to(q.astype(F32), (MOBA_HEADS, MOBA_W)), 0.0)
    jb = _iota((1, nb), 1)
    jb_f = jb.astype(F32)
    score = jnp.where(jb < jq, _mm(qbd, km_ref[...]), NEG)
    lane = _iota((1, LANES), 1)
    idx = jnp.full((MOBA_HEADS, LANES), -1, I32)
    for it in range(n_top):
        m, first, pick = _take_first_max(score, jb_f, float(nb), 1)
        idx = jnp.where(lane == it, jnp.where(m > 0.5 * NEG, first, -1.0).astype(I32), idx)
        score = jnp.where(pick, 3.0 * NEG, score)
    idx_ref[...] = idx


def _moba_dec_att_kernel(idx_ref, pt_ref, q_ref, kn_ref, vn_ref, *rest, n_top, ppb):
    del pt_ref
    src, o_ref = rest[:-1], rest[-1]
    b, hp = pl.program_id(0), pl.program_id(1)
    hpl = LANES // HEAD_DIM
    scale = HEAD_DIM ** -0.5
    rows = 8
    page = src[0].shape[-1]
    q_all = jnp.broadcast_to(q_ref[...].astype(F32), (rows, LANES))
    k_new = kn_ref[...]
    v_new = vn_ref[...]
    outs = []
    for hh in range(hpl):
        lanes = slice(hh * HEAD_DIM, (hh + 1) * HEAD_DIM)
        kts, vts, valids = [], [], []
        for t in range(n_top):
            ok = jnp.where(idx_ref[b, hp * hpl + hh, t] >= 0, 1.0, 0.0)
            for pg in range(ppb):
                base = 2 * ((hh * n_top + t) * ppb + pg)
                kts.append(src[base][...])
                vts.append(src[base + 1][...])
                valids.append(jnp.full((1, page), 1.0, F32) * ok)
        o = _attend_with_new(q_all[:, lanes], kts, vts, valids, k_new[:, lanes], v_new[:, lanes], True, scale)
        outs.append(o[0:1])
    o_ref[...] = jnp.concatenate(outs, axis=1)


def _moba_decode(p, cache_t, page_table, pos):
    db, n_pages = page_table.shape
    page = cache_t.shape[-1]
    assert MOBA_BLOCK % page == 0 and pos % MOBA_BLOCK == 0 and pos == n_pages * page
    ppb = MOBA_BLOCK // page
    nb = pos // MOBA_BLOCK
    assert nb >= MOBA_TOPK
    pps = PAGES_PER_STEP
    page_specs = [pl.BlockSpec((None, None, MOBA_HEADS, HEAD_DIM, page), functools.partial(
        lambda i, j, pt, pg: (pt[i, j * pps + pg], 0, 0, 0, 0), pg=pg)) for pg in range(pps)]
    kmean_t = pl.pallas_call(
        functools.partial(_moba_dec_mean_kernel, n_src=pps, ppb=ppb),
        grid_spec=pltpu.PrefetchScalarGridSpec(
            num_scalar_prefetch=1,
            grid=(db, n_pages // pps),
            in_specs=page_specs,
            out_specs=pl.BlockSpec((None, MOBA_W, nb), lambda i, j, pt: (i, 0, 0)),
        ),
        out_shape=jax.ShapeDtypeStruct((db, MOBA_W, nb), F32),
        compiler_params=_cparams(("parallel", "arbitrary")),
        name="moba_dec_mean",
    )(page_table, *([cache_t] * pps))
    idx = pl.pallas_call(
        functools.partial(_moba_dec_gate_kernel, jq=nb),
        grid=(db,),
        in_specs=[pl.BlockSpec((None, 1, MOBA_W), lambda i: (i, 0, 0)),
                  pl.BlockSpec((None, MOBA_W, nb), lambda i: (i, 0, 0))],
        out_specs=pl.BlockSpec((None, MOBA_HEADS, LANES), lambda i: (i, 0, 0)),
        out_shape=jax.ShapeDtypeStruct((db, MOBA_HEADS, LANES), I32),
        compiler_params=_cparams(("parallel",)),
        name="moba_dec_gate",
    )(p["qb"].reshape(db, 1, MOBA_W), kmean_t)
    n_top = min(MOBA_TOPK, nb)
    top_idx = idx[:, :, :n_top]
    hpl = LANES // HEAD_DIM
    npair = MOBA_HEADS // hpl

    def src_map(i, hp, ix, pt, hh, t, pg, kv):
        bj = jnp.clip(ix[i, hp * hpl + hh, t], 0, nb - 1)
        return (pt[i, bj * ppb + pg], kv, hp * hpl + hh, 0, 0)

    src_specs = [pl.BlockSpec((None, None, None, HEAD_DIM, page), functools.partial(src_map, hh=hh, t=t, pg=pg, kv=kv))
                 for hh in range(hpl) for t in range(n_top) for pg in range(ppb) for kv in range(2)]
    pair = lambda off: pl.BlockSpec((None, None, 1, LANES), lambda i, hp, ix, pt: (i, off + hp, 0, 0))
    o = pl.pallas_call(
        functools.partial(_moba_dec_att_kernel, n_top=n_top, ppb=ppb),
        grid_spec=pltpu.PrefetchScalarGridSpec(
            num_scalar_prefetch=2,
            grid=(db, npair),
            in_specs=[pair(0), pair(0), pair(npair)] + src_specs,
            out_specs=pair(0),
        ),
        out_shape=jax.ShapeDtypeStruct((db, npair, 1, LANES), F32),
        compiler_params=_cparams(("arbitrary", "arbitrary")),
        name="moba_dec_att",
    )(top_idx, page_table, p["qb"].reshape(db, npair, 1, LANES), p["kvb"].reshape(db, 2 * npair, 1, LANES),
      p["kvb"].reshape(db, 2 * npair, 1, LANES), *([cache_t] * len(src_specs)))
    return o.reshape(db, MOBA_W)


def _route_t(s_t, b_t):
    n_e, n_tok = s_t.shape
    per = n_e // N_GROUPS
    biased = s_t + b_t
    sub_f = _iota((per, 1), 0).astype(F32)
    gscore = []
    for g in range(N_GROUPS):
        x = biased[g * per:(g + 1) * per]
        m1, _, pick = _take_first_max(x, sub_f, float(per), 0)
        gscore.append(m1 + jnp.max(jnp.where(pick, NEG, x), axis=0, keepdims=True))
    gs = jnp.concatenate(gscore, axis=0)
    g_f = _iota((N_GROUPS, 1), 0).astype(F32)
    gmask = jnp.zeros((N_GROUPS, n_tok), F32)
    for _ in range(TOPK_GROUPS):
        _, _, pick = _take_first_max(gs, g_f, float(N_GROUPS), 0)
        gmask = jnp.where(pick, 1.0, gmask)
        gs = jnp.where(pick, NEG, gs)
    masked = jnp.concatenate([jnp.where(gmask[g:g + 1] > 0.5, biased[g * per:(g + 1) * per], NEG)
                              for g in range(N_GROUPS)], axis=0)
    e_f = _iota((n_e, 1), 0).astype(F32)
    ids, ws = [], []
    for _ in range(TOP_K):
        _, first, pick = _take_first_max(masked, e_f, float(n_e), 0)
        ids.append(first)
        ws.append(jnp.sum(jnp.where(pick, s_t, 0.0), axis=0, keepdims=True))
        masked = jnp.where(pick, 3.0 * NEG, masked)
    w = jnp.concatenate(ws, axis=0)
    w = w / jnp.sum(w, axis=0, keepdims=True) * ROUTED_SCALE
    return jnp.concatenate(ids, axis=0).astype(I32), w


def _merge_kernel(x_ref, oa_ref, ob_ref, sc1_ref, sh1_ref, g1_ref, sc2_ref, sh2_ref, wmg_ref, wa_ref, wb_ref,
                  wo_ref, lg_ref, lb_ref, wr_ref, br_ref, x1_ref, h_ref, ti_ref, tw_ref, *, alpha):
    x = x_ref[...]
    d = x.shape[1]
    u = x * (1.0 + sc1_ref[...]) + sh1_ref[...]
    mg = _mm(u, wmg_ref[...])
    y_a = _mm(oa_ref[...], wa_ref[...])
    y_b = _mm(ob_ref[...], wb_ref[...])
    mix = _mm(_sigmoid(mg[:, :d]) * y_a + _sigmoid(mg[:, d:]) * y_b, wo_ref[...])
    x1 = _layer_norm(alpha * x + g1_ref[...] * mix, lg_ref[...], lb_ref[...])
    x1_ref[...] = x1
    h = x1 * (1.0 + sc2_ref[...]) + sh2_ref[...]
    h_ref[...] = h
    s_t = _sigmoid(_mm_nt(wr_ref[...], h))
    ti_ref[...], tw_ref[...] = _route_t(s_t, br_ref[...])


def _merge(x, o_a, o_b, mods, w, *, tm, per_token_mod, rows_per_batch, alpha):
    n, d = x.shape
    nt = n // tm
    tiles_per_batch = rows_per_batch // tm
    row = lambda i: (i, 0)
    if per_token_mod:
        mod_spec = pl.BlockSpec((tm, d), row)
    else:
        mod_spec = pl.BlockSpec((None, 1, d), lambda i: (i // tiles_per_batch, 0, 0))
    full = lambda a: pl.BlockSpec(a.shape, lambda i: (0,) * a.ndim)
    ws = [w["w_mg"], w["w_nsa_out"], w["w_moba_out"], w["w_o"], w["ln1_g"], w["ln1_b"], w["w_router_t"], w["b_router"]]
    return pl.pallas_call(
        functools.partial(_merge_kernel, alpha=alpha),
        grid=(nt,),
        in_specs=[pl.BlockSpec((tm, d), row), pl.BlockSpec((tm, NSA_Q), row), pl.BlockSpec((tm, MOBA_W), row)]
        + [mod_spec] * 5 + [full(a) for a in ws],
        out_specs=[pl.BlockSpec((tm, d), row), pl.BlockSpec((tm, d), row),
                   pl.BlockSpec((TOP_K, tm), lambda i: (0, i)), pl.BlockSpec((TOP_K, tm), lambda i: (0, i))],
        out_shape=[jax.ShapeDtypeStruct((n, d), F32), jax.ShapeDtypeStruct((n, d), F32),
                   jax.ShapeDtypeStruct((TOP_K, n), I32), jax.ShapeDtypeStruct((TOP_K, n), F32)],
        compiler_params=_cparams(("parallel",), 48),
        name="merge",
    )(x, o_a, o_b, *mods, *ws)


def _expert_kernel(be_ref, nu_ref, x_ref, wg_ref, wu_ref, wd_ref, y_ref):
    del be_ref

    @pl.when(pl.program_id(0) < nu_ref[0])
    def _():
        x = x_ref[...]
        y_ref[...] = _mm(_silu(_mm(x, wg_ref[...])) * _mm(x, wu_ref[...]), wd_ref[...])


def _dispatch(top_i, blk):
    k, n = top_i.shape
    a = k * n
    flat_e = top_i.reshape(-1)
    order = jnp.argsort(flat_e)
    e_sorted = flat_e[order]
    tok = (order % n).astype(I32)
    bounds = jnp.searchsorted(e_sorted, jnp.arange(N_EXPERTS + 1, dtype=I32), side="left").astype(I32)
    counts = bounds[1:] - bounds[:-1]
    padded = (counts + blk - 1) // blk * blk
    end_pad = jnp.cumsum(padded)
    start_pad = end_pad - padded
    dest = start_pad[e_sorted] + jnp.arange(a, dtype=I32) - bounds[:-1][e_sorted]
    nblk = -(-a // blk) + N_EXPERTS
    row_tok = jnp.zeros((nblk * blk,), I32).at[dest].set(tok)
    blk_e = jnp.minimum(jnp.searchsorted(end_pad, jnp.arange(nblk, dtype=I32) * blk, side="right"),
                        N_EXPERTS - 1).astype(I32)
    slot = jnp.zeros((a,), I32).at[order].set(dest).reshape(k, n)
    n_used = (end_pad[-1] // blk).astype(I32).reshape(1)
    return row_tok, blk_e, slot, n_used, nblk


def _experts(h, top_i, top_w, w):
    n, d = h.shape
    blk = min(EXPERT_BLK, max(8, (TOP_K * n) // N_EXPERTS))
    row_tok, blk_e, slot, n_used, nblk = _dispatch(top_i, blk)
    x_sorted = jnp.take(h, row_tok, axis=0)
    de = w["w_exp_gate"].shape[2]
    y = pl.pallas_call(
        _expert_kernel,
        grid_spec=pltpu.PrefetchScalarGridSpec(
            num_scalar_prefetch=2,
            grid=(nblk,),
            in_specs=[pl.BlockSpec((blk, d), lambda i, be, nu: (i, 0)),
                      pl.BlockSpec((None, d, de), lambda i, be, nu: (be[i], 0, 0)),
                      pl.BlockSpec((None, d, de), lambda i, be, nu: (be[i], 0, 0)),
                      pl.BlockSpec((None, de, d), lambda i, be, nu: (be[i], 0, 0))],
            out_specs=pl.BlockSpec((blk, d), lambda i, be, nu: (i, 0)),
        ),
        out_shape=jax.ShapeDtypeStruct((nblk * blk, d), F32),
        compiler_params=_cparams(("arbitrary",), 48),
        name="experts",
    )(blk_e, n_used, x_sorted, w["w_exp_gate"], w["w_exp_up"], w["w_exp_down"])
    routed = jnp.zeros((n, d), F32)
    for k in range(TOP_K):
        routed = routed + jnp.take(y, slot[k], axis=0) * top_w[k][:, None]
    return routed


def _final_kernel(x1_ref, h_ref, r_ref, g2_ref, wg_ref, wu_ref, wd_ref, lg_ref, lb_ref, o_ref, *, alpha):
    h = h_ref[...]
    f = r_ref[...] + _mm(_silu(_mm(h, wg_ref[...])) * _mm(h, wu_ref[...]), wd_ref[...])
    o_ref[...] = _layer_norm(alpha * x1_ref[...] + g2_ref[...] * f, lg_ref[...], lb_ref[...])


def _final(x1, h, routed, g2, w, *, tm, per_token_mod, rows_per_batch, alpha):
    n, d = x1.shape
    tiles_per_batch = rows_per_batch // tm
    row = lambda i: (i, 0)
    tile = pl.BlockSpec((tm, d), row)
    mod_spec = tile if per_token_mod else pl.BlockSpec((None, 1, d), lambda i: (i // tiles_per_batch, 0, 0))
    full = lambda a: pl.BlockSpec(a.shape, lambda i: (0,) * a.ndim)
    ws = [w["w_sh_gate"], w["w_sh_up"], w["w_sh_down"], w["ln2_g"], w["ln2_b"]]
    return pl.pallas_call(
        functools.partial(_final_kernel, alpha=alpha),
        grid=(n // tm,),
        in_specs=[tile, tile, tile, mod_spec] + [full(a) for a in ws],
        out_specs=tile,
        out_shape=jax.ShapeDtypeStruct((n, d), F32),
        compiler_params=_cparams(("parallel",)),
        name="final",
    )(x1, h, routed, g2, *ws)


def _prep_weights(lp):
    (w_ada, b_ada, w_in, cmp_k_w1, cmp_k_w2, cmp_k_pe, cmp_v_w1, cmp_v_w2, cmp_v_pe, w_nsa_out, w_moba_out, w_o,
     ln1_g, ln1_b, w_router, b_router, w_exp_gate, w_exp_up, w_exp_down, w_sh_gate, w_sh_up, w_sh_down,
     ln2_g, ln2_b) = lp
    c = lambda a: a.astype(MXU_DTYPE)
    row = lambda a: a.reshape(1, -1)
    w_r, w_mg = _reorder_w_in(w_in)
    wk, w2k, pek = _cmp_weights(cmp_k_w1, cmp_k_w2, cmp_k_pe)
    wv, w2v, pev = _cmp_weights(cmp_v_w1, cmp_v_w2, cmp_v_pe)
    return dict(w_ada=w_ada, b_ada=b_ada, w_r=w_r, w_mg=w_mg, cmp=(wk, w2k, pek, wv, w2v, pev),
                w_nsa_out=c(w_nsa_out), w_moba_out=c(w_moba_out), w_o=c(w_o), ln1_g=row(ln1_g), ln1_b=row(ln1_b),
                w_router_t=c(w_router.T), b_router=b_router.reshape(-1, 1),
                w_exp_gate=c(w_exp_gate), w_exp_up=c(w_exp_up), w_exp_down=c(w_exp_down),
                w_sh_gate=c(w_sh_gate), w_sh_up=c(w_sh_up), w_sh_down=c(w_sh_down), ln2_g=row(ln2_g), ln2_b=row(ln2_b))


def _token_tail(x, o_a, o_b, mods, w, *, tm, per_token_mod, rows_per_batch, alpha):
    sc1, sh1, g1, sh2, sc2, g2 = mods
    kw = dict(tm=tm, per_token_mod=per_token_mod, rows_per_batch=rows_per_batch, alpha=alpha)
    x1, h, top_i, top_w = _merge(x, o_a, o_b, (sc1, sh1, g1, sc2, sh2), w, **kw)
    routed = _experts(h, top_i, top_w, w)
    return _final(x1, h, routed, g2, w, **kw)


def _layer(xp, xs, c_all, cache_nsa_l, cache_moba_l, win_state_l, page_table, w, alpha):
    b, t, d = xp.shape
    db, ts, _ = xs.shape
    assert ts == 1
    page = cache_nsa_l.shape[1]
    pos = page_table.shape[1] * page
    assert win_state_l.shape[1] == WINDOW and t >= WINDOW
    mod = _ada(c_all, w["w_ada"], w["b_ada"])
    pieces = [mod[:, i * d:(i + 1) * d] for i in range(6)]
    mods_p = [m[:b].reshape(b, 1, d) for m in pieces]
    mods_s = [m[b:b + db] for m in pieces]
    order = lambda m: (m[1], m[0], m[2], m[3], m[4], m[5])
    mods_p, mods_s = order(mods_p), order(mods_s)

    tm = TOKEN_TILE
    xp2 = xp.reshape(b * t, d)
    pp = _inproj(xp2, mods_p[0], mods_p[1], w["w_r"], _rope_tables(jnp.arange(t)), tm=tm, per_token_mod=False,
                 rows_per_batch=t, with_kmean=True)
    kcv_p = _cmp_prompt(pp["nsa"].reshape(b, t, NSA_ROW), w["cmp"])
    oa_p = _nsa_prompt(pp, kcv_p, b, t)
    ob_p = _moba_prompt(pp, b, t)
    yp = _token_tail(xp2, oa_p, ob_p, mods_p, w, tm=tm, per_token_mod=False, rows_per_batch=t, alpha=alpha)

    xs2 = xs.reshape(db, d)
    ps = _inproj(xs2, mods_s[0], mods_s[1], w["w_r"], _rope_tables(jnp.full((db,), pos)), tm=db, per_token_mod=True,
                 rows_per_batch=db, with_kmean=False)
    nsa_pages = cache_nsa_l.reshape(cache_nsa_l.shape[0], page, NSA_ROW)
    kcv_s = _cmp_decode(nsa_pages, page_table, w["cmp"])
    token_minor = lambda a: jnp.transpose(a, (0, 2, 3, 4, 1))
    oa_s = _nsa_decode(ps, kcv_s, token_minor(cache_nsa_l), token_minor(win_state_l), page_table, pos)
    ob_s = _moba_decode(ps, token_minor(cache_moba_l), page_table, pos)
    ys = _token_tail(xs2, oa_s, ob_s, mods_s, w, tm=db, per_token_mod=True, rows_per_batch=db, alpha=alpha)

    g, hd = NSA_KV_HEADS, HEAD_DIM
    win_p = pp["win"].reshape(b, t, 2, g, hd)[:, t - WINDOW:]
    win_s = jnp.concatenate([win_state_l[:, 1:], ps["win"].reshape(db, 1, 2, g, hd)], axis=1)
    return (yp.reshape(b, t, d), ys.reshape(db, 1, d),
            pp["nsa"].reshape(b, t, 4, g, hd), ps["nsa"].reshape(db, 1, 4, g, hd),
            pp["moba"].reshape(b, t, 2, MOBA_HEADS, hd), ps["moba"].reshape(db, 1, 2, MOBA_HEADS, hd), win_p, win_s)


def kernel(x_prompt, x_sample, cache_nsa, cache_moba, state_nsa_win, page_table, c_prompt, c_sample, w_ada, b_ada,
           w_in, cmp_k_w1, cmp_k_w2, cmp_k_pe, cmp_v_w1, cmp_v_w2, cmp_v_pe, w_nsa_out, w_moba_out, w_o, ln1_g,
           ln1_b, w_router, b_router, w_exp_gate, w_exp_up, w_exp_down, w_sh_gate, w_sh_up, w_sh_down, ln2_g,
           ln2_b):
    params = (w_ada, b_ada, w_in, cmp_k_w1, cmp_k_w2, cmp_k_pe, cmp_v_w1, cmp_v_w2, cmp_v_pe, w_nsa_out, w_moba_out,
              w_o, ln1_g, ln1_b, w_router, b_router, w_exp_gate, w_exp_up, w_exp_down, w_sh_gate, w_sh_up,
              w_sh_down, ln2_g, ln2_b)
    depth = w_ada.shape[0]
    alpha = (2 * depth) ** 0.25
    b, db = x_prompt.shape[0], x_sample.shape[0]
    rows = -(-(b + db) // 8) * 8
    c_all = jnp.pad(jnp.concatenate([c_prompt, c_sample], axis=0), ((0, rows - b - db), (0, 0)))
    xp, xs = x_prompt, x_sample
    outs = [[] for _ in range(6)]
    for l in range(depth):
        w = _prep_weights([p[l] for p in params])
        res = _layer(xp, xs, c_all, cache_nsa[l], cache_moba[l], state_nsa_win[l], page_table, w, alpha)
        xp, xs = res[0], res[1]
        for acc, r in zip(outs, res[2:]):
            acc.append(r)
    return (xp, xs) + tuple(jnp.stack(o) for o in outs)
```

```python
import functools

import jax
import jax.numpy as jnp
from jax import lax
from jax.experimental import pallas as pl
from jax.experimental.pallas import tpu as pltpu

F32 = jnp.float32
BF16 = jnp.bfloat16
I32 = jnp.int32
MXU_DTYPE = jnp.bfloat16

HEAD_DIM = 64
ROPE_DIMS = HEAD_DIM // 4
ROPE_HALF = ROPE_DIMS // 2
ROPE_THETA = 500000.0
NSA_HEADS = 8
NSA_KV_HEADS = 2
NSA_GROUP = NSA_HEADS // NSA_KV_HEADS
CMP_BLOCK = 32
CMP_STRIDE = 16
CMP_RATIO = CMP_BLOCK // CMP_STRIDE
SEL_BLOCK = 64
SEL_TOPN = 16
WINDOW = 512
MOBA_HEADS = 8
MOBA_BLOCK = 256
MOBA_TOPK = 3
N_EXPERTS = 64
TOP_K = 8
N_GROUPS = 8
TOPK_GROUPS = 4
ROUTED_SCALE = 2.5
LN_EPS = 1e-5

LANES = 128
NSA_Q = NSA_HEADS * HEAD_DIM
NSA_KV = NSA_KV_HEADS * HEAD_DIM
MOBA_W = MOBA_HEADS * HEAD_DIM
NSA_ROW = 4 * NSA_KV
MOBA_ROW = 2 * MOBA_W
GATE_PAD = LANES

_QK_SCALE = HEAD_DIM ** -0.5
assert _QK_SCALE == 0.125
NEG = -1e30
BIG = 3e38
TOKEN_TILE = 256
NSA_QB = 128
EXPERT_BLK = 256
MOBA_HEADS_PER_STEP = 8
PAGES_PER_STEP = 16


def _sigmoid(x):
    return 1.0 / (1.0 + jnp.exp(-x))


def _silu(x):
    return x * _sigmoid(x)


def _mm(a, b):
    return jnp.dot(a.astype(MXU_DTYPE), b.astype(MXU_DTYPE), preferred_element_type=F32)


def _mm_nt(a, b):
    return lax.dot_general(a.astype(MXU_DTYPE), b.astype(MXU_DTYPE), (((1,), (1,)), ((), ())),
                           preferred_element_type=F32)


def _iota(shape, axis):
    return lax.broadcasted_iota(I32, shape, axis)


def _cparams(sem, vmem_mb=None):
    kw = dict(dimension_semantics=sem)
    if vmem_mb is not None:
        kw["vmem_limit_bytes"] = vmem_mb << 20
    return pltpu.CompilerParams(**kw)


def _masked_softmax(s, mask, axis=-1):
    m = jnp.max(jnp.where(mask, s, NEG), axis=axis, keepdims=True)
    e = jnp.where(mask, jnp.exp(s - m), 0.0)
    d = jnp.sum(e, axis=axis, keepdims=True)
    return e / jnp.where(d > 0, d, 1.0)


def _layer_norm(z, g, b):
    mu = jnp.mean(z, axis=-1, keepdims=True)
    zc = z - mu
    var = jnp.mean(zc * zc, axis=-1, keepdims=True)
    return zc * lax.rsqrt(var + LN_EPS) * g + b


def _take_first_max(score, idx_f, n_f, axis):
    m = jnp.max(score, axis=axis, keepdims=True)
    first = jnp.min(jnp.where(score == m, idx_f, n_f), axis=axis, keepdims=True)
    return m, first, idx_f == first


def _ada_kernel(c_ref, w_ref, b_ref, o_ref):
    a = _silu(c_ref[...])
    o_ref[...] = _mm(a, w_ref[...]) + b_ref[...]


def _ada(c_all, w_ada, b_ada):
    r, d = c_all.shape
    e6 = w_ada.shape[1]
    tn = 1024
    return pl.pallas_call(
        _ada_kernel,
        grid=(e6 // tn,),
        in_specs=[pl.BlockSpec((r, d), lambda j: (0, 0)),
                  pl.BlockSpec((d, tn), lambda j: (0, j)),
                  pl.BlockSpec((1, tn), lambda j: (0, j))],
        out_specs=pl.BlockSpec((r, tn), lambda j: (0, j)),
        out_shape=jax.ShapeDtypeStruct((r, e6), F32),
        compiler_params=_cparams(("arbitrary",)),
        name="ada",
    )(c_all, w_ada, b_ada.reshape(1, e6))


_C_QA = 0
_C_KVA = _C_QA + NSA_Q
_C_QB = _C_KVA + 6 * NSA_KV
_C_KB = _C_QB + MOBA_W
_C_VB = _C_KB + MOBA_W
_C_GATE = _C_VB + MOBA_W
_C_END = _C_GATE + GATE_PAD


def _rope(x, cs, s1, s2):
    parts = []
    for j in range(x.shape[1] // LANES):
        xj = x[:, j * LANES:(j + 1) * LANES]
        parts.append(xj * cs + pltpu.roll(xj, ROPE_HALF, 1) * s1 + pltpu.roll(xj, LANES - ROPE_HALF, 1) * s2)
    return parts[0] if len(parts) == 1 else jnp.concatenate(parts, axis=1)


def _inproj_kernel(x_ref, sc_ref, sh_ref, w_ref, cs_ref, s1_ref, s2_ref,
                   qu_ref, qr_ref, qb_ref, nsa_ref, katt_ref, win_ref, moba_ref, kvb_ref, g_ref, *km_ref):
    u = x_ref[...] * (1.0 + sc_ref[...]) + sh_ref[...]
    r = _mm(u, w_ref[...])
    cs, s1, s2 = cs_ref[...], s1_ref[...], s2_ref[...]
    rope = lambda v: _rope(v, cs, s1, s2)
    qa = r[:, _C_QA:_C_KVA]
    qa_rot = rope(qa)
    qu_ref[...] = qa.astype(BF16)
    qr_ref[...] = qa_rot.astype(BF16)
    o = _C_KVA
    kc_vc = r[:, o:o + 2 * NSA_KV]
    ks = rope(r[:, o + 2 * NSA_KV:o + 3 * NSA_KV])
    vs = r[:, o + 3 * NSA_KV:o + 4 * NSA_KV]
    kw = rope(r[:, o + 4 * NSA_KV:o + 5 * NSA_KV])
    vw = r[:, o + 5 * NSA_KV:o + 6 * NSA_KV]
    nsa_ref[...] = jnp.concatenate([kc_vc, ks, vs], axis=1)
    win_ref[...] = jnp.concatenate([kw, vw], axis=1)
    katt_ref[...] = jnp.concatenate([ks, vs, kw, vw], axis=1).astype(BF16)
    qb = rope(r[:, _C_QB:_C_KB])
    qb_ref[...] = qb.astype(BF16)
    kb = rope(r[:, _C_KB:_C_VB])
    vb = r[:, _C_VB:_C_GATE]
    mb = jnp.concatenate([kb, vb], axis=1)
    moba_ref[...] = mb
    kvb_ref[...] = mb.astype(BF16)
    gate = r[:, _C_GATE:_C_END]
    g_ref[...] = gate
    if km_ref:
        kmean_ref, qbt_ref, vbt_ref, qut_ref, qrt_ref, gt_ref, vst_ref, vwt_ref = km_ref
        kmean_ref[...] = jnp.sum(kb, axis=0, keepdims=True) * (1.0 / kb.shape[0])
        tr = lambda v: jnp.transpose(v).astype(BF16)
        qbt_ref[...] = tr(qb)
        vbt_ref[...] = tr(vb)
        qut_ref[...] = tr(qa)
        qrt_ref[...] = tr(qa_rot)
        gt_ref[...] = jnp.transpose(gate)
        vst_ref[...] = tr(vs)
        vwt_ref[...] = tr(vw)


def _inproj(x, sc, sh, w_r, rope_tabs, *, tm, per_token_mod, rows_per_batch, with_kmean):
    n, d = x.shape
    nt = n // tm
    tiles_per_batch = rows_per_batch // tm
    row = lambda i: (i, 0)
    if per_token_mod:
        mod_spec = pl.BlockSpec((tm, d), row)
    else:
        mod_spec = pl.BlockSpec((None, 1, d), lambda i: (i // tiles_per_batch, 0, 0))
    tab_spec = pl.BlockSpec((tm, LANES), lambda i: (i % tiles_per_batch, 0))
    outs = [("qu", NSA_Q, BF16), ("qr", NSA_Q, BF16), ("qb", MOBA_W, BF16), ("nsa", NSA_ROW, F32),
            ("katt", 4 * NSA_KV, BF16), ("win", 2 * NSA_KV, F32), ("moba", MOBA_ROW, F32),
            ("kvb", MOBA_ROW, BF16), ("gate", GATE_PAD, F32)]
    out_shape = [jax.ShapeDtypeStruct((n, w), dt) for _, w, dt in outs]
    out_specs = [pl.BlockSpec((tm, w), row) for _, w, _ in outs]
    extra = []
    if with_kmean:
        assert tm == MOBA_BLOCK
        extra = ["kmean"]
        out_shape.append(jax.ShapeDtypeStruct((nt, 1, MOBA_W), F32))
        out_specs.append(pl.BlockSpec((None, 1, MOBA_W), lambda i: (i, 0, 0)))
        for name, w, dt in [("qbt", MOBA_W, BF16), ("vbt", MOBA_W, BF16), ("qut", NSA_Q, BF16), ("qrt", NSA_Q, BF16),
                            ("gt", GATE_PAD, F32), ("vst", NSA_KV, BF16), ("vwt", NSA_KV, BF16)]:
            extra.append(name)
            out_shape.append(jax.ShapeDtypeStruct((nt, w, tm), dt))
            out_specs.append(pl.BlockSpec((None, w, tm), lambda i: (i, 0, 0)))
    res = pl.pallas_call(
        _inproj_kernel,
        grid=(nt,),
        in_specs=[pl.BlockSpec((tm, d), row), mod_spec, mod_spec,
                  pl.BlockSpec(w_r.shape, lambda i: (0, 0)), tab_spec, tab_spec, tab_spec],
        out_specs=out_specs,
        out_shape=out_shape,
        compiler_params=_cparams(("parallel",), 48),
        name="inproj",
    )(x, sc, sh, w_r, *rope_tabs)
    return dict(zip([o[0] for o in outs] + extra, res))


def _rope_tables(pos):
    inv = ROPE_THETA ** (-jnp.arange(ROPE_HALF, dtype=F32) / ROPE_HALF)
    ang = pos.astype(F32)[:, None] * inv
    cos, sin = jnp.cos(ang), jnp.sin(ang)
    rows = pos.shape[0]
    one = jnp.ones((rows, HEAD_DIM - ROPE_DIMS), F32)
    zero = jnp.zeros((rows, HEAD_DIM - ROPE_DIMS), F32)
    zh = jnp.zeros((rows, ROPE_HALF), F32)
    cs = jnp.concatenate([cos, cos, one], axis=1)
    s1 = jnp.concatenate([zh, sin, zero], axis=1)
    s2 = jnp.concatenate([-sin, zh, zero], axis=1)
    rep = LANES // HEAD_DIM
    return tuple(jnp.tile(t, (1, rep)) for t in (cs, s1, s2))


def _reorder_w_in(w_in):
    d = w_in.shape[0]
    sizes = (NSA_Q, 6 * NSA_KV, 3 * NSA_HEADS, MOBA_W, MOBA_W, MOBA_W, d, d)
    offs = [0]
    for s in sizes:
        offs.append(offs[-1] + s)
    q_a, kv_a, gate, q_b, k_b, v_b, mg_a, mg_b = [w_in[:, offs[i]:offs[i + 1]] for i in range(8)]
    gate = jnp.pad(gate, ((0, 0), (0, GATE_PAD - 3 * NSA_HEADS)))
    w_r = jnp.concatenate([q_a, kv_a, q_b, k_b, v_b, gate], axis=1).astype(MXU_DTYPE)
    w_mg = jnp.concatenate([mg_a, mg_b], axis=1).astype(MXU_DTYPE)
    return w_r, w_mg


_CHUNK_W = CMP_STRIDE * NSA_ROW


def _cmp1_kernel(*refs, n_src, n_prefetch=0):
    refs = refs[n_prefetch:]
    x_refs = refs[:n_src]
    wk_ref, wv_ref, pek_ref, pev_ref, o_ref = refs[n_src:]

    def gather(off):
        cols = []
        for s in range(CMP_STRIDE):
            lo = s * NSA_ROW + off
            pieces = [xr[:, lo:lo + NSA_KV] for xr in x_refs]
            cols.append(pieces[0] if n_src == 1 else jnp.concatenate(pieces, axis=0))
        return jnp.concatenate(cols, axis=1)

    xk = gather(0)
    xv = gather(NSA_KV)
    outs = []
    for r in range(CMP_RATIO):
        outs.append(_mm(xk + pek_ref[r], wk_ref[r]))
        outs.append(_mm(xv + pev_ref[r], wv_ref[r]))
    o_ref[...] = jnp.concatenate(outs, axis=1)


def _cmp2_kernel(a_ref, w2k_ref, w2v_ref, o_ref, vt_ref):
    a = a_ref[...]
    n = a.shape[0]
    w = NSA_KV
    hk = a[:, 0:w] + pltpu.roll(a[:, 2 * w:3 * w], n - 1, 0)
    hv = a[:, w:2 * w] + pltpu.roll(a[:, 3 * w:4 * w], n - 1, 0)
    vc = _mm(_silu(hv), w2v_ref[...])
    o_ref[...] = jnp.concatenate([_mm(_silu(hk), w2k_ref[...]), vc], axis=1)
    vt_ref[...] = jnp.transpose(vc)


def _blockdiag(w, reps):
    k, n = w.shape
    out = jnp.zeros((reps * k, reps * n), w.dtype)
    for g in range(reps):
        out = out.at[g * k:(g + 1) * k, g * n:(g + 1) * n].set(w)
    return out


def _cmp_weights(w1, w2, pe):
    g = NSA_KV_HEADS
    w1s = jnp.stack([jnp.concatenate([_blockdiag(w1[r, s], g) for s in range(CMP_STRIDE)], axis=0)
                     for r in range(CMP_RATIO)]).astype(MXU_DTYPE)
    pes = jnp.tile(pe[:, :, None, :], (1, 1, g, 1)).reshape(CMP_RATIO, 1, CMP_STRIDE * NSA_KV)
    return w1s, _blockdiag(w2, g).astype(MXU_DTYPE), pes


def _cmp_stage2(a, w2k, w2v):
    b, nch, _ = a.shape
    return pl.pallas_call(
        _cmp2_kernel,
        grid=(b,),
        in_specs=[pl.BlockSpec((None, nch, 4 * NSA_KV), lambda i: (i, 0, 0)),
                  pl.BlockSpec(w2k.shape, lambda i: (0, 0)), pl.BlockSpec(w2v.shape, lambda i: (0, 0))],
        out_specs=[pl.BlockSpec((None, nch, 2 * NSA_KV), lambda i: (i, 0, 0)),
                   pl.BlockSpec((None, NSA_KV, nch), lambda i: (i, 0, 0))],
        out_shape=[jax.ShapeDtypeStruct((b, nch, 2 * NSA_KV), F32), jax.ShapeDtypeStruct((b, NSA_KV, nch), F32)],
        compiler_params=_cparams(("parallel",)),
        name="cmp2",
    )(a, w2k, w2v)


def _cmp_prompt(nsa_rows, cw):
    b, t, _ = nsa_rows.shape
    nch = t // CMP_STRIDE
    tc = min(128, nch)
    x = nsa_rows.reshape(b, nch, _CHUNK_W)
    wk, w2k, pek, wv, w2v, pev = cw
    full = lambda a: pl.BlockSpec(a.shape, lambda i, j: (0,) * a.ndim)
    a = pl.pallas_call(
        functools.partial(_cmp1_kernel, n_src=1),
        grid=(b, nch // tc),
        in_specs=[pl.BlockSpec((None, tc, _CHUNK_W), lambda i, j: (i, j, 0)),
                  full(wk), full(wv), full(pek), full(pev)],
        out_specs=pl.BlockSpec((None, tc, 4 * NSA_KV), lambda i, j: (i, j, 0)),
        out_shape=jax.ShapeDtypeStruct((b, nch, 4 * NSA_KV), F32),
        compiler_params=_cparams(("parallel", "parallel"), 48),
        name="cmp1_prompt",
    )(x, wk, wv, pek, pev)
    return _cmp_stage2(a, w2k, w2v)


def _cmp_decode(cache_pages, page_table, cw):
    n_pool, page, _ = cache_pages.shape
    db, n_pages = page_table.shape
    cpp = page // CMP_STRIDE
    pps = PAGES_PER_STEP
    x = cache_pages.reshape(n_pool, cpp, _CHUNK_W)
    wk, w2k, pek, wv, w2v, pev = cw
    nch = n_pages * cpp
    full = lambda a: pl.BlockSpec(a.shape, lambda i, j, pt: (0,) * a.ndim)
    page_specs = [pl.BlockSpec((None, cpp, _CHUNK_W), functools.partial(
        lambda i, j, pt, p: (pt[i, j * pps + p], 0, 0), p=p)) for p in range(pps)]
    a = pl.pallas_call(
        functools.partial(_cmp1_kernel, n_src=pps, n_prefetch=1),
        grid_spec=pltpu.PrefetchScalarGridSpec(
            num_scalar_prefetch=1,
            grid=(db, n_pages // pps),
            in_specs=page_specs + [full(wk), full(wv), full(pek), full(pev)],
            out_specs=pl.BlockSpec((None, pps * cpp, 4 * NSA_KV), lambda i, j, pt: (i, j, 0)),
        ),
        out_shape=jax.ShapeDtypeStruct((db, nch, 4 * NSA_KV), F32),
        compiler_params=_cparams(("parallel", "parallel"), 48),
        name="cmp1_decode",
    )(page_table, *([x] * pps), wk, wv, pek, pev)
    return _cmp_stage2(a, w2k, w2v)


def _cmp_to_sel(nch, ns_pad):
    cs = jnp.arange(nch) * CMP_STRIDE
    ss = jnp.arange(ns_pad) * SEL_BLOCK
    return ((cs[:, None] < ss[None] + SEL_BLOCK) & (cs[:, None] + CMP_BLOCK > ss[None])).astype(F32)


def _nsa_prompt_kernel(qut_ref, qrt_ref, gt_ref, kc_ref, vct_ref, c2st_ref, ks_ref, kw_ref, vst_ref, vwt_ref, o_ref, *,
                       qb, kc):
    s0 = pl.program_id(1) * qb
    nkc = vst_ref.shape[0]
    nch = kc_ref.shape[0]
    ns = c2st_ref.shape[0]
    n_sel = min(SEL_TOPN, ns)
    hg, hd, groups = NSA_GROUP, HEAD_DIM, range(NSA_KV_HEADS)
    sel_shift = SEL_BLOCK.bit_length() - 1
    pos_q = s0 + _iota((1, qb), 1)
    pos_l = jnp.concatenate([pos_q] * hg, axis=1)
    gates = _sigmoid(gt_ref[...])
    zeros = jnp.zeros((hd, hg * qb), F32)

    def group_q(ref, g):
        x = jnp.concatenate([ref[(g * hg + h) * hd:(g * hg + h + 1) * hd, :] for h in range(hg)], axis=1)
        parts = [zeros] * NSA_KV_HEADS
        parts[g] = x.astype(F32) * _QK_SCALE
        return jnp.concatenate(parts, axis=0).astype(BF16)

    qzu = [group_q(qut_ref, g) for g in groups]
    qzr = [group_q(qrt_ref, g) for g in groups]
    cend = _iota((nch, 1), 0) * CMP_STRIDE + (CMP_BLOCK - 1)
    kcb = kc_ref[...]
    p_c = [_masked_softmax(_mm(kcb, qzu[g]), cend <= pos_l, axis=0) for g in groups]
    o_c = [_mm(vct_ref[g * hd:(g + 1) * hd, :], p_c[g]) for g in groups]
    jsel = _iota((ns, 1), 0)
    jsel_f = jsel.astype(F32)
    jq = lax.shift_right_logical(pos_q, sel_shift)
    forced = (jsel == 0) | (jsel == jq) | (jsel == jq - 1)
    sel_b = []
    for g in groups:
        p_sum = p_c[g][:, 0:qb]
        for h in range(1, hg):
            p_sum = p_sum + p_c[g][:, h * qb:(h + 1) * qb]
        score = jnp.where(jsel <= jq, jnp.where(forced, BIG, _mm(c2st_ref[...], p_sum)), -1.0)
        sel = jnp.zeros((ns, qb), F32)
        for _ in range(n_sel):
            m, _, pick = _take_first_max(score, jsel_f, float(ns), 0)
            sel = jnp.where(pick & (m >= 0.0), 1.0, sel)
            score = jnp.where(pick, -2.0, score)
        sel_b.append(sel.astype(BF16))
    blk_lane = _iota((1, ns), 1)

    def sel_step(c, carry):
        k0 = pl.multiple_of(c * kc, kc)
        kpos = k0 + _iota((kc, 1), 0)
        expand = (lax.shift_right_logical(kpos, sel_shift) == blk_lane).astype(BF16)
        causal = kpos <= pos_q
        kk = ks_ref[pl.ds(k0, kc), :]
        s = [_mm(kk, qzr[g]) for g in groups]
        bias = [jnp.where(causal & (jnp.dot(expand, sel_b[g], preferred_element_type=F32) > 0.5), 0.0, NEG)
                for g in groups]
        s = [s[g] + jnp.concatenate([bias[g]] * hg, axis=1) for g in groups]
        m_new = [jnp.maximum(carry[3 * g], jnp.max(s[g], axis=0, keepdims=True)) for g in groups]
        p = [jnp.exp(s[g] - jnp.maximum(m_new[g], 0.5 * NEG)) for g in groups]
        pv = [_mm(vst_ref[c, g * hd:(g + 1) * hd, :], p[g]) for g in groups]
        out = []
        for g in groups:
            m_i, l_i, acc = carry[3 * g:3 * g + 3]
            alpha = jnp.exp(m_i - m_new[g])
            out += [m_new[g], alpha * l_i + jnp.sum(p[g], axis=0, keepdims=True), alpha * acc + pv[g]]
        return tuple(out)

    init = (jnp.full((1, hg * qb), NEG, F32), jnp.zeros((1, hg * qb), F32), zeros) * NSA_KV_HEADS
    fin = lax.fori_loop(0, s0 // kc + 1, sel_step, init)
    o_s = [fin[3 * g + 2] / jnp.where(fin[3 * g + 1] > 0, fin[3 * g + 1], 1.0) for g in groups]
    nwc = WINDOW // kc + 1
    c_lo = jnp.minimum(jnp.maximum(s0 - WINDOW, 0) // kc, nkc - nwc)
    wstart = pl.multiple_of(c_lo * kc, kc)
    wpos = wstart + _iota((nwc * kc, 1), 0)
    mask_w = (wpos <= pos_l) & (wpos > pos_l - WINDOW)
    kwb = kw_ref[pl.ds(wstart, nwc * kc), :]
    p_w = [_masked_softmax(_mm(kwb, qzr[g]), mask_w, axis=0) for g in groups]
    o_w = []
    for g in groups:
        acc = _mm(vwt_ref[c_lo, g * hd:(g + 1) * hd, :], p_w[g][0:kc])
        for i in range(1, nwc):
            acc = acc + _mm(vwt_ref[c_lo + i, g * hd:(g + 1) * hd, :], p_w[g][i * kc:(i + 1) * kc])
        o_w.append(acc)
    rows = []
    for g in groups:
        for h in range(hg):
            hh, lanes = g * hg + h, slice(h * qb, (h + 1) * qb)
            rows.append(gates[3 * hh:3 * hh + 1] * o_c[g][:, lanes] + gates[3 * hh + 1:3 * hh + 2] * o_s[g][:, lanes]
                        + gates[3 * hh + 2:3 * hh + 3] * o_w[g][:, lanes])
    o_ref[...] = jnp.transpose(jnp.concatenate(rows, axis=0)).astype(o_ref.dtype)


def _nsa_prompt(p, kcv, vct, b, t):
    qb, kc = NSA_QB, TOKEN_TILE
    nwc = WINDOW // kc + 1
    assert t % kc == 0 and kc % qb == 0 and t >= nwc * kc and t % SEL_BLOCK == 0 and WINDOW % kc == 0
    nkc = t // kc
    nch = kcv.shape[1]
    ns = t // SEL_BLOCK
    c2st = _cmp_to_sel(nch, ns).T.astype(BF16)
    katt = p["katt"].reshape(b, t, 4 * NSA_KV)
    qsub = kc // qb
    q_spec = lambda w: pl.BlockSpec((None, w, qb), lambda i, j: (i * nkc + j // qsub, 0, j % qsub))
    chunks = lambda a: a.reshape(b, nkc, NSA_KV, kc)
    chunk_spec = pl.BlockSpec((None, nkc, NSA_KV, kc), lambda i, j: (i, 0, 0, 0))
    return pl.pallas_call(
        functools.partial(_nsa_prompt_kernel, qb=qb, kc=kc),
        grid=(b, t // qb),
        in_specs=[q_spec(NSA_Q), q_spec(NSA_Q), q_spec(GATE_PAD),
                  pl.BlockSpec((None, nch, NSA_KV), lambda i, j: (i, 0, 0)),
                  pl.BlockSpec((None, NSA_KV, nch), lambda i, j: (i, 0, 0)),
                  pl.BlockSpec((ns, nch), lambda i, j: (0, 0)),
                  pl.BlockSpec((None, t, NSA_KV), lambda i, j: (i, 0, 0)),
                  pl.BlockSpec((None, t, NSA_KV), lambda i, j: (i, 0, 2)),
                  chunk_spec, chunk_spec],
        out_specs=pl.BlockSpec((None, qb, NSA_Q), lambda i, j: (i, j, 0)),
        out_shape=jax.ShapeDtypeStruct((b, t, NSA_Q), BF16),
        compiler_params=_cparams(("parallel", "arbitrary"), 48),
        name="nsa_prompt",
    )(p["qut"], p["qrt"], p["gt"], kcv, vct, c2st, katt, katt, chunks(p["vst"]), chunks(p["vwt"])).reshape(b * t, NSA_Q)


def _moba_prompt_kernel(qt_ref, km_ref, k_ref, vt_ref, o_ref, *, blk):
    jq = pl.program_id(2)
    nb = km_ref.shape[0]
    n_top = min(MOBA_TOPK, nb)
    hpl = LANES // HEAD_DIM
    n_heads = qt_ref.shape[0] // HEAD_DIM
    jb = _iota((nb, 1), 0)
    jb_f = jb.astype(F32)
    row_head = lax.shift_right_logical(_iota((LANES, 1), 0), HEAD_DIM.bit_length() - 1)
    tile = lambda hh: slice(hh // hpl * LANES, (hh // hpl + 1) * LANES)
    qz, sels = [], []
    for hh in range(n_heads):
        qt = qt_ref[tile(hh), :].astype(F32) * _QK_SCALE
        qz.append(jnp.where(row_head == hh % hpl, qt, 0.0).astype(BF16))
        score = jnp.where(jb < jq, _mm(km_ref[:, tile(hh)], qz[hh]), NEG)
        sel = jnp.zeros((nb, blk), F32)
        for _ in range(n_top):
            m, _, pick = _take_first_max(score, jb_f, float(nb), 0)
            sel = jnp.where(pick & (m > 0.5 * NEG), 1.0, sel)
            score = jnp.where(pick, 3.0 * NEG, score)
        sels.append(sel)

    def attend(j, carry, mask_fn):
        k0 = pl.multiple_of(j * blk, blk)
        heads = range(n_heads)
        s = [_mm(k_ref[pl.ds(k0, blk), tile(hh)], qz[hh]) for hh in heads]
        m_new, m_sub = zip(*[mask_fn(hh, s[hh], carry[3 * hh]) for hh in heads])
        p = [jnp.exp(s[hh] - m_sub[hh]) for hh in heads]
        pv = [_mm(vt_ref[j, hh * HEAD_DIM:(hh + 1) * HEAD_DIM, :], p[hh]) for hh in heads]
        out = []
        for hh in heads:
            m_i, l_i, acc = carry[3 * hh:3 * hh + 3]
            alpha = jnp.exp(m_i - m_new[hh])
            out += [m_new[hh], alpha * l_i + jnp.sum(p[hh], axis=0, keepdims=True), alpha * acc + pv[hh]]
        return tuple(out)

    def past_mask(j):
        def fn(hh, s, m_i):
            picked = jnp.sum(jnp.where(jb == j, sels[hh], 0.0), axis=0, keepdims=True) > 0.5
            m_new = jnp.maximum(m_i, jnp.where(picked, jnp.max(s, axis=0, keepdims=True), NEG))
            return m_new, jnp.where(picked, m_new, BIG)
        return fn

    init = (jnp.full((1, blk), NEG, F32), jnp.zeros((1, blk), F32), jnp.zeros((HEAD_DIM, blk), F32)) * n_heads
    carry = lax.fori_loop(0, jq, lambda j, c: attend(j, c, past_mask(j)), init)
    causal = _iota((blk, 1), 0) <= _iota((1, blk), 1)

    def own_mask(hh, s, m_i):
        m_new = jnp.maximum(m_i, jnp.max(jnp.where(causal, s, NEG), axis=0, keepdims=True))
        return m_new, jnp.where(causal, m_new, BIG)

    final = attend(jq, carry, own_mask)
    outs = [final[3 * hh + 2] / final[3 * hh + 1] for hh in range(n_heads)]
    o_ref[...] = jnp.transpose(jnp.concatenate(outs, axis=0)).astype(o_ref.dtype)


def _moba_prompt(p, b, t):
    blk = MOBA_BLOCK
    assert t % blk == 0
    nb = t // blk
    w = MOBA_HEADS_PER_STEP * HEAD_DIM
    nstep = MOBA_W // w
    qt = p["qbt"]
    vt = p["vbt"].reshape(b, nb, MOBA_W, blk)
    kv = p["kvb"].reshape(b, t, MOBA_ROW)
    km = p["kmean"].reshape(b, nb, MOBA_W)
    return pl.pallas_call(
        functools.partial(_moba_prompt_kernel, blk=blk),
        grid=(b, nstep, nb),
        in_specs=[pl.BlockSpec((None, w, blk), lambda i, hp, j: (i * nb + j, hp, 0)),
                  pl.BlockSpec((None, nb, w), lambda i, hp, j: (i, 0, hp)),
                  pl.BlockSpec((None, t, w), lambda i, hp, j: (i, 0, hp)),
                  pl.BlockSpec((None, nb, w, blk), lambda i, hp, j: (i, 0, hp, 0))],
        out_specs=pl.BlockSpec((None, blk, w), lambda i, hp, j: (i, j, hp)),
        out_shape=jax.ShapeDtypeStruct((b, t, MOBA_W), BF16),
        compiler_params=_cparams(("parallel", "parallel", "arbitrary"), 56),
        name="moba_prompt",
    )(qt, km, kv, vt).reshape(b * t, MOBA_W)


def _group_rows(n_rows=NSA_HEADS):
    return _iota((n_rows, 1), 0) < NSA_GROUP


def _nsa_dec_cmp_kernel(qu_ref, kcv_ref, c2s_ref, oc_ref, idx_ref, *, pos):
    assert NSA_KV_HEADS == 2
    q = qu_ref[...]
    nch = kcv_ref.shape[0]
    ns = c2s_ref.shape[1]
    n_sel = min(SEL_TOPN, -(-(pos + 1) // SEL_BLOCK))
    scale = HEAD_DIM ** -0.5
    g0 = _group_rows()
    hd = HEAD_DIM
    kcv = kcv_ref[...]
    s = jnp.where(g0, _mm_nt(q, kcv[:, 0:hd]), _mm_nt(q, kcv[:, hd:2 * hd])) * scale
    cend = _iota((1, nch), 1) * CMP_STRIDE + (CMP_BLOCK - 1)
    p = _masked_softmax(s, cend <= pos)
    oc_ref[...] = jnp.where(g0, _mm(p, kcv[:, NSA_KV:NSA_KV + hd]), _mm(p, kcv[:, NSA_KV + hd:NSA_KV + 2 * hd]))
    p0 = jnp.sum(jnp.where(g0, p, 0.0), axis=0, keepdims=True)
    p1 = jnp.sum(jnp.where(g0, 0.0, p), axis=0, keepdims=True)
    imp = _mm(jnp.where(g0, p0, p1), c2s_ref[...])
    jsel = _iota((1, ns), 1)
    jsel_f = jsel.astype(F32)
    jq = pos // SEL_BLOCK
    forced = (jsel == 0) | (jsel == jq) | (jsel == jq - 1)
    score = jnp.where(jsel <= jq, jnp.where(forced, BIG, imp), -1.0)
    lane = _iota((1, LANES), 1)
    idx = jnp.full((NSA_HEADS, LANES), -1, I32)
    for it in range(n_sel):
        m, first, pick = _take_first_max(score, jsel_f, float(ns), 1)
        idx = jnp.where(lane == it, jnp.where(m >= 0.0, first, -1.0).astype(I32), idx)
        score = jnp.where(pick, -2.0, score)
    idx_ref[...] = idx


def _attend_with_new(q, kts, vts, valids, k_new, v_new, new_valid, scale):
    s = [_mm(q, kt) * scale for kt in kts]
    qf = q.astype(MXU_DTYPE).astype(F32)
    s_new = jnp.sum(qf * k_new.astype(MXU_DTYPE).astype(F32), axis=1, keepdims=True) * scale
    m = jnp.where(new_valid, s_new, NEG)
    for sj, vj in zip(s, valids):
        m = jnp.maximum(m, jnp.max(jnp.where(vj > 0.5, sj, NEG), axis=1, keepdims=True))
    p = [jnp.where(vj > 0.5, jnp.exp(sj - m), 0.0) for sj, vj in zip(s, valids)]
    p_new = jnp.where(new_valid, jnp.exp(s_new - m), 0.0)
    d = p_new
    for pj in p:
        d = d + jnp.sum(pj, axis=1, keepdims=True)
    d = jnp.where(d > 0, d, 1.0)
    o = (p_new / d).astype(MXU_DTYPE).astype(F32) * v_new.astype(MXU_DTYPE).astype(F32)
    for pj, vt in zip(p, vts):
        o = o + _mm_nt(pj / d, vt)
    return o


def _nsa_dec_att_kernel(idx_ref, pt_ref, qr_ref, gate_ref, oc_ref, new_ref, win_ref, *rest, pos, n_sel, past_blocks):
    del pt_ref
    blk_refs, o_ref = rest[:-1], rest[-1]
    b = pl.program_id(0)
    q = qr_ref[...]
    scale = HEAD_DIM ** -0.5
    hd = HEAD_DIM
    g0 = _group_rows()
    new = new_ref[...]
    n_win = win_ref.shape[-1]
    wpos = pos - n_win + _iota((1, n_win), 1)
    valid_w = jnp.where((wpos > pos - WINDOW) & (wpos >= 0), 1.0, 0.0)
    page = blk_refs[0].shape[-1]
    bpp = page // SEL_BLOCK
    blk_of_lane = lax.shift_right_logical(_iota((1, page), 1), SEL_BLOCK.bit_length() - 1)
    o_s, o_w = [], []
    for g in range(NSA_KV_HEADS):
        glo = g * hd
        kts, vts, valids = [], [], []
        new_valid = jnp.zeros((1, 1), F32)
        for j in range(n_sel):
            bj = idx_ref[b, g, j]
            kts.append(blk_refs[2 * (g * n_sel + j)][...])
            vts.append(blk_refs[2 * (g * n_sel + j) + 1][...])
            in_past = jnp.where((bj >= 0) & (bj < past_blocks), 1.0, 0.0)
            valids.append(jnp.where(blk_of_lane == bj % bpp, in_past, 0.0))
            new_valid = jnp.maximum(new_valid, jnp.where(bj == past_blocks, 1.0, 0.0))
        o_s.append(_attend_with_new(q, kts, vts, valids, new[:, glo:glo + hd],
                                    new[:, NSA_KV + glo:NSA_KV + glo + hd], new_valid > 0.5, scale))
        o_w.append(_attend_with_new(q, [win_ref[0, g]], [win_ref[1, g]], [valid_w],
                                    new[:, 2 * NSA_KV + glo:2 * NSA_KV + glo + hd],
                                    new[:, 3 * NSA_KV + glo:3 * NSA_KV + glo + hd], True, scale))
    gs = _sigmoid(gate_ref[...])
    o_ref[...] = (gs[:, 0:1] * oc_ref[...] + gs[:, 1:2] * jnp.where(g0, o_s[0], o_s[1])
                  + gs[:, 2:3] * jnp.where(g0, o_w[0], o_w[1]))


def _nsa_decode_select(p, kcv, pos):
    db = kcv.shape[0]
    nch = kcv.shape[1]
    ns = pos // SEL_BLOCK + 1
    ns_pad = -(-ns // LANES) * LANES
    n_sel = min(SEL_TOPN, ns)
    c2s = _cmp_to_sel(nch, ns_pad).astype(BF16)
    per_q = lambda w: pl.BlockSpec((None, NSA_HEADS, w), lambda i: (i, 0, 0))
    o_c, idx = pl.pallas_call(
        functools.partial(_nsa_dec_cmp_kernel, pos=pos),
        grid=(db,),
        in_specs=[per_q(HEAD_DIM), pl.BlockSpec((None, nch, 2 * NSA_KV), lambda i: (i, 0, 0)),
                  pl.BlockSpec((nch, ns_pad), lambda i: (0, 0))],
        out_specs=[per_q(HEAD_DIM), per_q(LANES)],
        out_shape=[jax.ShapeDtypeStruct((db, NSA_HEADS, HEAD_DIM), F32),
                   jax.ShapeDtypeStruct((db, NSA_HEADS, LANES), I32)],
        compiler_params=_cparams(("parallel",)),
        name="nsa_dec_cmp",
    )(p["qu"].reshape(db, NSA_HEADS, HEAD_DIM), kcv, c2s)
    return o_c, idx[:, ::NSA_GROUP, :n_sel]


def _nsa_decode(p, kcv, cache_t, win_t, page_table, pos):
    db, n_pages = page_table.shape
    page = cache_t.shape[-1]
    assert pos == n_pages * page and pos % SEL_BLOCK == 0 and page % SEL_BLOCK == 0
    past_blocks = pos // SEL_BLOCK
    o_c, sel_idx = _nsa_decode_select(p, kcv, pos)
    n_sel = sel_idx.shape[2]
    heads3 = lambda a: a.reshape(db, NSA_HEADS, HEAD_DIM)
    bpp = page // SEL_BLOCK

    def blk_map(i, ix, pt, g, j, part):
        bj = jnp.clip(ix[i, g, j], 0, past_blocks - 1)
        return (pt[i, bj // bpp], part, g, 0, 0)

    k_sel, v_sel = 2, 3
    blk_specs = [pl.BlockSpec((None, None, None, HEAD_DIM, page), functools.partial(blk_map, g=g, j=j, part=part))
                 for g in range(NSA_KV_HEADS) for j in range(n_sel) for part in (k_sel, v_sel)]
    per_q2 = lambda w: pl.BlockSpec((None, NSA_HEADS, w), lambda i, ix, pt: (i, 0, 0))
    n_win = win_t.shape[-1]
    gate3 = p["gate"][:, :3 * NSA_HEADS].reshape(db, NSA_HEADS, 3)
    o = pl.pallas_call(
        functools.partial(_nsa_dec_att_kernel, pos=pos, n_sel=n_sel, past_blocks=past_blocks),
        grid_spec=pltpu.PrefetchScalarGridSpec(
            num_scalar_prefetch=2,
            grid=(db,),
            in_specs=[per_q2(HEAD_DIM), per_q2(3), per_q2(HEAD_DIM),
                      pl.BlockSpec((None, 1, 4 * NSA_KV), lambda i, ix, pt: (i, 0, 0)),
                      pl.BlockSpec((None, 2, NSA_KV_HEADS, HEAD_DIM, n_win), lambda i, ix, pt: (i, 0, 0, 0, 0))]
            + blk_specs,
            out_specs=per_q2(HEAD_DIM),
        ),
        out_shape=jax.ShapeDtypeStruct((db, NSA_HEADS, HEAD_DIM), F32),
        compiler_params=_cparams(("arbitrary",)),
        name="nsa_dec_att",
    )(sel_idx, page_table, heads3(p["qr"]), gate3, o_c, p["katt"].reshape(db, 1, 4 * NSA_KV), win_t,
      *([cache_t] * len(blk_specs)))
    return o.reshape(db, NSA_Q)


def _moba_dec_mean_kernel(*refs, n_src, ppb):
    x_refs, o_ref = refs[1:1 + n_src], refs[1 + n_src]
    j = pl.program_id(1)
    nb = o_ref.shape[1]
    page = x_refs[0].shape[-1]
    bps = n_src // ppb

    @pl.when(j == 0)
    def _():
        o_ref[...] = jnp.zeros(o_ref.shape, F32)

    lane = _iota((1, nb), 1)
    acc = o_ref[...]
    for i in range(bps):
        tot = x_refs[i * ppb][...]
        for r in x_refs[i * ppb + 1:(i + 1) * ppb]:
            tot = tot + r[...]
        col = jnp.sum(tot.reshape(MOBA_W, page), axis=1, keepdims=True) * (1.0 / (ppb * page))
        acc = jnp.where(lane == j * bps + i, col, acc)
    o_ref[...] = acc


def _moba_dec_gate_kernel(q_ref, km_ref, idx_ref, *, jq):
    q = q_ref[...]
    nb = km_ref.shape[1]
    n_top = min(MOBA_TOPK, nb)
    head_of_lane = lax.shift_right_logical(_iota((MOBA_HEADS, MOBA_W), 1), HEAD_DIM.bit_length() - 1)
    qbd = jnp.where(head_of_lane == _iota((MOBA_HEADS, MOBA_W), 0),
                    jnp.broadcast_to(q.astype(F32), (MOBA_HEADS, MOBA_W)), 0.0)
    jb = _iota((1, nb), 1)
    jb_f = jb.astype(F32)
    score = jnp.where(jb < jq, _mm(qbd, km_ref[...]), NEG)
    lane = _iota((1, LANES), 1)
    idx = jnp.full((MOBA_HEADS, LANES), -1, I32)
    for it in range(n_top):
        m, first, pick = _take_first_max(score, jb_f, float(nb), 1)
        idx = jnp.where(lane == it, jnp.where(m > 0.5 * NEG, first, -1.0).astype(I32), idx)
        score = jnp.where(pick, 3.0 * NEG, score)
    idx_ref[...] = idx


def _moba_dec_att_kernel(idx_ref, pt_ref, q_ref, kn_ref, vn_ref, *rest, n_top, ppb):
    del pt_ref
    src, o_ref = rest[:-1], rest[-1]
    b, hp = pl.program_id(0), pl.program_id(1)
    hpl = LANES // HEAD_DIM
    scale = HEAD_DIM ** -0.5
    rows = 8
    page = src[0].shape[-1]
    q_all = jnp.broadcast_to(q_ref[...].astype(F32), (rows, LANES))
    k_new = kn_ref[...]
    v_new = vn_ref[...]
    outs = []
    for hh in range(hpl):
        lanes = slice(hh * HEAD_DIM, (hh + 1) * HEAD_DIM)
        kts, vts, valids = [], [], []
        for t in range(n_top):
            ok = jnp.where(idx_ref[b, hp * hpl + hh, t] >= 0, 1.0, 0.0)
            for pg in range(ppb):
                base = 2 * ((hh * n_top + t) * ppb + pg)
                kts.append(src[base][...])
                vts.append(src[base + 1][...])
                valids.append(jnp.full((1, page), 1.0, F32) * ok)
        o = _attend_with_new(q_all[:, lanes], kts, vts, valids, k_new[:, lanes], v_new[:, lanes], True, scale)
        outs.append(o[0:1])
    o_ref[...] = jnp.concatenate(outs, axis=1)


def _moba_decode(p, cache_t, page_table, pos):
    db, n_pages = page_table.shape
    page = cache_t.shape[-1]
    assert MOBA_BLOCK % page == 0 and pos % MOBA_BLOCK == 0 and pos == n_pages * page
    ppb = MOBA_BLOCK // page
    nb = pos // MOBA_BLOCK
    assert nb >= MOBA_TOPK
    pps = PAGES_PER_STEP
    page_specs = [pl.BlockSpec((None, None, MOBA_HEADS, HEAD_DIM, page), functools.partial(
        lambda i, j, pt, pg: (pt[i, j * pps + pg], 0, 0, 0, 0), pg=pg)) for pg in range(pps)]
    kmean_t = pl.pallas_call(
        functools.partial(_moba_dec_mean_kernel, n_src=pps, ppb=ppb),
        grid_spec=pltpu.PrefetchScalarGridSpec(
            num_scalar_prefetch=1,
            grid=(db, n_pages // pps),
            in_specs=page_specs,
            out_specs=pl.BlockSpec((None, MOBA_W, nb), lambda i, j, pt: (i, 0, 0)),
        ),
        out_shape=jax.ShapeDtypeStruct((db, MOBA_W, nb), F32),
        compiler_params=_cparams(("parallel", "arbitrary")),
        name="moba_dec_mean",
    )(page_table, *([cache_t] * pps))
    idx = pl.pallas_call(
        functools.partial(_moba_dec_gate_kernel, jq=nb),
        grid=(db,),
        in_specs=[pl.BlockSpec((None, 1, MOBA_W), lambda i: (i, 0, 0)),
                  pl.BlockSpec((None, MOBA_W, nb), lambda i: (i, 0, 0))],
        out_specs=pl.BlockSpec((None, MOBA_HEADS, LANES), lambda i: (i, 0, 0)),
        out_shape=jax.ShapeDtypeStruct((db, MOBA_HEADS, LANES), I32),
        compiler_params=_cparams(("parallel",)),
        name="moba_dec_gate",
    )(p["qb"].reshape(db, 1, MOBA_W), kmean_t)
    n_top = min(MOBA_TOPK, nb)
    top_idx = idx[:, :, :n_top]
    hpl = LANES // HEAD_DIM
    npair = MOBA_HEADS // hpl

    def src_map(i, hp, ix, pt, hh, t, pg, kv):
        bj = jnp.clip(ix[i, hp * hpl + hh, t], 0, nb - 1)
        return (pt[i, bj * ppb + pg], kv, hp * hpl + hh, 0, 0)

    src_specs = [pl.BlockSpec((None, None, None, HEAD_DIM, page), functools.partial(src_map, hh=hh, t=t, pg=pg, kv=kv))
                 for hh in range(hpl) for t in range(n_top) for pg in range(ppb) for kv in range(2)]
    pair = lambda off: pl.BlockSpec((None, None, 1, LANES), lambda i, hp, ix, pt: (i, off + hp, 0, 0))
    o = pl.pallas_call(
        functools.partial(_moba_dec_att_kernel, n_top=n_top, ppb=ppb),
        grid_spec=pltpu.PrefetchScalarGridSpec(
            num_scalar_prefetch=2,
            grid=(db, npair),
            in_specs=[pair(0), pair(0), pair(npair)] + src_specs,
            out_specs=pair(0),
        ),
        out_shape=jax.ShapeDtypeStruct((db, npair, 1, LANES), F32),
        compiler_params=_cparams(("arbitrary", "arbitrary")),
        name="moba_dec_att",
    )(top_idx, page_table, p["qb"].reshape(db, npair, 1, LANES), p["kvb"].reshape(db, 2 * npair, 1, LANES),
      p["kvb"].reshape(db, 2 * npair, 1, LANES), *([cache_t] * len(src_specs)))
    return o.reshape(db, MOBA_W)


def _route_t(s_t, b_t):
    n_e, n_tok = s_t.shape
    per = n_e // N_GROUPS
    biased = s_t + b_t
    sub_f = _iota((per, 1), 0).astype(F32)
    gscore = []
    for g in range(N_GROUPS):
        x = biased[g * per:(g + 1) * per]
        m1, _, pick = _take_first_max(x, sub_f, float(per), 0)
        gscore.append(m1 + jnp.max(jnp.where(pick, NEG, x), axis=0, keepdims=True))
    gs = jnp.concatenate(gscore, axis=0)
    g_f = _iota((N_GROUPS, 1), 0).astype(F32)
    gmask = jnp.zeros((N_GROUPS, n_tok), F32)
    for _ in range(TOPK_GROUPS):
        _, _, pick = _take_first_max(gs, g_f, float(N_GROUPS), 0)
        gmask = jnp.where(pick, 1.0, gmask)
        gs = jnp.where(pick, NEG, gs)
    masked = jnp.concatenate([jnp.where(gmask[g:g + 1] > 0.5, biased[g * per:(g + 1) * per], NEG)
                              for g in range(N_GROUPS)], axis=0)
    e_f = _iota((n_e, 1), 0).astype(F32)
    ids, ws = [], []
    for _ in range(TOP_K):
        _, first, pick = _take_first_max(masked, e_f, float(n_e), 0)
        ids.append(first)
        ws.append(jnp.sum(jnp.where(pick, s_t, 0.0), axis=0, keepdims=True))
        masked = jnp.where(pick, 3.0 * NEG, masked)
    w = jnp.concatenate(ws, axis=0)
    w = w / jnp.sum(w, axis=0, keepdims=True) * ROUTED_SCALE
    return jnp.concatenate(ids, axis=0).astype(I32), w


def _merge_kernel(x_ref, oa_ref, ob_ref, sc1_ref, sh1_ref, g1_ref, sc2_ref, sh2_ref, wmg_ref, wa_ref, wb_ref,
                  wo_ref, lg_ref, lb_ref, wr_ref, br_ref, x1_ref, h_ref, ti_ref, tw_ref, *, alpha):
    x = x_ref[...]
    d = x.shape[1]
    u = x * (1.0 + sc1_ref[...]) + sh1_ref[...]
    mg = _mm(u, wmg_ref[...])
    y_a = _mm(oa_ref[...], wa_ref[...])
    y_b = _mm(ob_ref[...], wb_ref[...])
    mix = _mm(_sigmoid(mg[:, :d]) * y_a + _sigmoid(mg[:, d:]) * y_b, wo_ref[...])
    x1 = _layer_norm(alpha * x + g1_ref[...] * mix, lg_ref[...], lb_ref[...])
    x1_ref[...] = x1
    h = x1 * (1.0 + sc2_ref[...]) + sh2_ref[...]
    h_ref[...] = h
    s_t = _sigmoid(_mm_nt(wr_ref[...], h))
    ti_ref[...], tw_ref[...] = _route_t(s_t, br_ref[...])


def _merge(x, o_a, o_b, mods, w, *, tm, per_token_mod, rows_per_batch, alpha):
    n, d = x.shape
    nt = n // tm
    tiles_per_batch = rows_per_batch // tm
    row = lambda i: (i, 0)
    if per_token_mod:
        mod_spec = pl.BlockSpec((tm, d), row)
    else:
        mod_spec = pl.BlockSpec((None, 1, d), lambda i: (i // tiles_per_batch, 0, 0))
    full = lambda a: pl.BlockSpec(a.shape, lambda i: (0,) * a.ndim)
    ws = [w["w_mg"], w["w_nsa_out"], w["w_moba_out"], w["w_o"], w["ln1_g"], w["ln1_b"], w["w_router_t"], w["b_router"]]
    return pl.pallas_call(
        functools.partial(_merge_kernel, alpha=alpha),
        grid=(nt,),
        in_specs=[pl.BlockSpec((tm, d), row), pl.BlockSpec((tm, NSA_Q), row), pl.BlockSpec((tm, MOBA_W), row)]
        + [mod_spec] * 5 + [full(a) for a in ws],
        out_specs=[pl.BlockSpec((tm, d), row), pl.BlockSpec((tm, d), row),
                   pl.BlockSpec((TOP_K, tm), lambda i: (0, i)), pl.BlockSpec((TOP_K, tm), lambda i: (0, i))],
        out_shape=[jax.ShapeDtypeStruct((n, d), F32), jax.ShapeDtypeStruct((n, d), F32),
                   jax.ShapeDtypeStruct((TOP_K, n), I32), jax.ShapeDtypeStruct((TOP_K, n), F32)],
        compiler_params=_cparams(("parallel",), 48),
        name="merge",
    )(x, o_a, o_b, *mods, *ws)


def _expert_kernel(be_ref, nu_ref, x_ref, wg_ref, wu_ref, wd_ref, y_ref):
    del be_ref

    @pl.when(pl.program_id(0) < nu_ref[0])
    def _():
        x = x_ref[...]
        y_ref[...] = _mm(_silu(_mm(x, wg_ref[...])) * _mm(x, wu_ref[...]), wd_ref[...])


def _dispatch(top_i, blk):
    k, n = top_i.shape
    a = k * n
    flat_e = top_i.reshape(-1)
    order = jnp.argsort(flat_e)
    e_sorted = flat_e[order]
    tok = (order % n).astype(I32)
    bounds = jnp.searchsorted(e_sorted, jnp.arange(N_EXPERTS + 1, dtype=I32), side="left").astype(I32)
    counts = bounds[1:] - bounds[:-1]
    padded = (counts + blk - 1) // blk * blk
    end_pad = jnp.cumsum(padded)
    start_pad = end_pad - padded
    dest = start_pad[e_sorted] + jnp.arange(a, dtype=I32) - bounds[:-1][e_sorted]
    nblk = -(-a // blk) + N_EXPERTS
    row_tok = jnp.zeros((nblk * blk,), I32).at[dest].set(tok)
    blk_e = jnp.minimum(jnp.searchsorted(end_pad, jnp.arange(nblk, dtype=I32) * blk, side="right"),
                        N_EXPERTS - 1).astype(I32)
    slot = jnp.zeros((a,), I32).at[order].set(dest).reshape(k, n)
    n_used = (end_pad[-1] // blk).astype(I32).reshape(1)
    return row_tok, blk_e, slot, n_used, nblk


def _experts(h, top_i, top_w, w):
    n, d = h.shape
    blk = min(EXPERT_BLK, max(8, (TOP_K * n) // N_EXPERTS))
    row_tok, blk_e, slot, n_used, nblk = _dispatch(top_i, blk)
    x_sorted = jnp.take(h, row_tok, axis=0)
    de = w["w_exp_gate"].shape[2]
    y = pl.pallas_call(
        _expert_kernel,
        grid_spec=pltpu.PrefetchScalarGridSpec(
            num_scalar_prefetch=2,
            grid=(nblk,),
            in_specs=[pl.BlockSpec((blk, d), lambda i, be, nu: (i, 0)),
                      pl.BlockSpec((None, d, de), lambda i, be, nu: (be[i], 0, 0)),
                      pl.BlockSpec((None, d, de), lambda i, be, nu: (be[i], 0, 0)),
                      pl.BlockSpec((None, de, d), lambda i, be, nu: (be[i], 0, 0))],
            out_specs=pl.BlockSpec((blk, d), lambda i, be, nu: (i, 0)),
        ),
        out_shape=jax.ShapeDtypeStruct((nblk * blk, d), F32),
        compiler_params=_cparams(("arbitrary",), 48),
        name="experts",
    )(blk_e, n_used, x_sorted, w["w_exp_gate"], w["w_exp_up"], w["w_exp_down"])
    routed = jnp.zeros((n, d), F32)
    for k in range(TOP_K):
        routed = routed + jnp.take(y, slot[k], axis=0) * top_w[k][:, None]
    return routed


def _final_kernel(x1_ref, h_ref, r_ref, g2_ref, wg_ref, wu_ref, wd_ref, lg_ref, lb_ref, o_ref, *, alpha):
    h = h_ref[...]
    f = r_ref[...] + _mm(_silu(_mm(h, wg_ref[...])) * _mm(h, wu_ref[...]), wd_ref[...])
    o_ref[...] = _layer_norm(alpha * x1_ref[...] + g2_ref[...] * f, lg_ref[...], lb_ref[...])


def _final(x1, h, routed, g2, w, *, tm, per_token_mod, rows_per_batch, alpha):
    n, d = x1.shape
    tiles_per_batch = rows_per_batch // tm
    row = lambda i: (i, 0)
    tile = pl.BlockSpec((tm, d), row)
    mod_spec = tile if per_token_mod else pl.BlockSpec((None, 1, d), lambda i: (i // tiles_per_batch, 0, 0))
    full = lambda a: pl.BlockSpec(a.shape, lambda i: (0,) * a.ndim)
    ws = [w["w_sh_gate"], w["w_sh_up"], w["w_sh_down"], w["ln2_g"], w["ln2_b"]]
    return pl.pallas_call(
        functools.partial(_final_kernel, alpha=alpha),
        grid=(n // tm,),
        in_specs=[tile, tile, tile, mod_spec] + [full(a) for a in ws],
        out_specs=tile,
        out_shape=jax.ShapeDtypeStruct((n, d), F32),
        compiler_params=_cparams(("parallel",)),
        name="final",
    )(x1, h, routed, g2, *ws)


def _prep_weights(lp):
    (w_ada, b_ada, w_in, cmp_k_w1, cmp_k_w2, cmp_k_pe, cmp_v_w1, cmp_v_w2, cmp_v_pe, w_nsa_out, w_moba_out, w_o,
     ln1_g, ln1_b, w_router, b_router, w_exp_gate, w_exp_up, w_exp_down, w_sh_gate, w_sh_up, w_sh_down,
     ln2_g, ln2_b) = lp
    c = lambda a: a.astype(MXU_DTYPE)
    row = lambda a: a.reshape(1, -1)
    w_r, w_mg = _reorder_w_in(w_in)
    wk, w2k, pek = _cmp_weights(cmp_k_w1, cmp_k_w2, cmp_k_pe)
    wv, w2v, pev = _cmp_weights(cmp_v_w1, cmp_v_w2, cmp_v_pe)
    return dict(w_ada=w_ada, b_ada=b_ada, w_r=w_r, w_mg=w_mg, cmp=(wk, w2k, pek, wv, w2v, pev),
                w_nsa_out=c(w_nsa_out), w_moba_out=c(w_moba_out), w_o=c(w_o), ln1_g=row(ln1_g), ln1_b=row(ln1_b),
                w_router_t=c(w_router.T), b_router=b_router.reshape(-1, 1),
                w_exp_gate=c(w_exp_gate), w_exp_up=c(w_exp_up), w_exp_down=c(w_exp_down),
                w_sh_gate=c(w_sh_gate), w_sh_up=c(w_sh_up), w_sh_down=c(w_sh_down), ln2_g=row(ln2_g), ln2_b=row(ln2_b))


def _token_tail(x, o_a, o_b, mods, w, *, tm, per_token_mod, rows_per_batch, alpha):
    sc1, sh1, g1, sh2, sc2, g2 = mods
    kw = dict(tm=tm, per_token_mod=per_token_mod, rows_per_batch=rows_per_batch, alpha=alpha)
    x1, h, top_i, top_w = _merge(x, o_a, o_b, (sc1, sh1, g1, sc2, sh2), w, **kw)
    routed = _experts(h, top_i, top_w, w)
    return _final(x1, h, routed, g2, w, **kw)


def _layer(xp, xs, c_all, cache_nsa_l, cache_moba_l, win_state_l, page_table, w, alpha):
    b, t, d = xp.shape
    db, ts, _ = xs.shape
    assert ts == 1
    page = cache_nsa_l.shape[1]
    pos = page_table.shape[1] * page
    assert win_state_l.shape[1] == WINDOW and t >= WINDOW
    mod = _ada(c_all, w["w_ada"], w["b_ada"])
    pieces = [mod[:, i * d:(i + 1) * d] for i in range(6)]
    mods_p = [m[:b].reshape(b, 1, d) for m in pieces]
    mods_s = [m[b:b + db] for m in pieces]
    order = lambda m: (m[1], m[0], m[2], m[3], m[4], m[5])
    mods_p, mods_s = order(mods_p), order(mods_s)

    tm = TOKEN_TILE
    xp2 = xp.reshape(b * t, d)
    pp = _inproj(xp2, mods_p[0], mods_p[1], w["w_r"], _rope_tables(jnp.arange(t)), tm=tm, per_token_mod=False,
                 rows_per_batch=t, with_kmean=True)
    kcv_p, vct_p = _cmp_prompt(pp["nsa"].reshape(b, t, NSA_ROW), w["cmp"])
    oa_p = _nsa_prompt(pp, kcv_p, vct_p, b, t)
    ob_p = _moba_prompt(pp, b, t)
    yp = _token_tail(xp2, oa_p, ob_p, mods_p, w, tm=tm, per_token_mod=False, rows_per_batch=t, alpha=alpha)

    xs2 = xs.reshape(db, d)
    ps = _inproj(xs2, mods_s[0], mods_s[1], w["w_r"], _rope_tables(jnp.full((db,), pos)), tm=db, per_token_mod=True,
                 rows_per_batch=db, with_kmean=False)
    nsa_pages = cache_nsa_l.reshape(cache_nsa_l.shape[0], page, NSA_ROW)
    kcv_s, _ = _cmp_decode(nsa_pages, page_table, w["cmp"])
    token_minor = lambda a: jnp.transpose(a, (0, 2, 3, 4, 1))
    oa_s = _nsa_decode(ps, kcv_s, token_minor(cache_nsa_l), token_minor(win_state_l), page_table, pos)
    ob_s = _moba_decode(ps, token_minor(cache_moba_l), page_table, pos)
    ys = _token_tail(xs2, oa_s, ob_s, mods_s, w, tm=db, per_token_mod=True, rows_per_batch=db, alpha=alpha)

    g, hd = NSA_KV_HEADS, HEAD_DIM
    win_p = pp["win"].reshape(b, t, 2, g, hd)[:, t - WINDOW:]
    win_s = jnp.concatenate([win_state_l[:, 1:], ps["win"].reshape(db, 1, 2, g, hd)], axis=1)
    return (yp.reshape(b, t, d), ys.reshape(db, 1, d),
            pp["nsa"].reshape(b, t, 4, g, hd), ps["nsa"].reshape(db, 1, 4, g, hd),
            pp["moba"].reshape(b, t, 2, MOBA_HEADS, hd), ps["moba"].reshape(db, 1, 2, MOBA_HEADS, hd), win_p, win_s)


def kernel(x_prompt, x_sample, cache_nsa, cache_moba, state_nsa_win, page_table, c_prompt, c_sample, w_ada, b_ada,
           w_in, cmp_k_w1, cmp_k_w2, cmp_k_pe, cmp_v_w1, cmp_v_w2, cmp_v_pe, w_nsa_out, w_moba_out, w_o, ln1_g,
           ln1_b, w_router, b_router, w_exp_gate, w_exp_up, w_exp_down, w_sh_gate, w_sh_up, w_sh_down, ln2_g,
           ln2_b):
    params = (w_ada, b_ada, w_in, cmp_k_w1, cmp_k_w2, cmp_k_pe, cmp_v_w1, cmp_v_w2, cmp_v_pe, w_nsa_out, w_moba_out,
              w_o, ln1_g, ln1_b, w_router, b_router, w_exp_gate, w_exp_up, w_exp_down, w_sh_gate, w_sh_up,
              w_sh_down, ln2_g, ln2_b)
    depth = w_ada.shape[0]
    alpha = (2 * depth) ** 0.25
    b, db = x_prompt.shape[0], x_sample.shape[0]
    rows = -(-(b + db) // 8) * 8
    c_all = jnp.pad(jnp.concatenate([c_prompt, c_sample], axis=0), ((0, rows - b - db), (0, 0)))
    xp, xs = x_prompt, x_sample
    outs = [[] for _ in range(6)]
    for l in range(depth):
        w = _prep_weights([p[l] for p in params])
        res = _layer(xp, xs, c_all, cache_nsa[l], cache_moba[l], state_nsa_win[l], page_table, w, alpha)
        xp, xs = res[0], res[1]
        for acc, r in zip(outs, res[2:]):
            acc.append(r)
    return (xp, xs) + tuple(jnp.stack(o) for o in outs)
```

```python
import functools

import jax
import jax.numpy as jnp
from jax import lax
from jax.experimental import pallas as pl
from jax.experimental.pallas import tpu as pltpu

F32 = jnp.float32
BF16 = jnp.bfloat16
I32 = jnp.int32
MXU_DTYPE = jnp.bfloat16

HEAD_DIM = 64
ROPE_DIMS = HEAD_DIM // 4
ROPE_HALF = ROPE_DIMS // 2
ROPE_THETA = 500000.0
NSA_HEADS = 8
NSA_KV_HEADS = 2
NSA_GROUP = NSA_HEADS // NSA_KV_HEADS
CMP_BLOCK = 32
CMP_STRIDE = 16
CMP_RATIO = CMP_BLOCK // CMP_STRIDE
SEL_BLOCK = 64
SEL_TOPN = 16
WINDOW = 512
MOBA_HEADS = 8
MOBA_BLOCK = 256
MOBA_TOPK = 3
N_EXPERTS = 64
TOP_K = 8
N_GROUPS = 8
TOPK_GROUPS = 4
ROUTED_SCALE = 2.5
LN_EPS = 1e-5

LANES = 128
NSA_Q = NSA_HEADS * HEAD_DIM
NSA_KV = NSA_KV_HEADS * HEAD_DIM
MOBA_W = MOBA_HEADS * HEAD_DIM
NSA_ROW = 4 * NSA_KV
MOBA_ROW = 2 * MOBA_W
GATE_PAD = LANES

_QK_SCALE = HEAD_DIM ** -0.5
assert _QK_SCALE == 0.125
NEG = -1e30
BIG = 3e38
TOKEN_TILE = 256
NSA_QB = 128
EXPERT_BLK = 256
RANK_TILE = 512
ROW_COPY_TILE = 128
MOBA_HEADS_PER_STEP = 8
PAGES_PER_STEP = 16


def _sigmoid(x):
    return 1.0 / (1.0 + jnp.exp(-x))


def _silu(x):
    return x * _sigmoid(x)


def _mm(a, b):
    return jnp.dot(a.astype(MXU_DTYPE), b.astype(MXU_DTYPE), preferred_element_type=F32)


def _mm_nt(a, b):
    return lax.dot_general(a.astype(MXU_DTYPE), b.astype(MXU_DTYPE), (((1,), (1,)), ((), ())),
                           preferred_element_type=F32)


def _iota(shape, axis):
    return lax.broadcasted_iota(I32, shape, axis)


def _cparams(sem, vmem_mb=None):
    kw = dict(dimension_semantics=sem)
    if vmem_mb is not None:
        kw["vmem_limit_bytes"] = vmem_mb << 20
    return pltpu.CompilerParams(**kw)


def _masked_softmax(s, mask, axis=-1):
    m = jnp.max(jnp.where(mask, s, NEG), axis=axis, keepdims=True)
    e = jnp.where(mask, jnp.exp(s - m), 0.0)
    d = jnp.sum(e, axis=axis, keepdims=True)
    return e / jnp.where(d > 0, d, 1.0)


def _layer_norm(z, g, b):
    mu = jnp.mean(z, axis=-1, keepdims=True)
    zc = z - mu
    var = jnp.mean(zc * zc, axis=-1, keepdims=True)
    return zc * lax.rsqrt(var + LN_EPS) * g + b


def _take_first_max(score, idx_f, n_f, axis):
    m = jnp.max(score, axis=axis, keepdims=True)
    first = jnp.min(jnp.where(score == m, idx_f, n_f), axis=axis, keepdims=True)
    return m, first, idx_f == first


def _ada_kernel(c_ref, w_ref, b_ref, o_ref):
    a = _silu(c_ref[...])
    o_ref[...] = _mm(a, w_ref[...]) + b_ref[...]


def _ada(c_all, w_ada, b_ada):
    r, d = c_all.shape
    e6 = w_ada.shape[1]
    tn = 1024
    return pl.pallas_call(
        _ada_kernel,
        grid=(e6 // tn,),
        in_specs=[pl.BlockSpec((r, d), lambda j: (0, 0)),
                  pl.BlockSpec((d, tn), lambda j: (0, j)),
                  pl.BlockSpec((1, tn), lambda j: (0, j))],
        out_specs=pl.BlockSpec((r, tn), lambda j: (0, j)),
        out_shape=jax.ShapeDtypeStruct((r, e6), F32),
        compiler_params=_cparams(("arbitrary",)),
        name="ada",
    )(c_all, w_ada, b_ada.reshape(1, e6))


_C_QA = 0
_C_KVA = _C_QA + NSA_Q
_C_QB = _C_KVA + 6 * NSA_KV
_C_KB = _C_QB + MOBA_W
_C_VB = _C_KB + MOBA_W
_C_GATE = _C_VB + MOBA_W
_C_END = _C_GATE + GATE_PAD


def _rope(x, cs, s1, s2):
    parts = []
    for j in range(x.shape[1] // LANES):
        xj = x[:, j * LANES:(j + 1) * LANES]
        parts.append(xj * cs + pltpu.roll(xj, ROPE_HALF, 1) * s1 + pltpu.roll(xj, LANES - ROPE_HALF, 1) * s2)
    return parts[0] if len(parts) == 1 else jnp.concatenate(parts, axis=1)


def _inproj_kernel(x_ref, sc_ref, sh_ref, w_ref, cs_ref, s1_ref, s2_ref,
                   qu_ref, qr_ref, qb_ref, nsa_ref, katt_ref, win_ref, moba_ref, kvb_ref, g_ref, *km_ref):
    u = x_ref[...] * (1.0 + sc_ref[...]) + sh_ref[...]
    r = _mm(u, w_ref[...])
    cs, s1, s2 = cs_ref[...], s1_ref[...], s2_ref[...]
    rope = lambda v: _rope(v, cs, s1, s2)
    qa = r[:, _C_QA:_C_KVA]
    qa_rot = rope(qa)
    qu_ref[...] = qa.astype(BF16)
    qr_ref[...] = qa_rot.astype(BF16)
    o = _C_KVA
    kc_vc = r[:, o:o + 2 * NSA_KV]
    ks = rope(r[:, o + 2 * NSA_KV:o + 3 * NSA_KV])
    vs = r[:, o + 3 * NSA_KV:o + 4 * NSA_KV]
    kw = rope(r[:, o + 4 * NSA_KV:o + 5 * NSA_KV])
    vw = r[:, o + 5 * NSA_KV:o + 6 * NSA_KV]
    nsa_ref[...] = jnp.concatenate([kc_vc, ks, vs], axis=1)
    win_ref[...] = jnp.concatenate([kw, vw], axis=1)
    katt_ref[...] = jnp.concatenate([ks, vs, kw, vw], axis=1).astype(BF16)
    qb = rope(r[:, _C_QB:_C_KB])
    qb_ref[...] = qb.astype(BF16)
    kb = rope(r[:, _C_KB:_C_VB])
    vb = r[:, _C_VB:_C_GATE]
    mb = jnp.concatenate([kb, vb], axis=1)
    moba_ref[...] = mb
    kvb_ref[...] = mb.astype(BF16)
    gate = r[:, _C_GATE:_C_END]
    g_ref[...] = gate
    if km_ref:
        kmean_ref, qbt_ref, vbt_ref, qut_ref, qrt_ref, gt_ref, vst_ref, vwt_ref = km_ref
        kmean_ref[...] = jnp.sum(kb, axis=0, keepdims=True) * (1.0 / kb.shape[0])
        tr = lambda v: jnp.transpose(v).astype(BF16)
        qbt_ref[...] = tr(qb)
        vbt_ref[...] = tr(vb)
        qut_ref[...] = tr(qa)
        qrt_ref[...] = tr(qa_rot)
        gt_ref[...] = jnp.transpose(gate)
        vst_ref[...] = tr(vs)
        vwt_ref[...] = tr(vw)


def _inproj(x, sc, sh, w_r, rope_tabs, *, tm, per_token_mod, rows_per_batch, with_kmean):
    n, d = x.shape
    nt = n // tm
    tiles_per_batch = rows_per_batch // tm
    row = lambda i: (i, 0)
    if per_token_mod:
        mod_spec = pl.BlockSpec((tm, d), row)
    else:
        mod_spec = pl.BlockSpec((None, 1, d), lambda i: (i // tiles_per_batch, 0, 0))
    tab_spec = pl.BlockSpec((tm, LANES), lambda i: (i % tiles_per_batch, 0))
    outs = [("qu", NSA_Q, BF16), ("qr", NSA_Q, BF16), ("qb", MOBA_W, BF16), ("nsa", NSA_ROW, F32),
            ("katt", 4 * NSA_KV, BF16), ("win", 2 * NSA_KV, F32), ("moba", MOBA_ROW, F32),
            ("kvb", MOBA_ROW, BF16), ("gate", GATE_PAD, F32)]
    out_shape = [jax.ShapeDtypeStruct((n, w), dt) for _, w, dt in outs]
    out_specs = [pl.BlockSpec((tm, w), row) for _, w, _ in outs]
    extra = []
    if with_kmean:
        assert tm == MOBA_BLOCK
        extra = ["kmean"]
        out_shape.append(jax.ShapeDtypeStruct((nt, 1, MOBA_W), F32))
        out_specs.append(pl.BlockSpec((None, 1, MOBA_W), lambda i: (i, 0, 0)))
        for name, w, dt in [("qbt", MOBA_W, BF16), ("vbt", MOBA_W, BF16), ("qut", NSA_Q, BF16), ("qrt", NSA_Q, BF16),
                            ("gt", GATE_PAD, F32), ("vst", NSA_KV, BF16), ("vwt", NSA_KV, BF16)]:
            extra.append(name)
            out_shape.append(jax.ShapeDtypeStruct((nt, w, tm), dt))
            out_specs.append(pl.BlockSpec((None, w, tm), lambda i: (i, 0, 0)))
    res = pl.pallas_call(
        _inproj_kernel,
        grid=(nt,),
        in_specs=[pl.BlockSpec((tm, d), row), mod_spec, mod_spec,
                  pl.BlockSpec(w_r.shape, lambda i: (0, 0)), tab_spec, tab_spec, tab_spec],
        out_specs=out_specs,
        out_shape=out_shape,
        compiler_params=_cparams(("parallel",), 48),
        name="inproj",
    )(x, sc, sh, w_r, *rope_tabs)
    return dict(zip([o[0] for o in outs] + extra, res))


def _rope_tables(pos):
    inv = ROPE_THETA ** (-jnp.arange(ROPE_HALF, dtype=F32) / ROPE_HALF)
    ang = pos.astype(F32)[:, None] * inv
    cos, sin = jnp.cos(ang), jnp.sin(ang)
    rows = pos.shape[0]
    one = jnp.ones((rows, HEAD_DIM - ROPE_DIMS), F32)
    zero = jnp.zeros((rows, HEAD_DIM - ROPE_DIMS), F32)
    zh = jnp.zeros((rows, ROPE_HALF), F32)
    cs = jnp.concatenate([cos, cos, one], axis=1)
    s1 = jnp.concatenate([zh, sin, zero], axis=1)
    s2 = jnp.concatenate([-sin, zh, zero], axis=1)
    rep = LANES // HEAD_DIM
    return tuple(jnp.tile(t, (1, rep)) for t in (cs, s1, s2))


def _reorder_w_in(w_in):
    d = w_in.shape[0]
    sizes = (NSA_Q, 6 * NSA_KV, 3 * NSA_HEADS, MOBA_W, MOBA_W, MOBA_W, d, d)
    offs = [0]
    for s in sizes:
        offs.append(offs[-1] + s)
    q_a, kv_a, gate, q_b, k_b, v_b, mg_a, mg_b = [w_in[:, offs[i]:offs[i + 1]] for i in range(8)]
    gate = jnp.pad(gate, ((0, 0), (0, GATE_PAD - 3 * NSA_HEADS)))
    w_r = jnp.concatenate([q_a, kv_a, q_b, k_b, v_b, gate], axis=1).astype(MXU_DTYPE)
    w_mg = jnp.concatenate([mg_a, mg_b], axis=1).astype(MXU_DTYPE)
    return w_r, w_mg


_CHUNK_W = CMP_STRIDE * NSA_ROW


def _cmp1_kernel(*refs, n_src, n_prefetch=0):
    refs = refs[n_prefetch:]
    x_refs = refs[:n_src]
    wk_ref, wv_ref, pek_ref, pev_ref, o_ref = refs[n_src:]

    def gather(off):
        cols = []
        for s in range(CMP_STRIDE):
            lo = s * NSA_ROW + off
            pieces = [xr[:, lo:lo + NSA_KV] for xr in x_refs]
            cols.append(pieces[0] if n_src == 1 else jnp.concatenate(pieces, axis=0))
        return jnp.concatenate(cols, axis=1)

    xk = gather(0)
    xv = gather(NSA_KV)
    outs = []
    for r in range(CMP_RATIO):
        outs.append(_mm(xk + pek_ref[r], wk_ref[r]))
        outs.append(_mm(xv + pev_ref[r], wv_ref[r]))
    o_ref[...] = jnp.concatenate(outs, axis=1)


def _cmp2_kernel(a_ref, w2k_ref, w2v_ref, o_ref, vt_ref):
    a = a_ref[...]
    n = a.shape[0]
    w = NSA_KV
    hk = a[:, 0:w] + pltpu.roll(a[:, 2 * w:3 * w], n - 1, 0)
    hv = a[:, w:2 * w] + pltpu.roll(a[:, 3 * w:4 * w], n - 1, 0)
    vc = _mm(_silu(hv), w2v_ref[...])
    o_ref[...] = jnp.concatenate([_mm(_silu(hk), w2k_ref[...]), vc], axis=1)
    vt_ref[...] = jnp.transpose(vc)


def _blockdiag(w, reps):
    k, n = w.shape
    out = jnp.zeros((reps * k, reps * n), w.dtype)
    for g in range(reps):
        out = out.at[g * k:(g + 1) * k, g * n:(g + 1) * n].set(w)
    return out


def _cmp_weights(w1, w2, pe):
    g = NSA_KV_HEADS
    w1s = jnp.stack([jnp.concatenate([_blockdiag(w1[r, s], g) for s in range(CMP_STRIDE)], axis=0)
                     for r in range(CMP_RATIO)]).astype(MXU_DTYPE)
    pes = jnp.tile(pe[:, :, None, :], (1, 1, g, 1)).reshape(CMP_RATIO, 1, CMP_STRIDE * NSA_KV)
    return w1s, _blockdiag(w2, g).astype(MXU_DTYPE), pes


def _cmp_stage2(a, w2k, w2v):
    b, nch, _ = a.shape
    return pl.pallas_call(
        _cmp2_kernel,
        grid=(b,),
        in_specs=[pl.BlockSpec((None, nch, 4 * NSA_KV), lambda i: (i, 0, 0)),
                  pl.BlockSpec(w2k.shape, lambda i: (0, 0)), pl.BlockSpec(w2v.shape, lambda i: (0, 0))],
        out_specs=[pl.BlockSpec((None, nch, 2 * NSA_KV), lambda i: (i, 0, 0)),
                   pl.BlockSpec((None, NSA_KV, nch), lambda i: (i, 0, 0))],
        out_shape=[jax.ShapeDtypeStruct((b, nch, 2 * NSA_KV), F32), jax.ShapeDtypeStruct((b, NSA_KV, nch), F32)],
        compiler_params=_cparams(("parallel",)),
        name="cmp2",
    )(a, w2k, w2v)


def _cmp_prompt(nsa_rows, cw):
    b, t, _ = nsa_rows.shape
    nch = t // CMP_STRIDE
    tc = min(128, nch)
    x = nsa_rows.reshape(b, nch, _CHUNK_W)
    wk, w2k, pek, wv, w2v, pev = cw
    full = lambda a: pl.BlockSpec(a.shape, lambda i, j: (0,) * a.ndim)
    a = pl.pallas_call(
        functools.partial(_cmp1_kernel, n_src=1),
        grid=(b, nch // tc),
        in_specs=[pl.BlockSpec((None, tc, _CHUNK_W), lambda i, j: (i, j, 0)),
                  full(wk), full(wv), full(pek), full(pev)],
        out_specs=pl.BlockSpec((None, tc, 4 * NSA_KV), lambda i, j: (i, j, 0)),
        out_shape=jax.ShapeDtypeStruct((b, nch, 4 * NSA_KV), F32),
        compiler_params=_cparams(("parallel", "parallel"), 48),
        name="cmp1_prompt",
    )(x, wk, wv, pek, pev)
    return _cmp_stage2(a, w2k, w2v)


def _cmp_decode(cache_pages, page_table, cw):
    n_pool, page, _ = cache_pages.shape
    db, n_pages = page_table.shape
    cpp = page // CMP_STRIDE
    pps = PAGES_PER_STEP
    x = cache_pages.reshape(n_pool, cpp, _CHUNK_W)
    wk, w2k, pek, wv, w2v, pev = cw
    nch = n_pages * cpp
    full = lambda a: pl.BlockSpec(a.shape, lambda i, j, pt: (0,) * a.ndim)
    page_specs = [pl.BlockSpec((None, cpp, _CHUNK_W), functools.partial(
        lambda i, j, pt, p: (pt[i, j * pps + p], 0, 0), p=p)) for p in range(pps)]
    a = pl.pallas_call(
        functools.partial(_cmp1_kernel, n_src=pps, n_prefetch=1),
        grid_spec=pltpu.PrefetchScalarGridSpec(
            num_scalar_prefetch=1,
            grid=(db, n_pages // pps),
            in_specs=page_specs + [full(wk), full(wv), full(pek), full(pev)],
            out_specs=pl.BlockSpec((None, pps * cpp, 4 * NSA_KV), lambda i, j, pt: (i, j, 0)),
        ),
        out_shape=jax.ShapeDtypeStruct((db, nch, 4 * NSA_KV), F32),
        compiler_params=_cparams(("parallel", "parallel"), 48),
        name="cmp1_decode",
    )(page_table, *([x] * pps), wk, wv, pek, pev)
    return _cmp_stage2(a, w2k, w2v)


def _cmp_to_sel(nch, ns_pad):
    cs = jnp.arange(nch) * CMP_STRIDE
    ss = jnp.arange(ns_pad) * SEL_BLOCK
    return ((cs[:, None] < ss[None] + SEL_BLOCK) & (cs[:, None] + CMP_BLOCK > ss[None])).astype(F32)


def _nsa_prompt_kernel(qut_ref, qrt_ref, gt_ref, kc_ref, vct_ref, c2st_ref, ks_ref, kw_ref, vst_ref, vwt_ref, o_ref, *,
                       qb, kc):
    s0 = pl.program_id(1) * qb
    nkc = vst_ref.shape[0]
    nch = kc_ref.shape[0]
    ns = c2st_ref.shape[0]
    n_sel = min(SEL_TOPN, ns)
    hg, hd, groups = NSA_GROUP, HEAD_DIM, range(NSA_KV_HEADS)
    sel_shift = SEL_BLOCK.bit_length() - 1
    pos_q = s0 + _iota((1, qb), 1)
    pos_l = jnp.concatenate([pos_q] * hg, axis=1)
    gates = _sigmoid(gt_ref[...])
    zeros = jnp.zeros((hd, hg * qb), F32)

    def group_q(ref, g):
        x = jnp.concatenate([ref[(g * hg + h) * hd:(g * hg + h + 1) * hd, :] for h in range(hg)], axis=1)
        parts = [zeros] * NSA_KV_HEADS
        parts[g] = x.astype(F32) * _QK_SCALE
        return jnp.concatenate(parts, axis=0).astype(BF16)

    qzu = [group_q(qut_ref, g) for g in groups]
    qzr = [group_q(qrt_ref, g) for g in groups]
    cend = _iota((nch, 1), 0) * CMP_STRIDE + (CMP_BLOCK - 1)
    kcb = kc_ref[...]
    p_c = [_masked_softmax(_mm(kcb, qzu[g]), cend <= pos_l, axis=0) for g in groups]
    o_c = [_mm(vct_ref[g * hd:(g + 1) * hd, :], p_c[g]) for g in groups]
    jsel = _iota((ns, 1), 0)
    jsel_f = jsel.astype(F32)
    jq = lax.shift_right_logical(pos_q, sel_shift)
    forced = (jsel == 0) | (jsel == jq) | (jsel == jq - 1)
    sel_b = []
    for g in groups:
        p_sum = p_c[g][:, 0:qb]
        for h in range(1, hg):
            p_sum = p_sum + p_c[g][:, h * qb:(h + 1) * qb]
        score = jnp.where(jsel <= jq, jnp.where(forced, BIG, _mm(c2st_ref[...], p_sum)), -1.0)
        sel = jnp.zeros((ns, qb), F32)
        for _ in range(n_sel):
            m, _, pick = _take_first_max(score, jsel_f, float(ns), 0)
            sel = jnp.where(pick & (m >= 0.0), 1.0, sel)
            score = jnp.where(pick, -2.0, score)
        sel_b.append(sel.astype(BF16))
    blk_lane = _iota((1, ns), 1)

    def sel_step(c, carry):
        k0 = pl.multiple_of(c * kc, kc)
        kpos = k0 + _iota((kc, 1), 0)
        expand = (lax.shift_right_logical(kpos, sel_shift) == blk_lane).astype(BF16)
        causal = kpos <= pos_q
        kk = ks_ref[pl.ds(k0, kc), :]
        s = [_mm(kk, qzr[g]) for g in groups]
        bias = [jnp.where(causal & (jnp.dot(expand, sel_b[g], preferred_element_type=F32) > 0.5), 0.0, NEG)
                for g in groups]
        s = [s[g] + jnp.concatenate([bias[g]] * hg, axis=1) for g in groups]
        m_new = [jnp.maximum(carry[3 * g], jnp.max(s[g], axis=0, keepdims=True)) for g in groups]
        p = [jnp.exp(s[g] - jnp.maximum(m_new[g], 0.5 * NEG)) for g in groups]
        pv = [_mm(vst_ref[c, g * hd:(g + 1) * hd, :], p[g]) for g in groups]
        out = []
        for g in groups:
            m_i, l_i, acc = carry[3 * g:3 * g + 3]
            alpha = jnp.exp(m_i - m_new[g])
            out += [m_new[g], alpha * l_i + jnp.sum(p[g], axis=0, keepdims=True), alpha * acc + pv[g]]
        return tuple(out)

    init = (jnp.full((1, hg * qb), NEG, F32), jnp.zeros((1, hg * qb), F32), zeros) * NSA_KV_HEADS
    fin = lax.fori_loop(0, s0 // kc + 1, sel_step, init)
    o_s = [fin[3 * g + 2] / jnp.where(fin[3 * g + 1] > 0, fin[3 * g + 1], 1.0) for g in groups]
    nwc = WINDOW // kc + 1
    c_lo = jnp.minimum(jnp.maximum(s0 - WINDOW, 0) // kc, nkc - nwc)
    wstart = pl.multiple_of(c_lo * kc, kc)
    wpos = wstart + _iota((nwc * kc, 1), 0)
    mask_w = (wpos <= pos_l) & (wpos > pos_l - WINDOW)
    kwb = kw_ref[pl.ds(wstart, nwc * kc), :]
    p_w = [_masked_softmax(_mm(kwb, qzr[g]), mask_w, axis=0) for g in groups]
    o_w = []
    for g in groups:
        acc = _mm(vwt_ref[c_lo, g * hd:(g + 1) * hd, :], p_w[g][0:kc])
        for i in range(1, nwc):
            acc = acc + _mm(vwt_ref[c_lo + i, g * hd:(g + 1) * hd, :], p_w[g][i * kc:(i + 1) * kc])
        o_w.append(acc)
    rows = []
    for g in groups:
        for h in range(hg):
            hh, lanes = g * hg + h, slice(h * qb, (h + 1) * qb)
            rows.append(gates[3 * hh:3 * hh + 1] * o_c[g][:, lanes] + gates[3 * hh + 1:3 * hh + 2] * o_s[g][:, lanes]
                        + gates[3 * hh + 2:3 * hh + 3] * o_w[g][:, lanes])
    o_ref[...] = jnp.transpose(jnp.concatenate(rows, axis=0)).astype(o_ref.dtype)


def _nsa_prompt(p, kcv, vct, b, t):
    qb, kc = NSA_QB, TOKEN_TILE
    nwc = WINDOW // kc + 1
    assert t % kc == 0 and kc % qb == 0 and t >= nwc * kc and t % SEL_BLOCK == 0 and WINDOW % kc == 0
    nkc = t // kc
    nch = kcv.shape[1]
    ns = t // SEL_BLOCK
    c2st = _cmp_to_sel(nch, ns).T.astype(BF16)
    katt = p["katt"].reshape(b, t, 4 * NSA_KV)
    qsub = kc // qb
    q_spec = lambda w: pl.BlockSpec((None, w, qb), lambda i, j: (i * nkc + j // qsub, 0, j % qsub))
    chunks = lambda a: a.reshape(b, nkc, NSA_KV, kc)
    chunk_spec = pl.BlockSpec((None, nkc, NSA_KV, kc), lambda i, j: (i, 0, 0, 0))
    return pl.pallas_call(
        functools.partial(_nsa_prompt_kernel, qb=qb, kc=kc),
        grid=(b, t // qb),
        in_specs=[q_spec(NSA_Q), q_spec(NSA_Q), q_spec(GATE_PAD),
                  pl.BlockSpec((None, nch, NSA_KV), lambda i, j: (i, 0, 0)),
                  pl.BlockSpec((None, NSA_KV, nch), lambda i, j: (i, 0, 0)),
                  pl.BlockSpec((ns, nch), lambda i, j: (0, 0)),
                  pl.BlockSpec((None, t, NSA_KV), lambda i, j: (i, 0, 0)),
                  pl.BlockSpec((None, t, NSA_KV), lambda i, j: (i, 0, 2)),
                  chunk_spec, chunk_spec],
        out_specs=pl.BlockSpec((None, qb, NSA_Q), lambda i, j: (i, j, 0)),
        out_shape=jax.ShapeDtypeStruct((b, t, NSA_Q), BF16),
        compiler_params=_cparams(("parallel", "arbitrary"), 48),
        name="nsa_prompt",
    )(p["qut"], p["qrt"], p["gt"], kcv, vct, c2st, katt, katt, chunks(p["vst"]), chunks(p["vwt"])).reshape(b * t, NSA_Q)


def _moba_prompt_kernel(qt_ref, km_ref, k_ref, vt_ref, o_ref, *, blk):
    jq = pl.program_id(2)
    nb = km_ref.shape[0]
    n_top = min(MOBA_TOPK, nb)
    hpl = LANES // HEAD_DIM
    n_heads = qt_ref.shape[0] // HEAD_DIM
    jb = _iota((nb, 1), 0)
    jb_f = jb.astype(F32)
    row_head = lax.shift_right_logical(_iota((LANES, 1), 0), HEAD_DIM.bit_length() - 1)
    tile = lambda hh: slice(hh // hpl * LANES, (hh // hpl + 1) * LANES)
    qz, sels = [], []
    for hh in range(n_heads):
        qt = qt_ref[tile(hh), :].astype(F32) * _QK_SCALE
        qz.append(jnp.where(row_head == hh % hpl, qt, 0.0).astype(BF16))
        score = jnp.where(jb < jq, _mm(km_ref[:, tile(hh)], qz[hh]), NEG)
        sel = jnp.zeros((nb, blk), F32)
        for _ in range(n_top):
            m, _, pick = _take_first_max(score, jb_f, float(nb), 0)
            sel = jnp.where(pick & (m > 0.5 * NEG), 1.0, sel)
            score = jnp.where(pick, 3.0 * NEG, score)
        sels.append(sel)

    def attend(j, carry, mask_fn):
        k0 = pl.multiple_of(j * blk, blk)
        heads = range(n_heads)
        s = [_mm(k_ref[pl.ds(k0, blk), tile(hh)], qz[hh]) for hh in heads]
        m_new, m_sub = zip(*[mask_fn(hh, s[hh], carry[3 * hh]) for hh in heads])
        p = [jnp.exp(s[hh] - m_sub[hh]) for hh in heads]
        pv = [_mm(vt_ref[j, hh * HEAD_DIM:(hh + 1) * HEAD_DIM, :], p[hh]) for hh in heads]
        out = []
        for hh in heads:
            m_i, l_i, acc = carry[3 * hh:3 * hh + 3]
            alpha = jnp.exp(m_i - m_new[hh])
            out += [m_new[hh], alpha * l_i + jnp.sum(p[hh], axis=0, keepdims=True), alpha * acc + pv[hh]]
        return tuple(out)

    def past_mask(j):
        def fn(hh, s, m_i):
            picked = jnp.sum(jnp.where(jb == j, sels[hh], 0.0), axis=0, keepdims=True) > 0.5
            m_new = jnp.maximum(m_i, jnp.where(picked, jnp.max(s, axis=0, keepdims=True), NEG))
            return m_new, jnp.where(picked, m_new, BIG)
        return fn

    init = (jnp.full((1, blk), NEG, F32), jnp.zeros((1, blk), F32), jnp.zeros((HEAD_DIM, blk), F32)) * n_heads
    carry = lax.fori_loop(0, jq, lambda j, c: attend(j, c, past_mask(j)), init)
    causal = _iota((blk, 1), 0) <= _iota((1, blk), 1)

    def own_mask(hh, s, m_i):
        m_new = jnp.maximum(m_i, jnp.max(jnp.where(causal, s, NEG), axis=0, keepdims=True))
        return m_new, jnp.where(causal, m_new, BIG)

    final = attend(jq, carry, own_mask)
    outs = [final[3 * hh + 2] / final[3 * hh + 1] for hh in range(n_heads)]
    o_ref[...] = jnp.transpose(jnp.concatenate(outs, axis=0)).astype(o_ref.dtype)


def _moba_prompt(p, b, t):
    blk = MOBA_BLOCK
    assert t % blk == 0
    nb = t // blk
    w = MOBA_HEADS_PER_STEP * HEAD_DIM
    nstep = MOBA_W // w
    qt = p["qbt"]
    vt = p["vbt"].reshape(b, nb, MOBA_W, blk)
    kv = p["kvb"].reshape(b, t, MOBA_ROW)
    km = p["kmean"].reshape(b, nb, MOBA_W)
    return pl.pallas_call(
        functools.partial(_moba_prompt_kernel, blk=blk),
        grid=(b, nstep, nb),
        in_specs=[pl.BlockSpec((None, w, blk), lambda i, hp, j: (i * nb + j, hp, 0)),
                  pl.BlockSpec((None, nb, w), lambda i, hp, j: (i, 0, hp)),
                  pl.BlockSpec((None, t, w), lambda i, hp, j: (i, 0, hp)),
                  pl.BlockSpec((None, nb, w, blk), lambda i, hp, j: (i, 0, hp, 0))],
        out_specs=pl.BlockSpec((None, blk, w), lambda i, hp, j: (i, j, hp)),
        out_shape=jax.ShapeDtypeStruct((b, t, MOBA_W), BF16),
        compiler_params=_cparams(("parallel", "parallel", "arbitrary"), 56),
        name="moba_prompt",
    )(qt, km, kv, vt).reshape(b * t, MOBA_W)


def _group_rows(n_rows=NSA_HEADS):
    return _iota((n_rows, 1), 0) < NSA_GROUP


def _nsa_dec_cmp_kernel(qu_ref, kcv_ref, c2s_ref, oc_ref, idx_ref, *, pos):
    assert NSA_KV_HEADS == 2
    q = qu_ref[...]
    nch = kcv_ref.shape[0]
    ns = c2s_ref.shape[1]
    n_sel = min(SEL_TOPN, -(-(pos + 1) // SEL_BLOCK))
    scale = HEAD_DIM ** -0.5
    g0 = _group_rows()
    hd = HEAD_DIM
    kcv = kcv_ref[...]
    s = jnp.where(g0, _mm_nt(q, kcv[:, 0:hd]), _mm_nt(q, kcv[:, hd:2 * hd])) * scale
    cend = _iota((1, nch), 1) * CMP_STRIDE + (CMP_BLOCK - 1)
    p = _masked_softmax(s, cend <= pos)
    oc_ref[...] = jnp.where(g0, _mm(p, kcv[:, NSA_KV:NSA_KV + hd]), _mm(p, kcv[:, NSA_KV + hd:NSA_KV + 2 * hd]))
    p0 = jnp.sum(jnp.where(g0, p, 0.0), axis=0, keepdims=True)
    p1 = jnp.sum(jnp.where(g0, 0.0, p), axis=0, keepdims=True)
    imp = _mm(jnp.where(g0, p0, p1), c2s_ref[...])
    jsel = _iota((1, ns), 1)
    jsel_f = jsel.astype(F32)
    jq = pos // SEL_BLOCK
    forced = (jsel == 0) | (jsel == jq) | (jsel == jq - 1)
    score = jnp.where(jsel <= jq, jnp.where(forced, BIG, imp), -1.0)
    lane = _iota((1, LANES), 1)
    idx = jnp.full((NSA_HEADS, LANES), -1, I32)
    for it in range(n_sel):
        m, first, pick = _take_first_max(score, jsel_f, float(ns), 1)
        idx = jnp.where(lane == it, jnp.where(m >= 0.0, first, -1.0).astype(I32), idx)
        score = jnp.where(pick, -2.0, score)
    idx_ref[...] = idx


def _attend_with_new(q, kts, vts, valids, k_new, v_new, new_valid, scale):
    s = [_mm(q, kt) * scale for kt in kts]
    qf = q.astype(MXU_DTYPE).astype(F32)
    s_new = jnp.sum(qf * k_new.astype(MXU_DTYPE).astype(F32), axis=1, keepdims=True) * scale
    m = jnp.where(new_valid, s_new, NEG)
    for sj, vj in zip(s, valids):
        m = jnp.maximum(m, jnp.max(jnp.where(vj > 0.5, sj, NEG), axis=1, keepdims=True))
    p = [jnp.where(vj > 0.5, jnp.exp(sj - m), 0.0) for sj, vj in zip(s, valids)]
    p_new = jnp.where(new_valid, jnp.exp(s_new - m), 0.0)
    d = p_new
    for pj in p:
        d = d + jnp.sum(pj, axis=1, keepdims=True)
    d = jnp.where(d > 0, d, 1.0)
    o = (p_new / d).astype(MXU_DTYPE).astype(F32) * v_new.astype(MXU_DTYPE).astype(F32)
    for pj, vt in zip(p, vts):
        o = o + _mm_nt(pj / d, vt)
    return o


def _nsa_dec_att_kernel(idx_ref, pt_ref, qr_ref, gate_ref, oc_ref, new_ref, win_ref, *rest, pos, n_sel, past_blocks):
    del pt_ref
    blk_refs, o_ref = rest[:-1], rest[-1]
    b = pl.program_id(0)
    q = qr_ref[...]
    scale = HEAD_DIM ** -0.5
    hd = HEAD_DIM
    g0 = _group_rows()
    new = new_ref[...]
    n_win = win_ref.shape[-1]
    wpos = pos - n_win + _iota((1, n_win), 1)
    valid_w = jnp.where((wpos > pos - WINDOW) & (wpos >= 0), 1.0, 0.0)
    page = blk_refs[0].shape[-1]
    bpp = page // SEL_BLOCK
    blk_of_lane = lax.shift_right_logical(_iota((1, page), 1), SEL_BLOCK.bit_length() - 1)
    o_s, o_w = [], []
    for g in range(NSA_KV_HEADS):
        glo = g * hd
        kts, vts, valids = [], [], []
        new_valid = jnp.zeros((1, 1), F32)
        for j in range(n_sel):
            bj = idx_ref[b, g, j]
            kts.append(blk_refs[2 * (g * n_sel + j)][...])
            vts.append(blk_refs[2 * (g * n_sel + j) + 1][...])
            in_past = jnp.where((bj >= 0) & (bj < past_blocks), 1.0, 0.0)
            valids.append(jnp.where(blk_of_lane == bj % bpp, in_past, 0.0))
            new_valid = jnp.maximum(new_valid, jnp.where(bj == past_blocks, 1.0, 0.0))
        o_s.append(_attend_with_new(q, kts, vts, valids, new[:, glo:glo + hd],
                                    new[:, NSA_KV + glo:NSA_KV + glo + hd], new_valid > 0.5, scale))
        o_w.append(_attend_with_new(q, [win_ref[0, g]], [win_ref[1, g]], [valid_w],
                                    new[:, 2 * NSA_KV + glo:2 * NSA_KV + glo + hd],
                                    new[:, 3 * NSA_KV + glo:3 * NSA_KV + glo + hd], True, scale))
    gs = _sigmoid(gate_ref[...])
    o_ref[...] = (gs[:, 0:1] * oc_ref[...] + gs[:, 1:2] * jnp.where(g0, o_s[0], o_s[1])
                  + gs[:, 2:3] * jnp.where(g0, o_w[0], o_w[1]))


def _nsa_decode_select(p, kcv, pos):
    db = kcv.shape[0]
    nch = kcv.shape[1]
    ns = pos // SEL_BLOCK + 1
    ns_pad = -(-ns // LANES) * LANES
    n_sel = min(SEL_TOPN, ns)
    c2s = _cmp_to_sel(nch, ns_pad).astype(BF16)
    per_q = lambda w: pl.BlockSpec((None, NSA_HEADS, w), lambda i: (i, 0, 0))
    o_c, idx = pl.pallas_call(
        functools.partial(_nsa_dec_cmp_kernel, pos=pos),
        grid=(db,),
        in_specs=[per_q(HEAD_DIM), pl.BlockSpec((None, nch, 2 * NSA_KV), lambda i: (i, 0, 0)),
                  pl.BlockSpec((nch, ns_pad), lambda i: (0, 0))],
        out_specs=[per_q(HEAD_DIM), per_q(LANES)],
        out_shape=[jax.ShapeDtypeStruct((db, NSA_HEADS, HEAD_DIM), F32),
                   jax.ShapeDtypeStruct((db, NSA_HEADS, LANES), I32)],
        compiler_params=_cparams(("parallel",)),
        name="nsa_dec_cmp",
    )(p["qu"].reshape(db, NSA_HEADS, HEAD_DIM), kcv, c2s)
    return o_c, idx[:, ::NSA_GROUP, :n_sel]


def _nsa_decode(p, kcv, cache_t, win_t, page_table, pos):
    db, n_pages = page_table.shape
    page = cache_t.shape[-1]
    assert pos == n_pages * page and pos % SEL_BLOCK == 0 and page % SEL_BLOCK == 0
    past_blocks = pos // SEL_BLOCK
    o_c, sel_idx = _nsa_decode_select(p, kcv, pos)
    n_sel = sel_idx.shape[2]
    heads3 = lambda a: a.reshape(db, NSA_HEADS, HEAD_DIM)
    bpp = page // SEL_BLOCK

    def blk_map(i, ix, pt, g, j, part):
        bj = jnp.clip(ix[i, g, j], 0, past_blocks - 1)
        return (pt[i, bj // bpp], part, g, 0, 0)

    k_sel, v_sel = 2, 3
    blk_specs = [pl.BlockSpec((None, None, None, HEAD_DIM, page), functools.partial(blk_map, g=g, j=j, part=part))
                 for g in range(NSA_KV_HEADS) for j in range(n_sel) for part in (k_sel, v_sel)]
    per_q2 = lambda w: pl.BlockSpec((None, NSA_HEADS, w), lambda i, ix, pt: (i, 0, 0))
    n_win = win_t.shape[-1]
    gate3 = p["gate"][:, :3 * NSA_HEADS].reshape(db, NSA_HEADS, 3)
    o = pl.pallas_call(
        functools.partial(_nsa_dec_att_kernel, pos=pos, n_sel=n_sel, past_blocks=past_blocks),
        grid_spec=pltpu.PrefetchScalarGridSpec(
            num_scalar_prefetch=2,
            grid=(db,),
            in_specs=[per_q2(HEAD_DIM), per_q2(3), per_q2(HEAD_DIM),
                      pl.BlockSpec((None, 1, 4 * NSA_KV), lambda i, ix, pt: (i, 0, 0)),
                      pl.BlockSpec((None, 2, NSA_KV_HEADS, HEAD_DIM, n_win), lambda i, ix, pt: (i, 0, 0, 0, 0))]
            + blk_specs,
            out_specs=per_q2(HEAD_DIM),
        ),
        out_shape=jax.ShapeDtypeStruct((db, NSA_HEADS, HEAD_DIM), F32),
        compiler_params=_cparams(("arbitrary",)),
        name="nsa_dec_att",
    )(sel_idx, page_table, heads3(p["qr"]), gate3, o_c, p["katt"].reshape(db, 1, 4 * NSA_KV), win_t,
      *([cache_t] * len(blk_specs)))
    return o.reshape(db, NSA_Q)


def _moba_dec_mean_kernel(*refs, n_src, ppb):
    x_refs, o_ref = refs[1:1 + n_src], refs[1 + n_src]
    j = pl.program_id(1)
    nb = o_ref.shape[1]
    page = x_refs[0].shape[-1]
    bps = n_src // ppb

    @pl.when(j == 0)
    def _():
        o_ref[...] = jnp.zeros(o_ref.shape, F32)

    lane = _iota((1, nb), 1)
    acc = o_ref[...]
    for i in range(bps):
        tot = x_refs[i * ppb][...]
        for r in x_refs[i * ppb + 1:(i + 1) * ppb]:
            tot = tot + r[...]
        col = jnp.sum(tot.reshape(MOBA_W, page), axis=1, keepdims=True) * (1.0 / (ppb * page))
        acc = jnp.where(lane == j * bps + i, col, acc)
    o_ref[...] = acc


def _moba_dec_gate_kernel(q_ref, km_ref, idx_ref, *, jq):
    q = q_ref[...]
    nb = km_ref.shape[1]
    n_top = min(MOBA_TOPK, nb)
    head_of_lane = lax.shift_right_logical(_iota((MOBA_HEADS, MOBA_W), 1), HEAD_DIM.bit_length() - 1)
    qbd = jnp.where(head_of_lane == _iota((MOBA_HEADS, MOBA_W), 0),
                    jnp.broadcast_to(q.astype(F32), (MOBA_HEADS, MOBA_W)), 0.0)
    jb = _iota((1, nb), 1)
    jb_f = jb.astype(F32)
    score = jnp.where(jb < jq, _mm(qbd, km_ref[...]), NEG)
    lane = _iota((1, LANES), 1)
    idx = jnp.full((MOBA_HEADS, LANES), -1, I32)
    for it in range(n_top):
        m, first, pick = _take_first_max(score, jb_f, float(nb), 1)
        idx = jnp.where(lane == it, jnp.where(m > 0.5 * NEG, first, -1.0).astype(I32), idx)
        score = jnp.where(pick, 3.0 * NEG, score)
    idx_ref[...] = idx


def _moba_dec_att_kernel(idx_ref, pt_ref, q_ref, kn_ref, vn_ref, *rest, n_top, ppb):
    del pt_ref
    src, o_ref = rest[:-1], rest[-1]
    b, hp = pl.program_id(0), pl.program_id(1)
    hpl = LANES // HEAD_DIM
    scale = HEAD_DIM ** -0.5
    rows = 8
    page = src[0].shape[-1]
    q_all = jnp.broadcast_to(q_ref[...].astype(F32), (rows, LANES))
    k_new = kn_ref[...]
    v_new = vn_ref[...]
    outs = []
    for hh in range(hpl):
        lanes = slice(hh * HEAD_DIM, (hh + 1) * HEAD_DIM)
        kts, vts, valids = [], [], []
        for t in range(n_top):
            ok = jnp.where(idx_ref[b, hp * hpl + hh, t] >= 0, 1.0, 0.0)
            for pg in range(ppb):
                base = 2 * ((hh * n_top + t) * ppb + pg)
                kts.append(src[base][...])
                vts.append(src[base + 1][...])
                valids.append(jnp.full((1, page), 1.0, F32) * ok)
        o = _attend_with_new(q_all[:, lanes], kts, vts, valids, k_new[:, lanes], v_new[:, lanes], True, scale)
        outs.append(o[0:1])
    o_ref[...] = jnp.concatenate(outs, axis=1)


def _moba_decode(p, cache_t, page_table, pos):
    db, n_pages = page_table.shape
    page = cache_t.shape[-1]
    assert MOBA_BLOCK % page == 0 and pos % MOBA_BLOCK == 0 and pos == n_pages * page
    ppb = MOBA_BLOCK // page
    nb = pos // MOBA_BLOCK
    assert nb >= MOBA_TOPK
    pps = PAGES_PER_STEP
    page_specs = [pl.BlockSpec((None, None, MOBA_HEADS, HEAD_DIM, page), functools.partial(
        lambda i, j, pt, pg: (pt[i, j * pps + pg], 0, 0, 0, 0), pg=pg)) for pg in range(pps)]
    kmean_t = pl.pallas_call(
        functools.partial(_moba_dec_mean_kernel, n_src=pps, ppb=ppb),
        grid_spec=pltpu.PrefetchScalarGridSpec(
            num_scalar_prefetch=1,
            grid=(db, n_pages // pps),
            in_specs=page_specs,
            out_specs=pl.BlockSpec((None, MOBA_W, nb), lambda i, j, pt: (i, 0, 0)),
        ),
        out_shape=jax.ShapeDtypeStruct((db, MOBA_W, nb), F32),
        compiler_params=_cparams(("parallel", "arbitrary")),
        name="moba_dec_mean",
    )(page_table, *([cache_t] * pps))
    idx = pl.pallas_call(
        functools.partial(_moba_dec_gate_kernel, jq=nb),
        grid=(db,),
        in_specs=[pl.BlockSpec((None, 1, MOBA_W), lambda i: (i, 0, 0)),
                  pl.BlockSpec((None, MOBA_W, nb), lambda i: (i, 0, 0))],
        out_specs=pl.BlockSpec((None, MOBA_HEADS, LANES), lambda i: (i, 0, 0)),
        out_shape=jax.ShapeDtypeStruct((db, MOBA_HEADS, LANES), I32),
        compiler_params=_cparams(("parallel",)),
        name="moba_dec_gate",
    )(p["qb"].reshape(db, 1, MOBA_W), kmean_t)
    n_top = min(MOBA_TOPK, nb)
    top_idx = idx[:, :, :n_top]
    hpl = LANES // HEAD_DIM
    npair = MOBA_HEADS // hpl

    def src_map(i, hp, ix, pt, hh, t, pg, kv):
        bj = jnp.clip(ix[i, hp * hpl + hh, t], 0, nb - 1)
        return (pt[i, bj * ppb + pg], kv, hp * hpl + hh, 0, 0)

    src_specs = [pl.BlockSpec((None, None, None, HEAD_DIM, page), functools.partial(src_map, hh=hh, t=t, pg=pg, kv=kv))
                 for hh in range(hpl) for t in range(n_top) for pg in range(ppb) for kv in range(2)]
    pair = lambda off: pl.BlockSpec((None, None, 1, LANES), lambda i, hp, ix, pt: (i, off + hp, 0, 0))
    o = pl.pallas_call(
        functools.partial(_moba_dec_att_kernel, n_top=n_top, ppb=ppb),
        grid_spec=pltpu.PrefetchScalarGridSpec(
            num_scalar_prefetch=2,
            grid=(db, npair),
            in_specs=[pair(0), pair(0), pair(npair)] + src_specs,
            out_specs=pair(0),
        ),
        out_shape=jax.ShapeDtypeStruct((db, npair, 1, LANES), F32),
        compiler_params=_cparams(("arbitrary", "arbitrary")),
        name="moba_dec_att",
    )(top_idx, page_table, p["qb"].reshape(db, npair, 1, LANES), p["kvb"].reshape(db, 2 * npair, 1, LANES),
      p["kvb"].reshape(db, 2 * npair, 1, LANES), *([cache_t] * len(src_specs)))
    return o.reshape(db, MOBA_W)


def _route_t(s_t, b_t):
    n_e, n_tok = s_t.shape
    per = n_e // N_GROUPS
    biased = s_t + b_t
    sub_f = _iota((per, 1), 0).astype(F32)
    gscore = []
    for g in range(N_GROUPS):
        x = biased[g * per:(g + 1) * per]
        m1, _, pick = _take_first_max(x, sub_f, float(per), 0)
        gscore.append(m1 + jnp.max(jnp.where(pick, NEG, x), axis=0, keepdims=True))
    gs = jnp.concatenate(gscore, axis=0)
    g_f = _iota((N_GROUPS, 1), 0).astype(F32)
    gmask = jnp.zeros((N_GROUPS, n_tok), F32)
    for _ in range(TOPK_GROUPS):
        _, _, pick = _take_first_max(gs, g_f, float(N_GROUPS), 0)
        gmask = jnp.where(pick, 1.0, gmask)
        gs = jnp.where(pick, NEG, gs)
    masked = jnp.concatenate([jnp.where(gmask[g:g + 1] > 0.5, biased[g * per:(g + 1) * per], NEG)
                              for g in range(N_GROUPS)], axis=0)
    e_f = _iota((n_e, 1), 0).astype(F32)
    ids, ws = [], []
    for _ in range(TOP_K):
        _, first, pick = _take_first_max(masked, e_f, float(n_e), 0)
        ids.append(first)
        ws.append(jnp.sum(jnp.where(pick, s_t, 0.0), axis=0, keepdims=True))
        masked = jnp.where(pick, 3.0 * NEG, masked)
    w = jnp.concatenate(ws, axis=0)
    w = w / jnp.sum(w, axis=0, keepdims=True) * ROUTED_SCALE
    return jnp.concatenate(ids, axis=0).astype(I32), w


def _merge_kernel(x_ref, oa_ref, ob_ref, sc1_ref, sh1_ref, g1_ref, sc2_ref, sh2_ref, wmg_ref, wa_ref, wb_ref,
                  wo_ref, lg_ref, lb_ref, wr_ref, br_ref, x1_ref, h_ref, ti_ref, tw_ref, *, alpha):
    x = x_ref[...]
    d = x.shape[1]
    u = x * (1.0 + sc1_ref[...]) + sh1_ref[...]
    mg = _mm(u, wmg_ref[...])
    y_a = _mm(oa_ref[...], wa_ref[...])
    y_b = _mm(ob_ref[...], wb_ref[...])
    mix = _mm(_sigmoid(mg[:, :d]) * y_a + _sigmoid(mg[:, d:]) * y_b, wo_ref[...])
    x1 = _layer_norm(alpha * x + g1_ref[...] * mix, lg_ref[...], lb_ref[...])
    x1_ref[...] = x1
    h = x1 * (1.0 + sc2_ref[...]) + sh2_ref[...]
    h_ref[...] = h
    s_t = _sigmoid(_mm_nt(wr_ref[...], h))
    ti_ref[...], tw_ref[...] = _route_t(s_t, br_ref[...])


def _merge(x, o_a, o_b, mods, w, *, tm, per_token_mod, rows_per_batch, alpha):
    n, d = x.shape
    nt = n // tm
    tiles_per_batch = rows_per_batch // tm
    row = lambda i: (i, 0)
    if per_token_mod:
        mod_spec = pl.BlockSpec((tm, d), row)
    else:
        mod_spec = pl.BlockSpec((None, 1, d), lambda i: (i // tiles_per_batch, 0, 0))
    full = lambda a: pl.BlockSpec(a.shape, lambda i: (0,) * a.ndim)
    ws = [w["w_mg"], w["w_nsa_out"], w["w_moba_out"], w["w_o"], w["ln1_g"], w["ln1_b"], w["w_router_t"], w["b_router"]]
    return pl.pallas_call(
        functools.partial(_merge_kernel, alpha=alpha),
        grid=(nt,),
        in_specs=[pl.BlockSpec((tm, d), row), pl.BlockSpec((tm, NSA_Q), row), pl.BlockSpec((tm, MOBA_W), row)]
        + [mod_spec] * 5 + [full(a) for a in ws],
        out_specs=[pl.BlockSpec((tm, d), row), pl.BlockSpec((tm, d), row),
                   pl.BlockSpec((TOP_K, tm), lambda i: (0, i)), pl.BlockSpec((TOP_K, tm), lambda i: (0, i))],
        out_shape=[jax.ShapeDtypeStruct((n, d), F32), jax.ShapeDtypeStruct((n, d), F32),
                   jax.ShapeDtypeStruct((TOP_K, n), I32), jax.ShapeDtypeStruct((TOP_K, n), F32)],
        compiler_params=_cparams(("parallel",), 48),
        name="merge",
    )(x, o_a, o_b, *mods, *ws)


def _rank_kernel(ti_ref, rank_ref, cnt_ref, base_ref):
    @pl.when(pl.program_id(0) == 0)
    def _():
        base_ref[...] = jnp.zeros(base_ref.shape, F32)

    ti = ti_ref[...]
    k, tr = ti.shape
    e_iota = _iota((N_EXPERTS, 1), 0)
    hit = lambda j: e_iota == ti[j:j + 1, :]
    onehot = jnp.zeros((N_EXPERTS, tr), F32)
    for j in range(k):
        onehot = onehot + jnp.where(hit(j), 1.0, 0.0)
    tri = (_iota((tr, 1), 0) <= _iota((1, tr), 1)).astype(BF16)
    before = jnp.dot(onehot.astype(BF16), tri, preferred_element_type=F32) - onehot + base_ref[...]
    rank_ref[...] = jnp.concatenate([jnp.sum(jnp.where(hit(j), before, 0.0), axis=0, keepdims=True)
                                     for j in range(k)], axis=0).astype(I32)
    total = base_ref[...] + jnp.sum(onehot, axis=1, keepdims=True)
    base_ref[...] = total
    cnt_ref[...] = jnp.broadcast_to(total, cnt_ref.shape).astype(I32)


def _slot_kernel(ti_ref, rank_ref, start_ref, slot_ref):
    ti = ti_ref[...]
    e_iota = _iota((N_EXPERTS, 1), 0)
    start = start_ref[...]
    rows = [jnp.sum(jnp.where(e_iota == ti[j:j + 1, :], start, 0.0), axis=0, keepdims=True) for j in range(ti.shape[0])]
    slot_ref[...] = rank_ref[...] + jnp.concatenate(rows, axis=0).astype(I32)


def _row_copies(n_tok, n_choice, make_copy):
    def start(t, c):
        for j in range(n_choice):
            make_copy(t, j).start()
        return c

    def wait(t, c):
        for j in range(n_choice):
            make_copy(t, j).wait()
        return c

    return (lambda: lax.fori_loop(0, n_tok, start, 0)), (lambda: lax.fori_loop(0, n_tok, wait, 0))


def _scatter_rows_kernel(slot_ref, h_ref, xs_ref, sem):
    k, td = slot_ref.shape
    copy = lambda t, j: pltpu.make_async_copy(h_ref.at[pl.ds(t, 1), :], xs_ref.at[pl.ds(slot_ref[j, t], 1), :], sem)
    start, wait = _row_copies(td, k, copy)
    start()
    wait()


def _expert_kernel(be_ref, nu_ref, x_ref, wg_ref, wu_ref, wd_ref, y_ref):
    del be_ref

    @pl.when(pl.program_id(0) < nu_ref[0])
    def _():
        x = x_ref[...]
        y_ref[...] = _mm(_silu(_mm(x, wg_ref[...])) * _mm(x, wu_ref[...]), wd_ref[...])


def _token_tile(n, pref):
    return pref if n % pref == 0 else n


def _routed_experts(h, top_i, w):
    n, d = h.shape
    k = top_i.shape[0]
    blk = min(EXPERT_BLK, max(8, (k * n) // N_EXPERTS))
    nblk = -(-(k * n) // blk) + N_EXPERTS
    tr = _token_tile(n, RANK_TILE)
    tiles = lambda t: pl.BlockSpec((k, t), lambda i: (0, i))
    rank, counts = pl.pallas_call(
        _rank_kernel,
        grid=(n // tr,),
        in_specs=[tiles(tr)],
        out_specs=[tiles(tr), pl.BlockSpec((N_EXPERTS, LANES), lambda i: (0, 0))],
        out_shape=[jax.ShapeDtypeStruct((k, n), I32), jax.ShapeDtypeStruct((N_EXPERTS, LANES), I32)],
        scratch_shapes=[pltpu.VMEM((N_EXPERTS, 1), F32)],
        compiler_params=_cparams(("arbitrary",)),
        name="moe_rank",
    )(top_i)
    counts = counts[:, 0]
    padded = (counts + blk - 1) // blk * blk
    end_pad = jnp.cumsum(padded)
    start_pad = end_pad - padded
    blk_e = jnp.minimum(jnp.sum(end_pad[None, :] <= (jnp.arange(nblk, dtype=I32) * blk)[:, None], axis=1),
                        N_EXPERTS - 1).astype(I32)
    n_used = (end_pad[-1] // blk).astype(I32).reshape(1)
    slot = pl.pallas_call(
        _slot_kernel,
        grid=(n // tr,),
        in_specs=[tiles(tr), tiles(tr), pl.BlockSpec((N_EXPERTS, 1), lambda i: (0, 0))],
        out_specs=tiles(tr),
        out_shape=jax.ShapeDtypeStruct((k, n), I32),
        compiler_params=_cparams(("parallel",)),
        name="moe_slot",
    )(top_i, rank, start_pad.astype(F32).reshape(N_EXPERTS, 1))
    td = _token_tile(n, ROW_COPY_TILE)
    x_sorted = pl.pallas_call(
        _scatter_rows_kernel,
        grid=(n // td,),
        in_specs=[pl.BlockSpec((k, td), lambda i: (0, i), memory_space=pltpu.SMEM),
                  pl.BlockSpec((td, d), lambda i: (i, 0))],
        out_specs=pl.BlockSpec(memory_space=pl.ANY),
        out_shape=jax.ShapeDtypeStruct((nblk * blk, d), F32),
        scratch_shapes=[pltpu.SemaphoreType.DMA],
        compiler_params=_cparams(("arbitrary",)),
        name="moe_scatter",
    )(slot, h)
    de = w["w_exp_gate"].shape[2]
    y = pl.pallas_call(
        _expert_kernel,
        grid_spec=pltpu.PrefetchScalarGridSpec(
            num_scalar_prefetch=2,
            grid=(nblk,),
            in_specs=[pl.BlockSpec((blk, d), lambda i, be, nu: (i, 0)),
                      pl.BlockSpec((None, d, de), lambda i, be, nu: (be[i], 0, 0)),
                      pl.BlockSpec((None, d, de), lambda i, be, nu: (be[i], 0, 0)),
                      pl.BlockSpec((None, de, d), lambda i, be, nu: (be[i], 0, 0))],
            out_specs=pl.BlockSpec((blk, d), lambda i, be, nu: (i, 0)),
        ),
        out_shape=jax.ShapeDtypeStruct((nblk * blk, d), F32),
        compiler_params=_cparams(("arbitrary",), 48),
        name="experts",
    )(blk_e, n_used, x_sorted, w["w_exp_gate"], w["w_exp_up"], w["w_exp_down"])
    return y, slot


def _final_kernel(slot_ref, x1_ref, h_ref, tw_ref, g2_ref, wg_ref, wu_ref, wd_ref, lg_ref, lb_ref, y_ref, o_ref,
                  buf_ref, sem, *, alpha):
    k, tc = slot_ref.shape
    copy = lambda t, j: pltpu.make_async_copy(y_ref.at[pl.ds(slot_ref[j, t], 1), :],
                                              buf_ref.at[j, pl.ds(t, 1), :], sem)
    start, wait = _row_copies(tc, k, copy)
    start()
    h = h_ref[...]
    shared = _mm(_silu(_mm(h, wg_ref[...])) * _mm(h, wu_ref[...]), wd_ref[...])
    wait()
    side = max(tc, LANES)
    tw = tw_ref[...]
    if tc < side:
        tw = jnp.concatenate([tw, jnp.zeros((k, side - tc), F32)], axis=1)
    tw_t = jnp.transpose(jnp.concatenate([tw, jnp.zeros((side - k, side), F32)], axis=0))[0:tc]
    routed = tw_t[:, 0:1] * buf_ref[0]
    for j in range(1, k):
        routed = routed + tw_t[:, j:j + 1] * buf_ref[j]
    o_ref[...] = _layer_norm(alpha * x1_ref[...] + g2_ref[...] * (routed + shared), lg_ref[...], lb_ref[...])


def _final(x1, h, y_sorted, slot, top_w, g2, w, *, per_token_mod, rows_per_batch, alpha):
    n, d = x1.shape
    k = slot.shape[0]
    tc = _token_tile(n, ROW_COPY_TILE)
    tiles_per_batch = rows_per_batch // tc
    tile = pl.BlockSpec((tc, d), lambda i: (i, 0))
    mod_spec = tile if per_token_mod else pl.BlockSpec((None, 1, d), lambda i: (i // tiles_per_batch, 0, 0))
    full = lambda a: pl.BlockSpec(a.shape, lambda i: (0,) * a.ndim)
    ws = [w["w_sh_gate"], w["w_sh_up"], w["w_sh_down"], w["ln2_g"], w["ln2_b"]]
    return pl.pallas_call(
        functools.partial(_final_kernel, alpha=alpha),
        grid=(n // tc,),
        in_specs=[pl.BlockSpec((k, tc), lambda i: (0, i), memory_space=pltpu.SMEM), tile, tile,
                  pl.BlockSpec((k, tc), lambda i: (0, i)), mod_spec] + [full(a) for a in ws]
        + [pl.BlockSpec(memory_space=pl.ANY)],
        out_specs=tile,
        out_shape=jax.ShapeDtypeStruct((n, d), F32),
        scratch_shapes=[pltpu.VMEM((k, tc, d), F32), pltpu.SemaphoreType.DMA],
        compiler_params=_cparams(("arbitrary",), 40),
        name="final",
    )(slot, x1, h, top_w, g2, *ws, y_sorted)


def _prep_weights(lp):
    (w_ada, b_ada, w_in, cmp_k_w1, cmp_k_w2, cmp_k_pe, cmp_v_w1, cmp_v_w2, cmp_v_pe, w_nsa_out, w_moba_out, w_o,
     ln1_g, ln1_b, w_router, b_router, w_exp_gate, w_exp_up, w_exp_down, w_sh_gate, w_sh_up, w_sh_down,
     ln2_g, ln2_b) = lp
    c = lambda a: a.astype(MXU_DTYPE)
    row = lambda a: a.reshape(1, -1)
    w_r, w_mg = _reorder_w_in(w_in)
    wk, w2k, pek = _cmp_weights(cmp_k_w1, cmp_k_w2, cmp_k_pe)
    wv, w2v, pev = _cmp_weights(cmp_v_w1, cmp_v_w2, cmp_v_pe)
    return dict(w_ada=w_ada, b_ada=b_ada, w_r=w_r, w_mg=w_mg, cmp=(wk, w2k, pek, wv, w2v, pev),
                w_nsa_out=c(w_nsa_out), w_moba_out=c(w_moba_out), w_o=c(w_o), ln1_g=row(ln1_g), ln1_b=row(ln1_b),
                w_router_t=c(w_router.T), b_router=b_router.reshape(-1, 1),
                w_exp_gate=c(w_exp_gate), w_exp_up=c(w_exp_up), w_exp_down=c(w_exp_down),
                w_sh_gate=c(w_sh_gate), w_sh_up=c(w_sh_up), w_sh_down=c(w_sh_down), ln2_g=row(ln2_g), ln2_b=row(ln2_b))


def _token_tail(x, o_a, o_b, mods, w, *, tm, per_token_mod, rows_per_batch, alpha):
    sc1, sh1, g1, sh2, sc2, g2 = mods
    kw = dict(tm=tm, per_token_mod=per_token_mod, rows_per_batch=rows_per_batch, alpha=alpha)
    x1, h, top_i, top_w = _merge(x, o_a, o_b, (sc1, sh1, g1, sc2, sh2), w, **kw)
    y_sorted, slot = _routed_experts(h, top_i, w)
    return _final(x1, h, y_sorted, slot, top_w, g2, w, per_token_mod=per_token_mod, rows_per_batch=rows_per_batch,
                  alpha=alpha)


def _layer(xp, xs, c_all, cache_nsa_l, cache_moba_l, win_state_l, page_table, w, alpha):
    b, t, d = xp.shape
    db, ts, _ = xs.shape
    assert ts == 1
    page = cache_nsa_l.shape[1]
    pos = page_table.shape[1] * page
    assert win_state_l.shape[1] == WINDOW and t >= WINDOW
    mod = _ada(c_all, w["w_ada"], w["b_ada"])
    pieces = [mod[:, i * d:(i + 1) * d] for i in range(6)]
    mods_p = [m[:b].reshape(b, 1, d) for m in pieces]
    mods_s = [m[b:b + db] for m in pieces]
    order = lambda m: (m[1], m[0], m[2], m[3], m[4], m[5])
    mods_p, mods_s = order(mods_p), order(mods_s)

    tm = TOKEN_TILE
    xp2 = xp.reshape(b * t, d)
    pp = _inproj(xp2, mods_p[0], mods_p[1], w["w_r"], _rope_tables(jnp.arange(t)), tm=tm, per_token_mod=False,
                 rows_per_batch=t, with_kmean=True)
    kcv_p, vct_p = _cmp_prompt(pp["nsa"].reshape(b, t, NSA_ROW), w["cmp"])
    oa_p = _nsa_prompt(pp, kcv_p, vct_p, b, t)
    ob_p = _moba_prompt(pp, b, t)
    yp = _token_tail(xp2, oa_p, ob_p, mods_p, w, tm=tm, per_token_mod=False, rows_per_batch=t, alpha=alpha)

    xs2 = xs.reshape(db, d)
    ps = _inproj(xs2, mods_s[0], mods_s[1], w["w_r"], _rope_tables(jnp.full((db,), pos)), tm=db, per_token_mod=True,
                 rows_per_batch=db, with_kmean=False)
    nsa_pages = cache_nsa_l.reshape(cache_nsa_l.shape[0], page, NSA_ROW)
    kcv_s, _ = _cmp_decode(nsa_pages, page_table, w["cmp"])
    token_minor = lambda a: jnp.transpose(a, (0, 2, 3, 4, 1))
    oa_s = _nsa_decode(ps, kcv_s, token_minor(cache_nsa_l), token_minor(win_state_l), page_table, pos)
    ob_s = _moba_decode(ps, token_minor(cache_moba_l), page_table, pos)
    ys = _token_tail(xs2, oa_s, ob_s, mods_s, w, tm=db, per_token_mod=True, rows_per_batch=db, alpha=alpha)

    g, hd = NSA_KV_HEADS, HEAD_DIM
    win_p = pp["win"].reshape(b, t, 2, g, hd)[:, t - WINDOW:]
    win_s = jnp.concatenate([win_state_l[:, 1:], ps["win"].reshape(db, 1, 2, g, hd)], axis=1)
    return (yp.reshape(b, t, d), ys.reshape(db, 1, d),
            pp["nsa"].reshape(b, t, 4, g, hd), ps["nsa"].reshape(db, 1, 4, g, hd),
            pp["moba"].reshape(b, t, 2, MOBA_HEADS, hd), ps["moba"].reshape(db, 1, 2, MOBA_HEADS, hd), win_p, win_s)


def kernel(x_prompt, x_sample, cache_nsa, cache_moba, state_nsa_win, page_table, c_prompt, c_sample, w_ada, b_ada,
           w_in, cmp_k_w1, cmp_k_w2, cmp_k_pe, cmp_v_w1, cmp_v_w2, cmp_v_pe, w_nsa_out, w_moba_out, w_o, ln1_g,
           ln1_b, w_router, b_router, w_exp_gate, w_exp_up, w_exp_down, w_sh_gate, w_sh_up, w_sh_down, ln2_g,
           ln2_b):
    params = (w_ada, b_ada, w_in, cmp_k_w1, cmp_k_w2, cmp_k_pe, cmp_v_w1, cmp_v_w2, cmp_v_pe, w_nsa_out, w_moba_out,
              w_o, ln1_g, ln1_b, w_router, b_router, w_exp_gate, w_exp_up, w_exp_down, w_sh_gate, w_sh_up,
              w_sh_down, ln2_g, ln2_b)
    depth = w_ada.shape[0]
    alpha = (2 * depth) ** 0.25
    b, db = x_prompt.shape[0], x_sample.shape[0]
    rows = -(-(b + db) // 8) * 8
    c_all = jnp.pad(jnp.concatenate([c_prompt, c_sample], axis=0), ((0, rows - b - db), (0, 0)))
    xp, xs = x_prompt, x_sample
    outs = [[] for _ in range(6)]
    for l in range(depth):
        w = _prep_weights([p[l] for p in params])
        res = _layer(xp, xs, c_all, cache_nsa[l], cache_moba[l], state_nsa_win[l], page_table, w, alpha)
        xp, xs = res[0], res[1]
        for acc, r in zip(outs, res[2:]):
            acc.append(r)
    return (xp, xs) + tuple(jnp.stack(o) for o in outs)
```

```python
import functools

import jax
import jax.numpy as jnp
from jax import lax
from jax.experimental import pallas as pl
from jax.experimental.pallas import tpu as pltpu

F32 = jnp.float32
BF16 = jnp.bfloat16
I32 = jnp.int32
MXU_DTYPE = jnp.bfloat16

HEAD_DIM = 64
ROPE_DIMS = HEAD_DIM // 4
ROPE_HALF = ROPE_DIMS // 2
ROPE_THETA = 500000.0
NSA_HEADS = 8
NSA_KV_HEADS = 2
NSA_GROUP = NSA_HEADS // NSA_KV_HEADS
CMP_BLOCK = 32
CMP_STRIDE = 16
CMP_RATIO = CMP_BLOCK // CMP_STRIDE
SEL_BLOCK = 64
SEL_TOPN = 16
WINDOW = 512
MOBA_HEADS = 8
MOBA_BLOCK = 256
MOBA_TOPK = 3
N_EXPERTS = 64
TOP_K = 8
N_GROUPS = 8
TOPK_GROUPS = 4
ROUTED_SCALE = 2.5
LN_EPS = 1e-5

LANES = 128
NSA_Q = NSA_HEADS * HEAD_DIM
NSA_KV = NSA_KV_HEADS * HEAD_DIM
MOBA_W = MOBA_HEADS * HEAD_DIM
NSA_ROW = 4 * NSA_KV
MOBA_ROW = 2 * MOBA_W
GATE_PAD = LANES

_QK_SCALE = HEAD_DIM ** -0.5
assert _QK_SCALE == 0.125
NEG = -1e30
BIG = 3e38
TOKEN_TILE = 256
NSA_QB = 256
EXPERT_BLK = 512
RANK_TILE = 512
ROW_COPY_TILE = 128
MOBA_HEADS_PER_STEP = 8
PAGES_PER_STEP = 16


def _sigmoid(x):
    return 1.0 / (1.0 + jnp.exp(-x))


def _silu(x):
    return x * _sigmoid(x)


def _mm(a, b):
    return jnp.dot(a.astype(MXU_DTYPE), b.astype(MXU_DTYPE), preferred_element_type=F32)


def _mm_nt(a, b):
    return lax.dot_general(a.astype(MXU_DTYPE), b.astype(MXU_DTYPE), (((1,), (1,)), ((), ())),
                           preferred_element_type=F32)


def _iota(shape, axis):
    return lax.broadcasted_iota(I32, shape, axis)


def _cparams(sem, vmem_mb=None):
    kw = dict(dimension_semantics=sem)
    if vmem_mb is not None:
        kw["vmem_limit_bytes"] = vmem_mb << 20
    return pltpu.CompilerParams(**kw)


def _masked_softmax(s, mask, axis=-1):
    m = jnp.max(jnp.where(mask, s, NEG), axis=axis, keepdims=True)
    e = jnp.where(mask, jnp.exp(s - m), 0.0)
    d = jnp.sum(e, axis=axis, keepdims=True)
    return e / jnp.where(d > 0, d, 1.0)


def _layer_norm(z, g, b):
    mu = jnp.mean(z, axis=-1, keepdims=True)
    zc = z - mu
    var = jnp.mean(zc * zc, axis=-1, keepdims=True)
    return zc * lax.rsqrt(var + LN_EPS) * g + b


def _take_first_max(score, idx_f, n_f, axis):
    m = jnp.max(score, axis=axis, keepdims=True)
    first = jnp.min(jnp.where(score == m, idx_f, n_f), axis=axis, keepdims=True)
    return m, first, idx_f == first


def _ada_kernel(c_ref, w_ref, b_ref, o_ref):
    a = _silu(c_ref[...])
    o_ref[...] = _mm(a, w_ref[...]) + b_ref[...]


def _ada(c_all, w_ada, b_ada):
    r, d = c_all.shape
    e6 = w_ada.shape[1]
    tn = 1024
    return pl.pallas_call(
        _ada_kernel,
        grid=(e6 // tn,),
        in_specs=[pl.BlockSpec((r, d), lambda j: (0, 0)),
                  pl.BlockSpec((d, tn), lambda j: (0, j)),
                  pl.BlockSpec((1, tn), lambda j: (0, j))],
        out_specs=pl.BlockSpec((r, tn), lambda j: (0, j)),
        out_shape=jax.ShapeDtypeStruct((r, e6), F32),
        compiler_params=_cparams(("arbitrary",)),
        name="ada",
    )(c_all, w_ada, b_ada.reshape(1, e6))


_C_QA = 0
_C_KVA = _C_QA + NSA_Q
_C_QB = _C_KVA + 6 * NSA_KV
_C_KB = _C_QB + MOBA_W
_C_VB = _C_KB + MOBA_W
_C_GATE = _C_VB + MOBA_W
_C_END = _C_GATE + GATE_PAD


def _rope(x, cs, s1, s2):
    parts = []
    for j in range(x.shape[1] // LANES):
        xj = x[:, j * LANES:(j + 1) * LANES]
        parts.append(xj * cs + pltpu.roll(xj, ROPE_HALF, 1) * s1 + pltpu.roll(xj, LANES - ROPE_HALF, 1) * s2)
    return parts[0] if len(parts) == 1 else jnp.concatenate(parts, axis=1)


def _inproj_kernel(x_ref, sc_ref, sh_ref, w_ref, cs_ref, s1_ref, s2_ref,
                   qu_ref, qr_ref, qb_ref, nsa_ref, katt_ref, win_ref, moba_ref, kvb_ref, g_ref, *km_ref):
    u = x_ref[...] * (1.0 + sc_ref[...]) + sh_ref[...]
    r = _mm(u, w_ref[...])
    cs, s1, s2 = cs_ref[...], s1_ref[...], s2_ref[...]
    rope = lambda v: _rope(v, cs, s1, s2)
    qa = r[:, _C_QA:_C_KVA]
    qa_rot = rope(qa)
    qu_ref[...] = qa.astype(BF16)
    qr_ref[...] = qa_rot.astype(BF16)
    o = _C_KVA
    kc_vc = r[:, o:o + 2 * NSA_KV]
    ks = rope(r[:, o + 2 * NSA_KV:o + 3 * NSA_KV])
    vs = r[:, o + 3 * NSA_KV:o + 4 * NSA_KV]
    kw = rope(r[:, o + 4 * NSA_KV:o + 5 * NSA_KV])
    vw = r[:, o + 5 * NSA_KV:o + 6 * NSA_KV]
    nsa_ref[...] = jnp.concatenate([kc_vc, ks, vs], axis=1)
    win_ref[...] = jnp.concatenate([kw, vw], axis=1)
    katt_ref[...] = jnp.concatenate([ks, vs, kw, vw], axis=1).astype(BF16)
    qb = rope(r[:, _C_QB:_C_KB])
    qb_ref[...] = qb.astype(BF16)
    kb = rope(r[:, _C_KB:_C_VB])
    vb = r[:, _C_VB:_C_GATE]
    mb = jnp.concatenate([kb, vb], axis=1)
    moba_ref[...] = mb
    kvb_ref[...] = mb.astype(BF16)
    gate = r[:, _C_GATE:_C_END]
    g_ref[...] = gate
    if km_ref:
        kmean_ref, qbt_ref, vbt_ref, qut_ref, qrt_ref, gt_ref, vst_ref, vwt_ref = km_ref
        kmean_ref[...] = jnp.sum(kb, axis=0, keepdims=True) * (1.0 / kb.shape[0])
        tr = lambda v: jnp.transpose(v).astype(BF16)
        qbt_ref[...] = tr(qb)
        vbt_ref[...] = tr(vb)
        qut_ref[...] = tr(qa)
        qrt_ref[...] = tr(qa_rot)
        gt_ref[...] = jnp.transpose(gate)
        vst_ref[...] = tr(vs)
        vwt_ref[...] = tr(vw)


def _inproj(x, sc, sh, w_r, rope_tabs, *, tm, per_token_mod, rows_per_batch, with_kmean):
    n, d = x.shape
    nt = n // tm
    tiles_per_batch = rows_per_batch // tm
    row = lambda i: (i, 0)
    if per_token_mod:
        mod_spec = pl.BlockSpec((tm, d), row)
    else:
        mod_spec = pl.BlockSpec((None, 1, d), lambda i: (i // tiles_per_batch, 0, 0))
    tab_spec = pl.BlockSpec((tm, LANES), lambda i: (i % tiles_per_batch, 0))
    outs = [("qu", NSA_Q, BF16), ("qr", NSA_Q, BF16), ("qb", MOBA_W, BF16), ("nsa", NSA_ROW, F32),
            ("katt", 4 * NSA_KV, BF16), ("win", 2 * NSA_KV, F32), ("moba", MOBA_ROW, F32),
            ("kvb", MOBA_ROW, BF16), ("gate", GATE_PAD, F32)]
    out_shape = [jax.ShapeDtypeStruct((n, w), dt) for _, w, dt in outs]
    out_specs = [pl.BlockSpec((tm, w), row) for _, w, _ in outs]
    extra = []
    if with_kmean:
        assert tm == MOBA_BLOCK
        extra = ["kmean"]
        out_shape.append(jax.ShapeDtypeStruct((nt, 1, MOBA_W), F32))
        out_specs.append(pl.BlockSpec((None, 1, MOBA_W), lambda i: (i, 0, 0)))
        for name, w, dt in [("qbt", MOBA_W, BF16), ("vbt", MOBA_W, BF16), ("qut", NSA_Q, BF16), ("qrt", NSA_Q, BF16),
                            ("gt", GATE_PAD, F32), ("vst", NSA_KV, BF16), ("vwt", NSA_KV, BF16)]:
            extra.append(name)
            out_shape.append(jax.ShapeDtypeStruct((nt, w, tm), dt))
            out_specs.append(pl.BlockSpec((None, w, tm), lambda i: (i, 0, 0)))
    res = pl.pallas_call(
        _inproj_kernel,
        grid=(nt,),
        in_specs=[pl.BlockSpec((tm, d), row), mod_spec, mod_spec,
                  pl.BlockSpec(w_r.shape, lambda i: (0, 0)), tab_spec, tab_spec, tab_spec],
        out_specs=out_specs,
        out_shape=out_shape,
        compiler_params=_cparams(("parallel",), 48),
        name="inproj",
    )(x, sc, sh, w_r, *rope_tabs)
    return dict(zip([o[0] for o in outs] + extra, res))


def _rope_tables(pos):
    inv = ROPE_THETA ** (-jnp.arange(ROPE_HALF, dtype=F32) / ROPE_HALF)
    ang = pos.astype(F32)[:, None] * inv
    cos, sin = jnp.cos(ang), jnp.sin(ang)
    rows = pos.shape[0]
    one = jnp.ones((rows, HEAD_DIM - ROPE_DIMS), F32)
    zero = jnp.zeros((rows, HEAD_DIM - ROPE_DIMS), F32)
    zh = jnp.zeros((rows, ROPE_HALF), F32)
    cs = jnp.concatenate([cos, cos, one], axis=1)
    s1 = jnp.concatenate([zh, sin, zero], axis=1)
    s2 = jnp.concatenate([-sin, zh, zero], axis=1)
    rep = LANES // HEAD_DIM
    return tuple(jnp.tile(t, (1, rep)) for t in (cs, s1, s2))


def _reorder_w_in(w_in):
    d = w_in.shape[0]
    sizes = (NSA_Q, 6 * NSA_KV, 3 * NSA_HEADS, MOBA_W, MOBA_W, MOBA_W, d, d)
    offs = [0]
    for s in sizes:
        offs.append(offs[-1] + s)
    q_a, kv_a, gate, q_b, k_b, v_b, mg_a, mg_b = [w_in[:, offs[i]:offs[i + 1]] for i in range(8)]
    gate = jnp.pad(gate, ((0, 0), (0, GATE_PAD - 3 * NSA_HEADS)))
    w_r = jnp.concatenate([q_a, kv_a, q_b, k_b, v_b, gate], axis=1).astype(MXU_DTYPE)
    w_mg = jnp.concatenate([mg_a, mg_b], axis=1).astype(MXU_DTYPE)
    return w_r, w_mg


_CHUNK_W = CMP_STRIDE * NSA_ROW


def _cmp1_kernel(*refs, n_src, n_prefetch=0):
    refs = refs[n_prefetch:]
    x_refs = refs[:n_src]
    wk_ref, wv_ref, pek_ref, pev_ref, o_ref = refs[n_src:]

    def gather(off):
        cols = []
        for s in range(CMP_STRIDE):
            lo = s * NSA_ROW + off
            pieces = [xr[:, lo:lo + NSA_KV] for xr in x_refs]
            cols.append(pieces[0] if n_src == 1 else jnp.concatenate(pieces, axis=0))
        return jnp.concatenate(cols, axis=1)

    xk = gather(0)
    xv = gather(NSA_KV)
    outs = []
    for r in range(CMP_RATIO):
        outs.append(_mm(xk + pek_ref[r], wk_ref[r]))
        outs.append(_mm(xv + pev_ref[r], wv_ref[r]))
    o_ref[...] = jnp.concatenate(outs, axis=1)


def _cmp2_kernel(a_ref, w2k_ref, w2v_ref, o_ref, vt_ref):
    a = a_ref[...]
    n = a.shape[0]
    w = NSA_KV
    hk = a[:, 0:w] + pltpu.roll(a[:, 2 * w:3 * w], n - 1, 0)
    hv = a[:, w:2 * w] + pltpu.roll(a[:, 3 * w:4 * w], n - 1, 0)
    vc = _mm(_silu(hv), w2v_ref[...])
    o_ref[...] = jnp.concatenate([_mm(_silu(hk), w2k_ref[...]), vc], axis=1)
    vt_ref[...] = jnp.transpose(vc)


def _blockdiag(w, reps):
    k, n = w.shape
    out = jnp.zeros((reps * k, reps * n), w.dtype)
    for g in range(reps):
        out = out.at[g * k:(g + 1) * k, g * n:(g + 1) * n].set(w)
    return out


def _cmp_weights(w1, w2, pe):
    g = NSA_KV_HEADS
    w1s = jnp.stack([jnp.concatenate([_blockdiag(w1[r, s], g) for s in range(CMP_STRIDE)], axis=0)
                     for r in range(CMP_RATIO)]).astype(MXU_DTYPE)
    pes = jnp.tile(pe[:, :, None, :], (1, 1, g, 1)).reshape(CMP_RATIO, 1, CMP_STRIDE * NSA_KV)
    return w1s, _blockdiag(w2, g).astype(MXU_DTYPE), pes


def _cmp_stage2(a, w2k, w2v):
    b, nch, _ = a.shape
    return pl.pallas_call(
        _cmp2_kernel,
        grid=(b,),
        in_specs=[pl.BlockSpec((None, nch, 4 * NSA_KV), lambda i: (i, 0, 0)),
                  pl.BlockSpec(w2k.shape, lambda i: (0, 0)), pl.BlockSpec(w2v.shape, lambda i: (0, 0))],
        out_specs=[pl.BlockSpec((None, nch, 2 * NSA_KV), lambda i: (i, 0, 0)),
                   pl.BlockSpec((None, NSA_KV, nch), lambda i: (i, 0, 0))],
        out_shape=[jax.ShapeDtypeStruct((b, nch, 2 * NSA_KV), F32), jax.ShapeDtypeStruct((b, NSA_KV, nch), F32)],
        compiler_params=_cparams(("parallel",)),
        name="cmp2",
    )(a, w2k, w2v)


def _cmp_prompt(nsa_rows, cw):
    b, t, _ = nsa_rows.shape
    nch = t // CMP_STRIDE
    tc = min(128, nch)
    x = nsa_rows.reshape(b, nch, _CHUNK_W)
    wk, w2k, pek, wv, w2v, pev = cw
    full = lambda a: pl.BlockSpec(a.shape, lambda i, j: (0,) * a.ndim)
    a = pl.pallas_call(
        functools.partial(_cmp1_kernel, n_src=1),
        grid=(b, nch // tc),
        in_specs=[pl.BlockSpec((None, tc, _CHUNK_W), lambda i, j: (i, j, 0)),
                  full(wk), full(wv), full(pek), full(pev)],
        out_specs=pl.BlockSpec((None, tc, 4 * NSA_KV), lambda i, j: (i, j, 0)),
        out_shape=jax.ShapeDtypeStruct((b, nch, 4 * NSA_KV), F32),
        compiler_params=_cparams(("parallel", "parallel"), 48),
        name="cmp1_prompt",
    )(x, wk, wv, pek, pev)
    return _cmp_stage2(a, w2k, w2v)


def _cmp1_paged_kernel(*refs, n_pages):
    page_refs = refs[1:1 + 2 * n_pages]
    wk_ref, wv_ref, pek_ref, pev_ref, o_ref, xk_scr, xv_scr = refs[1 + 2 * n_pages:]
    page = page_refs[0].shape[-1]
    for p in range(n_pages):
        for part, scr in ((0, xk_scr), (1, xv_scr)):
            scr[p * page:(p + 1) * page, :] = jnp.transpose(page_refs[2 * p + part][...].reshape(NSA_KV, page))
    n_chunk = n_pages * page // CMP_STRIDE

    def gather(scr):
        return jnp.concatenate([scr[pl.ds(s, n_chunk, stride=CMP_STRIDE), :] for s in range(CMP_STRIDE)], axis=1)

    xk, xv = gather(xk_scr), gather(xv_scr)
    outs = []
    for r in range(CMP_RATIO):
        outs.append(_mm(xk + pek_ref[r], wk_ref[r]))
        outs.append(_mm(xv + pev_ref[r], wv_ref[r]))
    o_ref[...] = jnp.concatenate(outs, axis=1)


def _cmp_decode(cache_t, page_table, cw):
    page = cache_t.shape[-1]
    db, n_pages = page_table.shape
    cpp = page // CMP_STRIDE
    pps = PAGES_PER_STEP
    wk, w2k, pek, wv, w2v, pev = cw
    nch = n_pages * cpp
    full = lambda a: pl.BlockSpec(a.shape, lambda i, j, pt: (0,) * a.ndim)
    k_cmp, v_cmp = 0, 1
    page_specs = [pl.BlockSpec((None, None, NSA_KV_HEADS, HEAD_DIM, page), functools.partial(
        lambda i, j, pt, p, part: (pt[i, j * pps + p], part, 0, 0, 0), p=p, part=part))
        for p in range(pps) for part in (k_cmp, v_cmp)]
    a = pl.pallas_call(
        functools.partial(_cmp1_paged_kernel, n_pages=pps),
        grid_spec=pltpu.PrefetchScalarGridSpec(
            num_scalar_prefetch=1,
            grid=(db, n_pages // pps),
            in_specs=page_specs + [full(wk), full(wv), full(pek), full(pev)],
            out_specs=pl.BlockSpec((None, pps * cpp, 4 * NSA_KV), lambda i, j, pt: (i, j, 0)),
            scratch_shapes=[pltpu.VMEM((pps * page, NSA_KV), F32), pltpu.VMEM((pps * page, NSA_KV), F32)],
        ),
        out_shape=jax.ShapeDtypeStruct((db, nch, 4 * NSA_KV), F32),
        compiler_params=_cparams(("parallel", "parallel"), 48),
        name="cmp1_decode",
    )(page_table, *([cache_t] * len(page_specs)), wk, wv, pek, pev)
    return _cmp_stage2(a, w2k, w2v)


def _cmp_to_sel(nch, ns_pad):
    cs = jnp.arange(nch) * CMP_STRIDE
    ss = jnp.arange(ns_pad) * SEL_BLOCK
    return ((cs[:, None] < ss[None] + SEL_BLOCK) & (cs[:, None] + CMP_BLOCK > ss[None])).astype(F32)


def _nsa_prompt_kernel(qut_ref, qrt_ref, gt_ref, kc_ref, vct_ref, c2st_ref, ks_ref, kw_ref, vst_ref, vwt_ref, o_ref, *,
                       qb, kc):
    s0 = pl.program_id(1) * qb
    nkc = vst_ref.shape[0]
    nch = kc_ref.shape[0]
    ns = c2st_ref.shape[0]
    n_sel = min(SEL_TOPN, ns)
    hg, hd, groups = NSA_GROUP, HEAD_DIM, range(NSA_KV_HEADS)
    sel_shift = SEL_BLOCK.bit_length() - 1
    pos_q = s0 + _iota((1, qb), 1)
    pos_l = jnp.concatenate([pos_q] * hg, axis=1)
    gates = _sigmoid(gt_ref[...])
    zeros = jnp.zeros((hd, hg * qb), F32)

    def group_q(ref, g):
        x = jnp.concatenate([ref[(g * hg + h) * hd:(g * hg + h + 1) * hd, :] for h in range(hg)], axis=1)
        parts = [zeros] * NSA_KV_HEADS
        parts[g] = x.astype(F32) * _QK_SCALE
        return jnp.concatenate(parts, axis=0).astype(BF16)

    qzu = [group_q(qut_ref, g) for g in groups]
    qzr = [group_q(qrt_ref, g) for g in groups]
    cend = _iota((nch, 1), 0) * CMP_STRIDE + (CMP_BLOCK - 1)
    kcb = kc_ref[...]
    p_c = [_masked_softmax(_mm(kcb, qzu[g]), cend <= pos_l, axis=0) for g in groups]
    o_c = [_mm(vct_ref[g * hd:(g + 1) * hd, :], p_c[g]) for g in groups]
    jsel = _iota((ns, 1), 0)
    jsel_f = jsel.astype(F32)
    jq = lax.shift_right_logical(pos_q, sel_shift)
    forced = (jsel == 0) | (jsel == jq) | (jsel == jq - 1)
    sel_b = []
    for g in groups:
        p_sum = p_c[g][:, 0:qb]
        for h in range(1, hg):
            p_sum = p_sum + p_c[g][:, h * qb:(h + 1) * qb]
        score = jnp.where(jsel <= jq, jnp.where(forced, BIG, _mm(c2st_ref[...], p_sum)), -1.0)
        sel = jnp.zeros((ns, qb), F32)
        for _ in range(n_sel):
            m, _, pick = _take_first_max(score, jsel_f, float(ns), 0)
            sel = jnp.where(pick & (m >= 0.0), 1.0, sel)
            score = jnp.where(pick, -2.0, score)
        sel_b.append(sel.astype(BF16))
    blk_lane = _iota((1, ns), 1)

    def sel_step(c, carry):
        k0 = pl.multiple_of(c * kc, kc)
        kpos = k0 + _iota((kc, 1), 0)
        expand = (lax.shift_right_logical(kpos, sel_shift) == blk_lane).astype(BF16)
        causal = kpos <= pos_q
        kk = ks_ref[pl.ds(k0, kc), :]
        s = [_mm(kk, qzr[g]) for g in groups]
        bias = [jnp.where(causal & (jnp.dot(expand, sel_b[g], preferred_element_type=F32) > 0.5), 0.0, NEG)
                for g in groups]
        s = [s[g] + jnp.concatenate([bias[g]] * hg, axis=1) for g in groups]
        m_new = [jnp.maximum(carry[3 * g], jnp.max(s[g], axis=0, keepdims=True)) for g in groups]
        p = [jnp.exp(s[g] - jnp.maximum(m_new[g], 0.5 * NEG)) for g in groups]
        pv = [_mm(vst_ref[c, g * hd:(g + 1) * hd, :], p[g]) for g in groups]
        out = []
        for g in groups:
            m_i, l_i, acc = carry[3 * g:3 * g + 3]
            alpha = jnp.exp(m_i - m_new[g])
            out += [m_new[g], alpha * l_i + jnp.sum(p[g], axis=0, keepdims=True), alpha * acc + pv[g]]
        return tuple(out)

    init = (jnp.full((1, hg * qb), NEG, F32), jnp.zeros((1, hg * qb), F32), zeros) * NSA_KV_HEADS
    fin = lax.fori_loop(0, s0 // kc + 1, sel_step, init)
    o_s = [fin[3 * g + 2] / jnp.where(fin[3 * g + 1] > 0, fin[3 * g + 1], 1.0) for g in groups]
    nwc = WINDOW // kc + 1
    c_lo = jnp.minimum(jnp.maximum(s0 - WINDOW, 0) // kc, nkc - nwc)
    wstart = pl.multiple_of(c_lo * kc, kc)
    wpos = wstart + _iota((nwc * kc, 1), 0)
    mask_w = (wpos <= pos_l) & (wpos > pos_l - WINDOW)
    kwb = kw_ref[pl.ds(wstart, nwc * kc), :]
    p_w = [_masked_softmax(_mm(kwb, qzr[g]), mask_w, axis=0) for g in groups]
    o_w = []
    for g in groups:
        acc = _mm(vwt_ref[c_lo, g * hd:(g + 1) * hd, :], p_w[g][0:kc])
        for i in range(1, nwc):
            acc = acc + _mm(vwt_ref[c_lo + i, g * hd:(g + 1) * hd, :], p_w[g][i * kc:(i + 1) * kc])
        o_w.append(acc)
    rows = []
    for g in groups:
        for h in range(hg):
            hh, lanes = g * hg + h, slice(h * qb, (h + 1) * qb)
            rows.append(gates[3 * hh:3 * hh + 1] * o_c[g][:, lanes] + gates[3 * hh + 1:3 * hh + 2] * o_s[g][:, lanes]
                        + gates[3 * hh + 2:3 * hh + 3] * o_w[g][:, lanes])
    o_ref[...] = jnp.transpose(jnp.concatenate(rows, axis=0)).astype(o_ref.dtype)


def _nsa_prompt(p, kcv, vct, b, t):
    qb, kc = NSA_QB, TOKEN_TILE
    nwc = WINDOW // kc + 1
    assert t % kc == 0 and kc % qb == 0 and t >= nwc * kc and t % SEL_BLOCK == 0 and WINDOW % kc == 0
    nkc = t // kc
    nch = kcv.shape[1]
    ns = t // SEL_BLOCK
    c2st = _cmp_to_sel(nch, ns).T.astype(BF16)
    katt = p["katt"].reshape(b, t, 4 * NSA_KV)
    qsub = kc // qb
    q_spec = lambda w: pl.BlockSpec((None, w, qb), lambda i, j: (i * nkc + j // qsub, 0, j % qsub))
    chunks = lambda a: a.reshape(b, nkc, NSA_KV, kc)
    chunk_spec = pl.BlockSpec((None, nkc, NSA_KV, kc), lambda i, j: (i, 0, 0, 0))
    return pl.pallas_call(
        functools.partial(_nsa_prompt_kernel, qb=qb, kc=kc),
        grid=(b, t // qb),
        in_specs=[q_spec(NSA_Q), q_spec(NSA_Q), q_spec(GATE_PAD),
                  pl.BlockSpec((None, nch, NSA_KV), lambda i, j: (i, 0, 0)),
                  pl.BlockSpec((None, NSA_KV, nch), lambda i, j: (i, 0, 0)),
                  pl.BlockSpec((ns, nch), lambda i, j: (0, 0)),
                  pl.BlockSpec((None, t, NSA_KV), lambda i, j: (i, 0, 0)),
                  pl.BlockSpec((None, t, NSA_KV), lambda i, j: (i, 0, 2)),
                  chunk_spec, chunk_spec],
        out_specs=pl.BlockSpec((None, qb, NSA_Q), lambda i, j: (i, j, 0)),
        out_shape=jax.ShapeDtypeStruct((b, t, NSA_Q), BF16),
        compiler_params=_cparams(("parallel", "arbitrary"), 48),
        name="nsa_prompt",
    )(p["qut"], p["qrt"], p["gt"], kcv, vct, c2st, katt, katt, chunks(p["vst"]), chunks(p["vwt"])).reshape(b * t, NSA_Q)


def _moba_prompt_kernel(qt_ref, km_ref, k_ref, vt_ref, o_ref, *, blk):
    jq = pl.program_id(2)
    nb = km_ref.shape[0]
    n_top = min(MOBA_TOPK, nb)
    hpl = LANES // HEAD_DIM
    n_heads = qt_ref.shape[0] // HEAD_DIM
    jb = _iota((nb, 1), 0)
    jb_f = jb.astype(F32)
    row_head = lax.shift_right_logical(_iota((LANES, 1), 0), HEAD_DIM.bit_length() - 1)
    tile = lambda hh: slice(hh // hpl * LANES, (hh // hpl + 1) * LANES)
    qz, sels = [], []
    for hh in range(n_heads):
        qt = qt_ref[tile(hh), :].astype(F32) * _QK_SCALE
        qz.append(jnp.where(row_head == hh % hpl, qt, 0.0).astype(BF16))
        score = jnp.where(jb < jq, _mm(km_ref[:, tile(hh)], qz[hh]), NEG)
        sel = jnp.zeros((nb, blk), F32)
        for _ in range(n_top):
            m, _, pick = _take_first_max(score, jb_f, float(nb), 0)
            sel = jnp.where(pick & (m > 0.5 * NEG), 1.0, sel)
            score = jnp.where(pick, 3.0 * NEG, score)
        sels.append(sel)

    def attend(j, carry, mask_fn):
        k0 = pl.multiple_of(j * blk, blk)
        heads = range(n_heads)
        s = [_mm(k_ref[pl.ds(k0, blk), tile(hh)], qz[hh]) for hh in heads]
        m_new, m_sub = zip(*[mask_fn(hh, s[hh], carry[3 * hh]) for hh in heads])
        p = [jnp.exp(s[hh] - m_sub[hh]) for hh in heads]
        pv = [_mm(vt_ref[j, hh * HEAD_DIM:(hh + 1) * HEAD_DIM, :], p[hh]) for hh in heads]
        out = []
        for hh in heads:
            m_i, l_i, acc = carry[3 * hh:3 * hh + 3]
            alpha = jnp.exp(m_i - m_new[hh])
            out += [m_new[hh], alpha * l_i + jnp.sum(p[hh], axis=0, keepdims=True), alpha * acc + pv[hh]]
        return tuple(out)

    def past_mask(j):
        def fn(hh, s, m_i):
            picked = jnp.sum(jnp.where(jb == j, sels[hh], 0.0), axis=0, keepdims=True) > 0.5
            m_new = jnp.maximum(m_i, jnp.where(picked, jnp.max(s, axis=0, keepdims=True), NEG))
            return m_new, jnp.where(picked, m_new, BIG)
        return fn

    init = (jnp.full((1, blk), NEG, F32), jnp.zeros((1, blk), F32), jnp.zeros((HEAD_DIM, blk), F32)) * n_heads
    carry = lax.fori_loop(0, jq, lambda j, c: attend(j, c, past_mask(j)), init)
    causal = _iota((blk, 1), 0) <= _iota((1, blk), 1)

    def own_mask(hh, s, m_i):
        m_new = jnp.maximum(m_i, jnp.max(jnp.where(causal, s, NEG), axis=0, keepdims=True))
        return m_new, jnp.where(causal, m_new, BIG)

    final = attend(jq, carry, own_mask)
    outs = [final[3 * hh + 2] / final[3 * hh + 1] for hh in range(n_heads)]
    o_ref[...] = jnp.transpose(jnp.concatenate(outs, axis=0)).astype(o_ref.dtype)


def _moba_prompt(p, b, t):
    blk = MOBA_BLOCK
    assert t % blk == 0
    nb = t // blk
    w = MOBA_HEADS_PER_STEP * HEAD_DIM
    nstep = MOBA_W // w
    qt = p["qbt"]
    vt = p["vbt"].reshape(b, nb, MOBA_W, blk)
    kv = p["kvb"].reshape(b, t, MOBA_ROW)
    km = p["kmean"].reshape(b, nb, MOBA_W)
    return pl.pallas_call(
        functools.partial(_moba_prompt_kernel, blk=blk),
        grid=(b, nstep, nb),
        in_specs=[pl.BlockSpec((None, w, blk), lambda i, hp, j: (i * nb + j, hp, 0)),
                  pl.BlockSpec((None, nb, w), lambda i, hp, j: (i, 0, hp)),
                  pl.BlockSpec((None, t, w), lambda i, hp, j: (i, 0, hp)),
                  pl.BlockSpec((None, nb, w, blk), lambda i, hp, j: (i, 0, hp, 0))],
        out_specs=pl.BlockSpec((None, blk, w), lambda i, hp, j: (i, j, hp)),
        out_shape=jax.ShapeDtypeStruct((b, t, MOBA_W), BF16),
        compiler_params=_cparams(("parallel", "parallel", "arbitrary"), 56),
        name="moba_prompt",
    )(qt, km, kv, vt).reshape(b * t, MOBA_W)


def _group_rows(n_rows=NSA_HEADS):
    return _iota((n_rows, 1), 0) < NSA_GROUP


def _nsa_dec_cmp_kernel(qu_ref, kcv_ref, c2s_ref, oc_ref, idx_ref, *, pos):
    assert NSA_KV_HEADS == 2
    q = qu_ref[...]
    nch = kcv_ref.shape[0]
    ns = c2s_ref.shape[1]
    n_sel = min(SEL_TOPN, -(-(pos + 1) // SEL_BLOCK))
    scale = HEAD_DIM ** -0.5
    g0 = _group_rows()
    hd = HEAD_DIM
    kcv = kcv_ref[...]
    s = jnp.where(g0, _mm_nt(q, kcv[:, 0:hd]), _mm_nt(q, kcv[:, hd:2 * hd])) * scale
    cend = _iota((1, nch), 1) * CMP_STRIDE + (CMP_BLOCK - 1)
    p = _masked_softmax(s, cend <= pos)
    oc_ref[...] = jnp.where(g0, _mm(p, kcv[:, NSA_KV:NSA_KV + hd]), _mm(p, kcv[:, NSA_KV + hd:NSA_KV + 2 * hd]))
    p0 = jnp.sum(jnp.where(g0, p, 0.0), axis=0, keepdims=True)
    p1 = jnp.sum(jnp.where(g0, 0.0, p), axis=0, keepdims=True)
    imp = _mm(jnp.where(g0, p0, p1), c2s_ref[...])
    jsel = _iota((1, ns), 1)
    jsel_f = jsel.astype(F32)
    jq = pos // SEL_BLOCK
    forced = (jsel == 0) | (jsel == jq) | (jsel == jq - 1)
    score = jnp.where(jsel <= jq, jnp.where(forced, BIG, imp), -1.0)
    lane = _iota((1, LANES), 1)
    idx = jnp.full((NSA_HEADS, LANES), -1, I32)
    for it in range(n_sel):
        m, first, pick = _take_first_max(score, jsel_f, float(ns), 1)
        idx = jnp.where(lane == it, jnp.where(m >= 0.0, first, -1.0).astype(I32), idx)
        score = jnp.where(pick, -2.0, score)
    idx_ref[...] = idx


def _attend_with_new(q, kts, vts, valids, k_new, v_new, new_valid, scale):
    s = [_mm(q, kt) * scale for kt in kts]
    qf = q.astype(MXU_DTYPE).astype(F32)
    s_new = jnp.sum(qf * k_new.astype(MXU_DTYPE).astype(F32), axis=1, keepdims=True) * scale
    m = jnp.where(new_valid, s_new, NEG)
    for sj, vj in zip(s, valids):
        m = jnp.maximum(m, jnp.max(jnp.where(vj > 0.5, sj, NEG), axis=1, keepdims=True))
    p = [jnp.where(vj > 0.5, jnp.exp(sj - m), 0.0) for sj, vj in zip(s, valids)]
    p_new = jnp.where(new_valid, jnp.exp(s_new - m), 0.0)
    d = p_new
    for pj in p:
        d = d + jnp.sum(pj, axis=1, keepdims=True)
    d = jnp.where(d > 0, d, 1.0)
    o = (p_new / d).astype(MXU_DTYPE).astype(F32) * v_new.astype(MXU_DTYPE).astype(F32)
    for pj, vt in zip(p, vts):
        o = o + _mm_nt(pj / d, vt)
    return o


def _nsa_dec_att_kernel(idx_ref, pt_ref, qr_ref, gate_ref, oc_ref, new_ref, win_ref, *rest, pos, n_sel, past_blocks):
    del pt_ref
    blk_refs, o_ref = rest[:-1], rest[-1]
    b = pl.program_id(0)
    q = qr_ref[...]
    scale = HEAD_DIM ** -0.5
    hd = HEAD_DIM
    g0 = _group_rows()
    new = new_ref[...]
    n_win = win_ref.shape[-1]
    wpos = pos - n_win + _iota((1, n_win), 1)
    valid_w = jnp.where((wpos > pos - WINDOW) & (wpos >= 0), 1.0, 0.0)
    page = blk_refs[0].shape[-1]
    bpp = page // SEL_BLOCK
    blk_of_lane = lax.shift_right_logical(_iota((1, page), 1), SEL_BLOCK.bit_length() - 1)
    o_s, o_w = [], []
    for g in range(NSA_KV_HEADS):
        glo = g * hd
        kts, vts, valids = [], [], []
        new_valid = jnp.zeros((1, 1), F32)
        for j in range(n_sel):
            bj = idx_ref[b, g, j]
            kts.append(blk_refs[2 * (g * n_sel + j)][...])
            vts.append(blk_refs[2 * (g * n_sel + j) + 1][...])
            in_past = jnp.where((bj >= 0) & (bj < past_blocks), 1.0, 0.0)
            valids.append(jnp.where(blk_of_lane == bj % bpp, in_past, 0.0))
            new_valid = jnp.maximum(new_valid, jnp.where(bj == past_blocks, 1.0, 0.0))
        o_s.append(_attend_with_new(q, kts, vts, valids, new[:, glo:glo + hd],
                                    new[:, NSA_KV + glo:NSA_KV + glo + hd], new_valid > 0.5, scale))
        o_w.append(_attend_with_new(q, [win_ref[0, g]], [win_ref[1, g]], [valid_w],
                                    new[:, 2 * NSA_KV + glo:2 * NSA_KV + glo + hd],
                                    new[:, 3 * NSA_KV + glo:3 * NSA_KV + glo + hd], True, scale))
    gs = _sigmoid(gate_ref[...])
    o_ref[...] = (gs[:, 0:1] * oc_ref[...] + gs[:, 1:2] * jnp.where(g0, o_s[0], o_s[1])
                  + gs[:, 2:3] * jnp.where(g0, o_w[0], o_w[1]))


def _nsa_decode_select(p, kcv, pos):
    db = kcv.shape[0]
    nch = kcv.shape[1]
    ns = pos // SEL_BLOCK + 1
    ns_pad = -(-ns // LANES) * LANES
    n_sel = min(SEL_TOPN, ns)
    c2s = _cmp_to_sel(nch, ns_pad).astype(BF16)
    per_q = lambda w: pl.BlockSpec((None, NSA_HEADS, w), lambda i: (i, 0, 0))
    o_c, idx = pl.pallas_call(
        functools.partial(_nsa_dec_cmp_kernel, pos=pos),
        grid=(db,),
        in_specs=[per_q(HEAD_DIM), pl.BlockSpec((None, nch, 2 * NSA_KV), lambda i: (i, 0, 0)),
                  pl.BlockSpec((nch, ns_pad), lambda i: (0, 0))],
        out_specs=[per_q(HEAD_DIM), per_q(LANES)],
        out_shape=[jax.ShapeDtypeStruct((db, NSA_HEADS, HEAD_DIM), F32),
                   jax.ShapeDtypeStruct((db, NSA_HEADS, LANES), I32)],
        compiler_params=_cparams(("parallel",)),
        name="nsa_dec_cmp",
    )(p["qu"].reshape(db, NSA_HEADS, HEAD_DIM), kcv, c2s)
    return o_c, idx[:, ::NSA_GROUP, :n_sel]


def _nsa_decode(p, kcv, cache_t, win_t, page_table, pos):
    db, n_pages = page_table.shape
    page = cache_t.shape[-1]
    assert pos == n_pages * page and pos % SEL_BLOCK == 0 and page % SEL_BLOCK == 0
    past_blocks = pos // SEL_BLOCK
    o_c, sel_idx = _nsa_decode_select(p, kcv, pos)
    n_sel = sel_idx.shape[2]
    heads3 = lambda a: a.reshape(db, NSA_HEADS, HEAD_DIM)
    bpp = page // SEL_BLOCK

    def blk_map(i, ix, pt, g, j, part):
        bj = jnp.clip(ix[i, g, j], 0, past_blocks - 1)
        return (pt[i, bj // bpp], part, g, 0, 0)

    k_sel, v_sel = 2, 3
    blk_specs = [pl.BlockSpec((None, None, None, HEAD_DIM, page), functools.partial(blk_map, g=g, j=j, part=part))
                 for g in range(NSA_KV_HEADS) for j in range(n_sel) for part in (k_sel, v_sel)]
    per_q2 = lambda w: pl.BlockSpec((None, NSA_HEADS, w), lambda i, ix, pt: (i, 0, 0))
    n_win = win_t.shape[-1]
    gate3 = p["gate"][:, :3 * NSA_HEADS].reshape(db, NSA_HEADS, 3)
    o = pl.pallas_call(
        functools.partial(_nsa_dec_att_kernel, pos=pos, n_sel=n_sel, past_blocks=past_blocks),
        grid_spec=pltpu.PrefetchScalarGridSpec(
            num_scalar_prefetch=2,
            grid=(db,),
            in_specs=[per_q2(HEAD_DIM), per_q2(3), per_q2(HEAD_DIM),
                      pl.BlockSpec((None, 1, 4 * NSA_KV), lambda i, ix, pt: (i, 0, 0)),
                      pl.BlockSpec((None, 2, NSA_KV_HEADS, HEAD_DIM, n_win), lambda i, ix, pt: (i, 0, 0, 0, 0))]
            + blk_specs,
            out_specs=per_q2(HEAD_DIM),
        ),
        out_shape=jax.ShapeDtypeStruct((db, NSA_HEADS, HEAD_DIM), F32),
        compiler_params=_cparams(("arbitrary",)),
        name="nsa_dec_att",
    )(sel_idx, page_table, heads3(p["qr"]), gate3, o_c, p["katt"].reshape(db, 1, 4 * NSA_KV), win_t,
      *([cache_t] * len(blk_specs)))
    return o.reshape(db, NSA_Q)


def _moba_dec_mean_kernel(*refs, n_src, ppb):
    x_refs, o_ref = refs[1:1 + n_src], refs[1 + n_src]
    j = pl.program_id(1)
    nb = o_ref.shape[1]
    page = x_refs[0].shape[-1]
    bps = n_src // ppb

    @pl.when(j == 0)
    def _():
        o_ref[...] = jnp.zeros(o_ref.shape, F32)

    lane = _iota((1, nb), 1)
    acc = o_ref[...]
    for i in range(bps):
        tot = x_refs[i * ppb][...]
        for r in x_refs[i * ppb + 1:(i + 1) * ppb]:
            tot = tot + r[...]
        col = jnp.sum(tot.reshape(MOBA_W, page), axis=1, keepdims=True) * (1.0 / (ppb * page))
        acc = jnp.where(lane == j * bps + i, col, acc)
    o_ref[...] = acc


def _moba_dec_gate_kernel(q_ref, km_ref, idx_ref, *, jq):
    q = q_ref[...]
    nb = km_ref.shape[1]
    n_top = min(MOBA_TOPK, nb)
    head_of_lane = lax.shift_right_logical(_iota((MOBA_HEADS, MOBA_W), 1), HEAD_DIM.bit_length() - 1)
    qbd = jnp.where(head_of_lane == _iota((MOBA_HEADS, MOBA_W), 0),
                    jnp.broadcast_to(q.astype(F32), (MOBA_HEADS, MOBA_W)), 0.0)
    jb = _iota((1, nb), 1)
    jb_f = jb.astype(F32)
    score = jnp.where(jb < jq, _mm(qbd, km_ref[...]), NEG)
    lane = _iota((1, LANES), 1)
    idx = jnp.full((MOBA_HEADS, LANES), -1, I32)
    for it in range(n_top):
        m, first, pick = _take_first_max(score, jb_f, float(nb), 1)
        idx = jnp.where(lane == it, jnp.where(m > 0.5 * NEG, first, -1.0).astype(I32), idx)
        score = jnp.where(pick, 3.0 * NEG, score)
    idx_ref[...] = idx


def _moba_dec_att_kernel(idx_ref, pt_ref, q_ref, kn_ref, vn_ref, *rest, n_top, ppb):
    del pt_ref
    src, o_ref = rest[:-1], rest[-1]
    b, hp = pl.program_id(0), pl.program_id(1)
    hpl = LANES // HEAD_DIM
    scale = HEAD_DIM ** -0.5
    rows = 8
    page = src[0].shape[-1]
    q_all = jnp.broadcast_to(q_ref[...].astype(F32), (rows, LANES))
    k_new = kn_ref[...]
    v_new = vn_ref[...]
    outs = []
    for hh in range(hpl):
        lanes = slice(hh * HEAD_DIM, (hh + 1) * HEAD_DIM)
        kts, vts, valids = [], [], []
        for t in range(n_top):
            ok = jnp.where(idx_ref[b, hp * hpl + hh, t] >= 0, 1.0, 0.0)
            for pg in range(ppb):
                base = 2 * ((hh * n_top + t) * ppb + pg)
                kts.append(src[base][...])
                vts.append(src[base + 1][...])
                valids.append(jnp.full((1, page), 1.0, F32) * ok)
        o = _attend_with_new(q_all[:, lanes], kts, vts, valids, k_new[:, lanes], v_new[:, lanes], True, scale)
        outs.append(o[0:1])
    o_ref[...] = jnp.concatenate(outs, axis=1)


def _moba_decode(p, cache_t, page_table, pos):
    db, n_pages = page_table.shape
    page = cache_t.shape[-1]
    assert MOBA_BLOCK % page == 0 and pos % MOBA_BLOCK == 0 and pos == n_pages * page
    ppb = MOBA_BLOCK // page
    nb = pos // MOBA_BLOCK
    assert nb >= MOBA_TOPK
    pps = PAGES_PER_STEP
    page_specs = [pl.BlockSpec((None, None, MOBA_HEADS, HEAD_DIM, page), functools.partial(
        lambda i, j, pt, pg: (pt[i, j * pps + pg], 0, 0, 0, 0), pg=pg)) for pg in range(pps)]
    kmean_t = pl.pallas_call(
        functools.partial(_moba_dec_mean_kernel, n_src=pps, ppb=ppb),
        grid_spec=pltpu.PrefetchScalarGridSpec(
            num_scalar_prefetch=1,
            grid=(db, n_pages // pps),
            in_specs=page_specs,
            out_specs=pl.BlockSpec((None, MOBA_W, nb), lambda i, j, pt: (i, 0, 0)),
        ),
        out_shape=jax.ShapeDtypeStruct((db, MOBA_W, nb), F32),
        compiler_params=_cparams(("parallel", "arbitrary")),
        name="moba_dec_mean",
    )(page_table, *([cache_t] * pps))
    idx = pl.pallas_call(
        functools.partial(_moba_dec_gate_kernel, jq=nb),
        grid=(db,),
        in_specs=[pl.BlockSpec((None, 1, MOBA_W), lambda i: (i, 0, 0)),
                  pl.BlockSpec((None, MOBA_W, nb), lambda i: (i, 0, 0))],
        out_specs=pl.BlockSpec((None, MOBA_HEADS, LANES), lambda i: (i, 0, 0)),
        out_shape=jax.ShapeDtypeStruct((db, MOBA_HEADS, LANES), I32),
        compiler_params=_cparams(("parallel",)),
        name="moba_dec_gate",
    )(p["qb"].reshape(db, 1, MOBA_W), kmean_t)
    n_top = min(MOBA_TOPK, nb)
    top_idx = idx[:, :, :n_top]
    hpl = LANES // HEAD_DIM
    npair = MOBA_HEADS // hpl

    def src_map(i, hp, ix, pt, hh, t, pg, kv):
        bj = jnp.clip(ix[i, hp * hpl + hh, t], 0, nb - 1)
        return (pt[i, bj * ppb + pg], kv, hp * hpl + hh, 0, 0)

    src_specs = [pl.BlockSpec((None, None, None, HEAD_DIM, page), functools.partial(src_map, hh=hh, t=t, pg=pg, kv=kv))
                 for hh in range(hpl) for t in range(n_top) for pg in range(ppb) for kv in range(2)]
    pair = lambda off: pl.BlockSpec((None, None, 1, LANES), lambda i, hp, ix, pt: (i, off + hp, 0, 0))
    o = pl.pallas_call(
        functools.partial(_moba_dec_att_kernel, n_top=n_top, ppb=ppb),
        grid_spec=pltpu.PrefetchScalarGridSpec(
            num_scalar_prefetch=2,
            grid=(db, npair),
            in_specs=[pair(0), pair(0), pair(npair)] + src_specs,
            out_specs=pair(0),
        ),
        out_shape=jax.ShapeDtypeStruct((db, npair, 1, LANES), F32),
        compiler_params=_cparams(("arbitrary", "arbitrary")),
        name="moba_dec_att",
    )(top_idx, page_table, p["qb"].reshape(db, npair, 1, LANES), p["kvb"].reshape(db, 2 * npair, 1, LANES),
      p["kvb"].reshape(db, 2 * npair, 1, LANES), *([cache_t] * len(src_specs)))
    return o.reshape(db, MOBA_W)


def _route_t(s_t, b_t):
    n_e, n_tok = s_t.shape
    per = n_e // N_GROUPS
    biased = s_t + b_t
    sub_f = _iota((per, 1), 0).astype(F32)
    gscore = []
    for g in range(N_GROUPS):
        x = biased[g * per:(g + 1) * per]
        m1, _, pick = _take_first_max(x, sub_f, float(per), 0)
        gscore.append(m1 + jnp.max(jnp.where(pick, NEG, x), axis=0, keepdims=True))
    gs = jnp.concatenate(gscore, axis=0)
    g_f = _iota((N_GROUPS, 1), 0).astype(F32)
    gmask = jnp.zeros((N_GROUPS, n_tok), F32)
    for _ in range(TOPK_GROUPS):
        _, _, pick = _take_first_max(gs, g_f, float(N_GROUPS), 0)
        gmask = jnp.where(pick, 1.0, gmask)
        gs = jnp.where(pick, NEG, gs)
    masked = jnp.concatenate([jnp.where(gmask[g:g + 1] > 0.5, biased[g * per:(g + 1) * per], NEG)
                              for g in range(N_GROUPS)], axis=0)
    e_f = _iota((n_e, 1), 0).astype(F32)
    ids, ws = [], []
    for _ in range(TOP_K):
        _, first, pick = _take_first_max(masked, e_f, float(n_e), 0)
        ids.append(first)
        ws.append(jnp.sum(jnp.where(pick, s_t, 0.0), axis=0, keepdims=True))
        masked = jnp.where(pick, 3.0 * NEG, masked)
    w = jnp.concatenate(ws, axis=0)
    w = w / jnp.sum(w, axis=0, keepdims=True) * ROUTED_SCALE
    return jnp.concatenate(ids, axis=0).astype(I32), w


def _merge_kernel(x_ref, oa_ref, ob_ref, sc1_ref, sh1_ref, g1_ref, sc2_ref, sh2_ref, wmg_ref, wa_ref, wb_ref,
                  wo_ref, lg_ref, lb_ref, wr_ref, br_ref, x1_ref, h_ref, ti_ref, tw_ref, *, alpha):
    x = x_ref[...]
    d = x.shape[1]
    u = x * (1.0 + sc1_ref[...]) + sh1_ref[...]
    mg = _mm(u, wmg_ref[...])
    y_a = _mm(oa_ref[...], wa_ref[...])
    y_b = _mm(ob_ref[...], wb_ref[...])
    mix = _mm(_sigmoid(mg[:, :d]) * y_a + _sigmoid(mg[:, d:]) * y_b, wo_ref[...])
    x1 = _layer_norm(alpha * x + g1_ref[...] * mix, lg_ref[...], lb_ref[...])
    x1_ref[...] = x1
    h = x1 * (1.0 + sc2_ref[...]) + sh2_ref[...]
    h_ref[...] = h
    s_t = _sigmoid(_mm_nt(wr_ref[...], h))
    ti_ref[...], tw_ref[...] = _route_t(s_t, br_ref[...])


def _merge(x, o_a, o_b, mods, w, *, tm, per_token_mod, rows_per_batch, alpha):
    n, d = x.shape
    nt = n // tm
    tiles_per_batch = rows_per_batch // tm
    row = lambda i: (i, 0)
    if per_token_mod:
        mod_spec = pl.BlockSpec((tm, d), row)
    else:
        mod_spec = pl.BlockSpec((None, 1, d), lambda i: (i // tiles_per_batch, 0, 0))
    full = lambda a: pl.BlockSpec(a.shape, lambda i: (0,) * a.ndim)
    ws = [w["w_mg"], w["w_nsa_out"], w["w_moba_out"], w["w_o"], w["ln1_g"], w["ln1_b"], w["w_router_t"], w["b_router"]]
    return pl.pallas_call(
        functools.partial(_merge_kernel, alpha=alpha),
        grid=(nt,),
        in_specs=[pl.BlockSpec((tm, d), row), pl.BlockSpec((tm, NSA_Q), row), pl.BlockSpec((tm, MOBA_W), row)]
        + [mod_spec] * 5 + [full(a) for a in ws],
        out_specs=[pl.BlockSpec((tm, d), row), pl.BlockSpec((tm, d), row),
                   pl.BlockSpec((TOP_K, tm), lambda i: (0, i)), pl.BlockSpec((TOP_K, tm), lambda i: (0, i))],
        out_shape=[jax.ShapeDtypeStruct((n, d), F32), jax.ShapeDtypeStruct((n, d), F32),
                   jax.ShapeDtypeStruct((TOP_K, n), I32), jax.ShapeDtypeStruct((TOP_K, n), F32)],
        compiler_params=_cparams(("parallel",), 48),
        name="merge",
    )(x, o_a, o_b, *mods, *ws)


def _rank_kernel(ti_ref, rank_ref, cnt_ref, base_ref):
    @pl.when(pl.program_id(0) == 0)
    def _():
        base_ref[...] = jnp.zeros(base_ref.shape, F32)

    ti = ti_ref[...]
    k, tr = ti.shape
    e_iota = _iota((N_EXPERTS, 1), 0)
    hit = lambda j: e_iota == ti[j:j + 1, :]
    onehot = jnp.zeros((N_EXPERTS, tr), F32)
    for j in range(k):
        onehot = onehot + jnp.where(hit(j), 1.0, 0.0)
    tri = (_iota((tr, 1), 0) <= _iota((1, tr), 1)).astype(BF16)
    before = jnp.dot(onehot.astype(BF16), tri, preferred_element_type=F32) - onehot + base_ref[...]
    rank_ref[...] = jnp.concatenate([jnp.sum(jnp.where(hit(j), before, 0.0), axis=0, keepdims=True)
                                     for j in range(k)], axis=0).astype(I32)
    total = base_ref[...] + jnp.sum(onehot, axis=1, keepdims=True)
    base_ref[...] = total
    cnt_ref[...] = jnp.broadcast_to(total, cnt_ref.shape).astype(I32)


def _slot_kernel(ti_ref, rank_ref, start_ref, slot_ref):
    ti = ti_ref[...]
    e_iota = _iota((N_EXPERTS, 1), 0)
    start = start_ref[...]
    rows = [jnp.sum(jnp.where(e_iota == ti[j:j + 1, :], start, 0.0), axis=0, keepdims=True) for j in range(ti.shape[0])]
    slot_ref[...] = rank_ref[...] + jnp.concatenate(rows, axis=0).astype(I32)


def _row_copies(n_tok, n_choice, make_copy):
    def start(t, c):
        for j in range(n_choice):
            make_copy(t, j).start()
        return c

    def wait(t, c):
        for j in range(n_choice):
            make_copy(t, j).wait()
        return c

    return (lambda: lax.fori_loop(0, n_tok, start, 0)), (lambda: lax.fori_loop(0, n_tok, wait, 0))


def _scatter_rows_kernel(slot_ref, h_ref, xs_ref, sem):
    k, td = slot_ref.shape
    copy = lambda t, j: pltpu.make_async_copy(h_ref.at[pl.ds(t, 1), :], xs_ref.at[pl.ds(slot_ref[j, t], 1), :], sem)
    start, wait = _row_copies(td, k, copy)
    start()
    wait()


def _expert_kernel(be_ref, nu_ref, x_ref, wg_ref, wu_ref, wd_ref, y_ref):
    del be_ref

    @pl.when(pl.program_id(0) < nu_ref[0])
    def _():
        x = x_ref[...]
        y_ref[...] = _mm(_silu(_mm(x, wg_ref[...])) * _mm(x, wu_ref[...]), wd_ref[...])


def _token_tile(n, pref):
    return pref if n % pref == 0 else n


def _routed_experts(h, top_i, w):
    n, d = h.shape
    k = top_i.shape[0]
    blk = min(EXPERT_BLK, max(8, (k * n) // N_EXPERTS))
    nblk = -(-(k * n) // blk) + N_EXPERTS
    tr = _token_tile(n, RANK_TILE)
    tiles = lambda t: pl.BlockSpec((k, t), lambda i: (0, i))
    rank, counts = pl.pallas_call(
        _rank_kernel,
        grid=(n // tr,),
        in_specs=[tiles(tr)],
        out_specs=[tiles(tr), pl.BlockSpec((N_EXPERTS, LANES), lambda i: (0, 0))],
        out_shape=[jax.ShapeDtypeStruct((k, n), I32), jax.ShapeDtypeStruct((N_EXPERTS, LANES), I32)],
        scratch_shapes=[pltpu.VMEM((N_EXPERTS, 1), F32)],
        compiler_params=_cparams(("arbitrary",)),
        name="moe_rank",
    )(top_i)
    counts = counts[:, 0]
    padded = (counts + blk - 1) // blk * blk
    end_pad = jnp.cumsum(padded)
    start_pad = end_pad - padded
    blk_e = jnp.minimum(jnp.sum(end_pad[None, :] <= (jnp.arange(nblk, dtype=I32) * blk)[:, None], axis=1),
                        N_EXPERTS - 1).astype(I32)
    n_used = (end_pad[-1] // blk).astype(I32).reshape(1)
    slot = pl.pallas_call(
        _slot_kernel,
        grid=(n // tr,),
        in_specs=[tiles(tr), tiles(tr), pl.BlockSpec((N_EXPERTS, 1), lambda i: (0, 0))],
        out_specs=tiles(tr),
        out_shape=jax.ShapeDtypeStruct((k, n), I32),
        compiler_params=_cparams(("parallel",)),
        name="moe_slot",
    )(top_i, rank, start_pad.astype(F32).reshape(N_EXPERTS, 1))
    td = _token_tile(n, ROW_COPY_TILE)
    x_sorted = pl.pallas_call(
        _scatter_rows_kernel,
        grid=(n // td,),
        in_specs=[pl.BlockSpec((k, td), lambda i: (0, i), memory_space=pltpu.SMEM),
                  pl.BlockSpec((td, d), lambda i: (i, 0))],
        out_specs=pl.BlockSpec(memory_space=pl.ANY),
        out_shape=jax.ShapeDtypeStruct((nblk * blk, d), F32),
        scratch_shapes=[pltpu.SemaphoreType.DMA],
        compiler_params=_cparams(("arbitrary",)),
        name="moe_scatter",
    )(slot, h)
    de = w["w_exp_gate"].shape[2]
    y = pl.pallas_call(
        _expert_kernel,
        grid_spec=pltpu.PrefetchScalarGridSpec(
            num_scalar_prefetch=2,
            grid=(nblk,),
            in_specs=[pl.BlockSpec((blk, d), lambda i, be, nu: (i, 0)),
                      pl.BlockSpec((None, d, de), lambda i, be, nu: (be[i], 0, 0)),
                      pl.BlockSpec((None, d, de), lambda i, be, nu: (be[i], 0, 0)),
                      pl.BlockSpec((None, de, d), lambda i, be, nu: (be[i], 0, 0))],
            out_specs=pl.BlockSpec((blk, d), lambda i, be, nu: (i, 0)),
        ),
        out_shape=jax.ShapeDtypeStruct((nblk * blk, d), F32),
        compiler_params=_cparams(("arbitrary",), 48),
        name="experts",
    )(blk_e, n_used, x_sorted, w["w_exp_gate"], w["w_exp_up"], w["w_exp_down"])
    return y, slot


def _final_kernel(slot_ref, x1_ref, h_ref, tw_ref, g2_ref, wg_ref, wu_ref, wd_ref, lg_ref, lb_ref, y_ref, o_ref,
                  buf_ref, sem, *, alpha):
    k, tc = slot_ref.shape
    copy = lambda t, j: pltpu.make_async_copy(y_ref.at[pl.ds(slot_ref[j, t], 1), :],
                                              buf_ref.at[j, pl.ds(t, 1), :], sem)
    start, wait = _row_copies(tc, k, copy)
    start()
    h = h_ref[...]
    shared = _mm(_silu(_mm(h, wg_ref[...])) * _mm(h, wu_ref[...]), wd_ref[...])
    wait()
    side = max(tc, LANES)
    tw = tw_ref[...]
    if tc < side:
        tw = jnp.concatenate([tw, jnp.zeros((k, side - tc), F32)], axis=1)
    tw_t = jnp.transpose(jnp.concatenate([tw, jnp.zeros((side - k, side), F32)], axis=0))[0:tc]
    routed = tw_t[:, 0:1] * buf_ref[0]
    for j in range(1, k):
        routed = routed + tw_t[:, j:j + 1] * buf_ref[j]
    o_ref[...] = _layer_norm(alpha * x1_ref[...] + g2_ref[...] * (routed + shared), lg_ref[...], lb_ref[...])


def _final(x1, h, y_sorted, slot, top_w, g2, w, *, per_token_mod, rows_per_batch, alpha):
    n, d = x1.shape
    k = slot.shape[0]
    tc = _token_tile(n, ROW_COPY_TILE)
    tiles_per_batch = rows_per_batch // tc
    tile = pl.BlockSpec((tc, d), lambda i: (i, 0))
    mod_spec = tile if per_token_mod else pl.BlockSpec((None, 1, d), lambda i: (i // tiles_per_batch, 0, 0))
    full = lambda a: pl.BlockSpec(a.shape, lambda i: (0,) * a.ndim)
    ws = [w["w_sh_gate"], w["w_sh_up"], w["w_sh_down"], w["ln2_g"], w["ln2_b"]]
    return pl.pallas_call(
        functools.partial(_final_kernel, alpha=alpha),
        grid=(n // tc,),
        in_specs=[pl.BlockSpec((k, tc), lambda i: (0, i), memory_space=pltpu.SMEM), tile, tile,
                  pl.BlockSpec((k, tc), lambda i: (0, i)), mod_spec] + [full(a) for a in ws]
        + [pl.BlockSpec(memory_space=pl.ANY)],
        out_specs=tile,
        out_shape=jax.ShapeDtypeStruct((n, d), F32),
        scratch_shapes=[pltpu.VMEM((k, tc, d), F32), pltpu.SemaphoreType.DMA],
        compiler_params=_cparams(("arbitrary",), 40),
        name="final",
    )(slot, x1, h, top_w, g2, *ws, y_sorted)


def _prep_weights(lp):
    (w_ada, b_ada, w_in, cmp_k_w1, cmp_k_w2, cmp_k_pe, cmp_v_w1, cmp_v_w2, cmp_v_pe, w_nsa_out, w_moba_out, w_o,
     ln1_g, ln1_b, w_router, b_router, w_exp_gate, w_exp_up, w_exp_down, w_sh_gate, w_sh_up, w_sh_down,
     ln2_g, ln2_b) = lp
    c = lambda a: a.astype(MXU_DTYPE)
    row = lambda a: a.reshape(1, -1)
    w_r, w_mg = _reorder_w_in(w_in)
    wk, w2k, pek = _cmp_weights(cmp_k_w1, cmp_k_w2, cmp_k_pe)
    wv, w2v, pev = _cmp_weights(cmp_v_w1, cmp_v_w2, cmp_v_pe)
    return dict(w_ada=w_ada, b_ada=b_ada, w_r=w_r, w_mg=w_mg, cmp=(wk, w2k, pek, wv, w2v, pev),
                w_nsa_out=c(w_nsa_out), w_moba_out=c(w_moba_out), w_o=c(w_o), ln1_g=row(ln1_g), ln1_b=row(ln1_b),
                w_router_t=c(w_router.T), b_router=b_router.reshape(-1, 1),
                w_exp_gate=c(w_exp_gate), w_exp_up=c(w_exp_up), w_exp_down=c(w_exp_down),
                w_sh_gate=c(w_sh_gate), w_sh_up=c(w_sh_up), w_sh_down=c(w_sh_down), ln2_g=row(ln2_g), ln2_b=row(ln2_b))


def _token_tail(x, o_a, o_b, mods, w, *, tm, per_token_mod, rows_per_batch, alpha):
    sc1, sh1, g1, sh2, sc2, g2 = mods
    kw = dict(tm=tm, per_token_mod=per_token_mod, rows_per_batch=rows_per_batch, alpha=alpha)
    x1, h, top_i, top_w = _merge(x, o_a, o_b, (sc1, sh1, g1, sc2, sh2), w, **kw)
    y_sorted, slot = _routed_experts(h, top_i, w)
    return _final(x1, h, y_sorted, slot, top_w, g2, w, per_token_mod=per_token_mod, rows_per_batch=rows_per_batch,
                  alpha=alpha)


def _layer(xp, xs, c_all, cache_nsa_l, cache_moba_l, win_state_l, page_table, w, alpha):
    b, t, d = xp.shape
    db, ts, _ = xs.shape
    assert ts == 1
    page = cache_nsa_l.shape[1]
    pos = page_table.shape[1] * page
    assert win_state_l.shape[1] == WINDOW and t >= WINDOW
    mod = _ada(c_all, w["w_ada"], w["b_ada"])
    pieces = [mod[:, i * d:(i + 1) * d] for i in range(6)]
    mods_p = [m[:b].reshape(b, 1, d) for m in pieces]
    mods_s = [m[b:b + db] for m in pieces]
    order = lambda m: (m[1], m[0], m[2], m[3], m[4], m[5])
    mods_p, mods_s = order(mods_p), order(mods_s)

    tm = TOKEN_TILE
    xp2 = xp.reshape(b * t, d)
    pp = _inproj(xp2, mods_p[0], mods_p[1], w["w_r"], _rope_tables(jnp.arange(t)), tm=tm, per_token_mod=False,
                 rows_per_batch=t, with_kmean=True)
    kcv_p, vct_p = _cmp_prompt(pp["nsa"].reshape(b, t, NSA_ROW), w["cmp"])
    oa_p = _nsa_prompt(pp, kcv_p, vct_p, b, t)
    ob_p = _moba_prompt(pp, b, t)
    yp = _token_tail(xp2, oa_p, ob_p, mods_p, w, tm=tm, per_token_mod=False, rows_per_batch=t, alpha=alpha)

    xs2 = xs.reshape(db, d)
    ps = _inproj(xs2, mods_s[0], mods_s[1], w["w_r"], _rope_tables(jnp.full((db,), pos)), tm=db, per_token_mod=True,
                 rows_per_batch=db, with_kmean=False)
    token_minor = lambda a: jnp.transpose(a, (0, 2, 3, 4, 1))
    kcv_s, _ = _cmp_decode(token_minor(cache_nsa_l), page_table, w["cmp"])
    oa_s = _nsa_decode(ps, kcv_s, token_minor(cache_nsa_l), token_minor(win_state_l), page_table, pos)
    ob_s = _moba_decode(ps, token_minor(cache_moba_l), page_table, pos)
    ys = _token_tail(xs2, oa_s, ob_s, mods_s, w, tm=db, per_token_mod=True, rows_per_batch=db, alpha=alpha)

    g, hd = NSA_KV_HEADS, HEAD_DIM
    win_p = pp["win"].reshape(b, t, 2, g, hd)[:, t - WINDOW:]
    win_s = jnp.concatenate([win_state_l[:, 1:], ps["win"].reshape(db, 1, 2, g, hd)], axis=1)
    return (yp.reshape(b, t, d), ys.reshape(db, 1, d),
            pp["nsa"].reshape(b, t, 4, g, hd), ps["nsa"].reshape(db, 1, 4, g, hd),
            pp["moba"].reshape(b, t, 2, MOBA_HEADS, hd), ps["moba"].reshape(db, 1, 2, MOBA_HEADS, hd), win_p, win_s)


def kernel(x_prompt, x_sample, cache_nsa, cache_moba, state_nsa_win, page_table, c_prompt, c_sample, w_ada, b_ada,
           w_in, cmp_k_w1, cmp_k_w2, cmp_k_pe, cmp_v_w1, cmp_v_w2, cmp_v_pe, w_nsa_out, w_moba_out, w_o, ln1_g,
           ln1_b, w_router, b_router, w_exp_gate, w_exp_up, w_exp_down, w_sh_gate, w_sh_up, w_sh_down, ln2_g,
           ln2_b):
    params = (w_ada, b_ada, w_in, cmp_k_w1, cmp_k_w2, cmp_k_pe, cmp_v_w1, cmp_v_w2, cmp_v_pe, w_nsa_out, w_moba_out,
              w_o, ln1_g, ln1_b, w_router, b_router, w_exp_gate, w_exp_up, w_exp_down, w_sh_gate, w_sh_up,
              w_sh_down, ln2_g, ln2_b)
    depth = w_ada.shape[0]
    alpha = (2 * depth) ** 0.25
    b, db = x_prompt.shape[0], x_sample.shape[0]
    rows = -(-(b + db) // 8) * 8
    c_all = jnp.pad(jnp.concatenate([c_prompt, c_sample], axis=0), ((0, rows - b - db), (0, 0)))
    xp, xs = x_prompt, x_sample
    outs = [[] for _ in range(6)]
    for l in range(depth):
        w = _prep_weights([p[l] for p in params])
        res = _layer(xp, xs, c_all, cache_nsa[l], cache_moba[l], state_nsa_win[l], page_table, w, alpha)
        xp, xs = res[0], res[1]
        for acc, r in zip(outs, res[2:]):
            acc.append(r)
    return (xp, xs) + tuple(jnp.stack(o) for o in outs)
```

```python
import functools

import jax
import jax.numpy as jnp
from jax import lax
from jax.experimental import pallas as pl
from jax.experimental.pallas import tpu as pltpu

F32 = jnp.float32
BF16 = jnp.bfloat16
I32 = jnp.int32
MXU_DTYPE = jnp.bfloat16

HEAD_DIM = 64
ROPE_DIMS = HEAD_DIM // 4
ROPE_HALF = ROPE_DIMS // 2
ROPE_THETA = 500000.0
NSA_HEADS = 8
NSA_KV_HEADS = 2
NSA_GROUP = NSA_HEADS // NSA_KV_HEADS
CMP_BLOCK = 32
CMP_STRIDE = 16
CMP_RATIO = CMP_BLOCK // CMP_STRIDE
SEL_BLOCK = 64
SEL_TOPN = 16
WINDOW = 512
MOBA_HEADS = 8
MOBA_BLOCK = 256
MOBA_TOPK = 3
N_EXPERTS = 64
TOP_K = 8
N_GROUPS = 8
TOPK_GROUPS = 4
ROUTED_SCALE = 2.5
LN_EPS = 1e-5

LANES = 128
NSA_Q = NSA_HEADS * HEAD_DIM
NSA_KV = NSA_KV_HEADS * HEAD_DIM
MOBA_W = MOBA_HEADS * HEAD_DIM
NSA_ROW = 4 * NSA_KV
MOBA_ROW = 2 * MOBA_W
GATE_PAD = LANES

_QK_SCALE = HEAD_DIM ** -0.5
assert _QK_SCALE == 0.125
NEG = -1e30
BIG = 3e38
TOKEN_TILE = 256
NSA_QB = 256
EXPERT_BLK = 512
RANK_TILE = 512
ROW_COPY_TILE = 256
MOBA_HEADS_PER_STEP = 8
PAGES_PER_STEP = 16


def _sigmoid(x):
    return 1.0 / (1.0 + jnp.exp(-x))


def _silu(x):
    return x * _sigmoid(x)


def _mm(a, b):
    return jnp.dot(a.astype(MXU_DTYPE), b.astype(MXU_DTYPE), preferred_element_type=F32)


def _mm_nt(a, b):
    return lax.dot_general(a.astype(MXU_DTYPE), b.astype(MXU_DTYPE), (((1,), (1,)), ((), ())),
                           preferred_element_type=F32)


def _iota(shape, axis):
    return lax.broadcasted_iota(I32, shape, axis)


def _cparams(sem, vmem_mb=None):
    kw = dict(dimension_semantics=sem)
    if vmem_mb is not None:
        kw["vmem_limit_bytes"] = vmem_mb << 20
    return pltpu.CompilerParams(**kw)


def _masked_softmax(s, mask, axis=-1):
    m = jnp.max(jnp.where(mask, s, NEG), axis=axis, keepdims=True)
    e = jnp.where(mask, jnp.exp(s - m), 0.0)
    d = jnp.sum(e, axis=axis, keepdims=True)
    return e / jnp.where(d > 0, d, 1.0)


def _layer_norm(z, g, b):
    mu = jnp.mean(z, axis=-1, keepdims=True)
    zc = z - mu
    var = jnp.mean(zc * zc, axis=-1, keepdims=True)
    return zc * lax.rsqrt(var + LN_EPS) * g + b


def _take_first_max(score, idx_f, n_f, axis):
    m = jnp.max(score, axis=axis, keepdims=True)
    first = jnp.min(jnp.where(score == m, idx_f, n_f), axis=axis, keepdims=True)
    return m, first, idx_f == first


def _ada_kernel(c_ref, w_ref, b_ref, o_ref):
    a = _silu(c_ref[...])
    o_ref[...] = _mm(a, w_ref[...]) + b_ref[...]


def _ada(c_all, w_ada, b_ada):
    r, d = c_all.shape
    e6 = w_ada.shape[1]
    tn = 1024
    return pl.pallas_call(
        _ada_kernel,
        grid=(e6 // tn,),
        in_specs=[pl.BlockSpec((r, d), lambda j: (0, 0)),
                  pl.BlockSpec((d, tn), lambda j: (0, j)),
                  pl.BlockSpec((1, tn), lambda j: (0, j))],
        out_specs=pl.BlockSpec((r, tn), lambda j: (0, j)),
        out_shape=jax.ShapeDtypeStruct((r, e6), F32),
        compiler_params=_cparams(("arbitrary",)),
        name="ada",
    )(c_all, w_ada, b_ada.reshape(1, e6))


_C_QA = 0
_C_KVA = _C_QA + NSA_Q
_C_QB = _C_KVA + 6 * NSA_KV
_C_KB = _C_QB + MOBA_W
_C_VB = _C_KB + MOBA_W
_C_GATE = _C_VB + MOBA_W
_C_END = _C_GATE + GATE_PAD


def _rope(x, cs, s1, s2):
    parts = []
    for j in range(x.shape[1] // LANES):
        xj = x[:, j * LANES:(j + 1) * LANES]
        parts.append(xj * cs + pltpu.roll(xj, ROPE_HALF, 1) * s1 + pltpu.roll(xj, LANES - ROPE_HALF, 1) * s2)
    return parts[0] if len(parts) == 1 else jnp.concatenate(parts, axis=1)


_TOKEN_MAJOR = dict(qu=(NSA_Q, BF16), qr=(NSA_Q, BF16), qb=(MOBA_W, BF16), nsa=(NSA_ROW, F32), kcvc=(2 * NSA_KV, F32),
                    katt=(4 * NSA_KV, BF16), win=(2 * NSA_KV, F32), moba=(MOBA_ROW, F32), kvb=(MOBA_ROW, BF16),
                    gate=(GATE_PAD, F32))
_PER_TILE_T = dict(qbt=(MOBA_W, BF16), vbt=(MOBA_W, BF16), qut=(NSA_Q, BF16), qrt=(NSA_Q, BF16), gt=(GATE_PAD, F32),
                   vst=(NSA_KV, BF16), vwt=(NSA_KV, BF16))
_PER_BATCH_T = dict(nsat=(NSA_ROW, F32), wint=(2 * NSA_KV, F32), mobat=(MOBA_ROW, F32))
_DECODE_OUTS = ("qu", "qr", "qb", "nsa", "katt", "win", "moba", "kvb", "gate")
_PROMPT_OUTS = ("kcvc", "katt", "kvb", "kmean") + tuple(_PER_TILE_T) + tuple(_PER_BATCH_T)


def _inproj_kernel(x_ref, sc_ref, sh_ref, w_ref, cs_ref, s1_ref, s2_ref, *out_refs, names):
    u = x_ref[...] * (1.0 + sc_ref[...]) + sh_ref[...]
    r = _mm(u, w_ref[...])
    cs, s1, s2 = cs_ref[...], s1_ref[...], s2_ref[...]
    rope = lambda v: _rope(v, cs, s1, s2)
    qa = r[:, _C_QA:_C_KVA]
    qa_rot = rope(qa)
    o = _C_KVA
    kc_vc = r[:, o:o + 2 * NSA_KV]
    ks = rope(r[:, o + 2 * NSA_KV:o + 3 * NSA_KV])
    vs = r[:, o + 3 * NSA_KV:o + 4 * NSA_KV]
    kw = rope(r[:, o + 4 * NSA_KV:o + 5 * NSA_KV])
    vw = r[:, o + 5 * NSA_KV:o + 6 * NSA_KV]
    qb = rope(r[:, _C_QB:_C_KB])
    kb = rope(r[:, _C_KB:_C_VB])
    vb = r[:, _C_VB:_C_GATE]
    gate = r[:, _C_GATE:_C_END]
    cat = lambda *v: jnp.concatenate(v, axis=1)
    pieces = dict(qu=lambda: qa, qr=lambda: qa_rot, qb=lambda: qb, nsa=lambda: cat(kc_vc, ks, vs),
                  kcvc=lambda: kc_vc, katt=lambda: cat(ks, vs, kw, vw), win=lambda: cat(kw, vw),
                  moba=lambda: cat(kb, vb), kvb=lambda: cat(kb, vb), gate=lambda: gate,
                  kmean=lambda: jnp.sum(kb, axis=0, keepdims=True) * (1.0 / kb.shape[0]),
                  qbt=lambda: qb.T, vbt=lambda: vb.T, qut=lambda: qa.T, qrt=lambda: qa_rot.T, gt=lambda: gate.T,
                  vst=lambda: vs.T, vwt=lambda: vw.T, nsat=lambda: cat(kc_vc, ks, vs).T, wint=lambda: cat(kw, vw).T,
                  mobat=lambda: cat(kb, vb).T)
    for name, ref in zip(names, out_refs):
        ref[...] = pieces[name]().astype(ref.dtype)


def _inproj(x, sc, sh, w_r, rope_tabs, *, tm, per_token_mod, rows_per_batch, prompt):
    n, d = x.shape
    nt = n // tm
    tiles_per_batch = rows_per_batch // tm
    row = lambda i: (i, 0)
    if per_token_mod:
        mod_spec = pl.BlockSpec((tm, d), row)
    else:
        mod_spec = pl.BlockSpec((None, 1, d), lambda i: (i // tiles_per_batch, 0, 0))
    tab_spec = pl.BlockSpec((tm, LANES), lambda i: (i % tiles_per_batch, 0))
    names = _PROMPT_OUTS if prompt else _DECODE_OUTS
    out_shape, out_specs = [], []
    for name in names:
        if name in _TOKEN_MAJOR:
            w, dt = _TOKEN_MAJOR[name]
            out_shape.append(jax.ShapeDtypeStruct((n, w), dt))
            out_specs.append(pl.BlockSpec((tm, w), row))
        elif name in _PER_TILE_T:
            w, dt = _PER_TILE_T[name]
            out_shape.append(jax.ShapeDtypeStruct((nt, w, tm), dt))
            out_specs.append(pl.BlockSpec((None, w, tm), lambda i: (i, 0, 0)))
        elif name in _PER_BATCH_T:
            w, dt = _PER_BATCH_T[name]
            out_shape.append(jax.ShapeDtypeStruct((n // rows_per_batch, w, rows_per_batch), dt))
            out_specs.append(pl.BlockSpec((None, w, tm), lambda i: (i // tiles_per_batch, 0, i % tiles_per_batch)))
        else:
            assert name == "kmean" and tm == MOBA_BLOCK
            out_shape.append(jax.ShapeDtypeStruct((nt, 1, MOBA_W), F32))
            out_specs.append(pl.BlockSpec((None, 1, MOBA_W), lambda i: (i, 0, 0)))
    res = pl.pallas_call(
        functools.partial(_inproj_kernel, names=names),
        grid=(nt,),
        in_specs=[pl.BlockSpec((tm, d), row), mod_spec, mod_spec,
                  pl.BlockSpec(w_r.shape, lambda i: (0, 0)), tab_spec, tab_spec, tab_spec],
        out_specs=out_specs,
        out_shape=out_shape,
        compiler_params=_cparams(("parallel",), 48),
        name="inproj",
    )(x, sc, sh, w_r, *rope_tabs)
    return dict(zip(names, res))


def _rope_tables(pos):
    inv = ROPE_THETA ** (-jnp.arange(ROPE_HALF, dtype=F32) / ROPE_HALF)
    ang = pos.astype(F32)[:, None] * inv
    cos, sin = jnp.cos(ang), jnp.sin(ang)
    rows = pos.shape[0]
    one = jnp.ones((rows, HEAD_DIM - ROPE_DIMS), F32)
    zero = jnp.zeros((rows, HEAD_DIM - ROPE_DIMS), F32)
    zh = jnp.zeros((rows, ROPE_HALF), F32)
    cs = jnp.concatenate([cos, cos, one], axis=1)
    s1 = jnp.concatenate([zh, sin, zero], axis=1)
    s2 = jnp.concatenate([-sin, zh, zero], axis=1)
    rep = LANES // HEAD_DIM
    return tuple(jnp.tile(t, (1, rep)) for t in (cs, s1, s2))


def _reorder_w_in(w_in):
    d = w_in.shape[0]
    sizes = (NSA_Q, 6 * NSA_KV, 3 * NSA_HEADS, MOBA_W, MOBA_W, MOBA_W, d, d)
    offs = [0]
    for s in sizes:
        offs.append(offs[-1] + s)
    q_a, kv_a, gate, q_b, k_b, v_b, mg_a, mg_b = [w_in[:, offs[i]:offs[i + 1]] for i in range(8)]
    gate = jnp.pad(gate, ((0, 0), (0, GATE_PAD - 3 * NSA_HEADS)))
    w_r = jnp.concatenate([q_a, kv_a, q_b, k_b, v_b, gate], axis=1).astype(MXU_DTYPE)
    w_mg = jnp.concatenate([mg_a, mg_b], axis=1).astype(MXU_DTYPE)
    return w_r, w_mg


def _cmp1_kernel(x_ref, wk_ref, wv_ref, pek_ref, pev_ref, o_ref):
    def gather(off):
        return jnp.concatenate([x_ref[:, s * 2 * NSA_KV + off:s * 2 * NSA_KV + off + NSA_KV]
                                for s in range(CMP_STRIDE)], axis=1)

    xk = gather(0)
    xv = gather(NSA_KV)
    outs = []
    for r in range(CMP_RATIO):
        outs.append(_mm(xk + pek_ref[r], wk_ref[r]))
        outs.append(_mm(xv + pev_ref[r], wv_ref[r]))
    o_ref[...] = jnp.concatenate(outs, axis=1)


def _cmp2_kernel(a_ref, w2k_ref, w2v_ref, o_ref, vt_ref):
    a = a_ref[...]
    n = a.shape[0]
    w = NSA_KV
    hk = a[:, 0:w] + pltpu.roll(a[:, 2 * w:3 * w], n - 1, 0)
    hv = a[:, w:2 * w] + pltpu.roll(a[:, 3 * w:4 * w], n - 1, 0)
    vc = _mm(_silu(hv), w2v_ref[...])
    o_ref[...] = jnp.concatenate([_mm(_silu(hk), w2k_ref[...]), vc], axis=1)
    vt_ref[...] = jnp.transpose(vc)


def _blockdiag(w, reps):
    k, n = w.shape
    out = jnp.zeros((reps * k, reps * n), w.dtype)
    for g in range(reps):
        out = out.at[g * k:(g + 1) * k, g * n:(g + 1) * n].set(w)
    return out


def _cmp_weights(w1, w2, pe):
    g = NSA_KV_HEADS
    w1s = jnp.stack([jnp.concatenate([_blockdiag(w1[r, s], g) for s in range(CMP_STRIDE)], axis=0)
                     for r in range(CMP_RATIO)]).astype(MXU_DTYPE)
    pes = jnp.tile(pe[:, :, None, :], (1, 1, g, 1)).reshape(CMP_RATIO, 1, CMP_STRIDE * NSA_KV)
    return w1s, _blockdiag(w2, g).astype(MXU_DTYPE), pes


def _cmp_stage2(a, w2k, w2v):
    b, nch, _ = a.shape
    return pl.pallas_call(
        _cmp2_kernel,
        grid=(b,),
        in_specs=[pl.BlockSpec((None, nch, 4 * NSA_KV), lambda i: (i, 0, 0)),
                  pl.BlockSpec(w2k.shape, lambda i: (0, 0)), pl.BlockSpec(w2v.shape, lambda i: (0, 0))],
        out_specs=[pl.BlockSpec((None, nch, 2 * NSA_KV), lambda i: (i, 0, 0)),
                   pl.BlockSpec((None, NSA_KV, nch), lambda i: (i, 0, 0))],
        out_shape=[jax.ShapeDtypeStruct((b, nch, 2 * NSA_KV), F32), jax.ShapeDtypeStruct((b, NSA_KV, nch), F32)],
        compiler_params=_cparams(("parallel",)),
        name="cmp2",
    )(a, w2k, w2v)


def _cmp_prompt(kcvc, cw):
    b, t, roww = kcvc.shape
    nch = t // CMP_STRIDE
    tc = min(128, nch)
    x = kcvc.reshape(b, nch, CMP_STRIDE * roww)
    wk, w2k, pek, wv, w2v, pev = cw
    full = lambda a: pl.BlockSpec(a.shape, lambda i, j: (0,) * a.ndim)
    a = pl.pallas_call(
        _cmp1_kernel,
        grid=(b, nch // tc),
        in_specs=[pl.BlockSpec((None, tc, CMP_STRIDE * roww), lambda i, j: (i, j, 0)),
                  full(wk), full(wv), full(pek), full(pev)],
        out_specs=pl.BlockSpec((None, tc, 4 * NSA_KV), lambda i, j: (i, j, 0)),
        out_shape=jax.ShapeDtypeStruct((b, nch, 4 * NSA_KV), F32),
        compiler_params=_cparams(("parallel", "parallel"), 48),
        name="cmp1_prompt",
    )(x, wk, wv, pek, pev)
    return _cmp_stage2(a, w2k, w2v)


def _cmp1_paged_kernel(*refs, n_pages):
    page_refs = refs[1:1 + 2 * n_pages]
    wk_ref, wv_ref, pek_ref, pev_ref, o_ref, xk_scr, xv_scr = refs[1 + 2 * n_pages:]
    page = page_refs[0].shape[-1]
    for p in range(n_pages):
        for part, scr in ((0, xk_scr), (1, xv_scr)):
            scr[p * page:(p + 1) * page, :] = jnp.transpose(page_refs[2 * p + part][...].reshape(NSA_KV, page))
    n_chunk = n_pages * page // CMP_STRIDE

    def gather(scr):
        return jnp.concatenate([scr[pl.ds(s, n_chunk, stride=CMP_STRIDE), :] for s in range(CMP_STRIDE)], axis=1)

    xk, xv = gather(xk_scr), gather(xv_scr)
    outs = []
    for r in range(CMP_RATIO):
        outs.append(_mm(xk + pek_ref[r], wk_ref[r]))
        outs.append(_mm(xv + pev_ref[r], wv_ref[r]))
    o_ref[...] = jnp.concatenate(outs, axis=1)


def _cmp_decode(cache_t, page_table, cw):
    page = cache_t.shape[-1]
    db, n_pages = page_table.shape
    cpp = page // CMP_STRIDE
    pps = PAGES_PER_STEP
    wk, w2k, pek, wv, w2v, pev = cw
    nch = n_pages * cpp
    full = lambda a: pl.BlockSpec(a.shape, lambda i, j, pt: (0,) * a.ndim)
    k_cmp, v_cmp = 0, 1
    page_specs = [pl.BlockSpec((None, None, NSA_KV_HEADS, HEAD_DIM, page), functools.partial(
        lambda i, j, pt, p, part: (pt[i, j * pps + p], part, 0, 0, 0), p=p, part=part))
        for p in range(pps) for part in (k_cmp, v_cmp)]
    a = pl.pallas_call(
        functools.partial(_cmp1_paged_kernel, n_pages=pps),
        grid_spec=pltpu.PrefetchScalarGridSpec(
            num_scalar_prefetch=1,
            grid=(db, n_pages // pps),
            in_specs=page_specs + [full(wk), full(wv), full(pek), full(pev)],
            out_specs=pl.BlockSpec((None, pps * cpp, 4 * NSA_KV), lambda i, j, pt: (i, j, 0)),
            scratch_shapes=[pltpu.VMEM((pps * page, NSA_KV), F32), pltpu.VMEM((pps * page, NSA_KV), F32)],
        ),
        out_shape=jax.ShapeDtypeStruct((db, nch, 4 * NSA_KV), F32),
        compiler_params=_cparams(("parallel", "parallel"), 48),
        name="cmp1_decode",
    )(page_table, *([cache_t] * len(page_specs)), wk, wv, pek, pev)
    return _cmp_stage2(a, w2k, w2v)


def _cmp_to_sel(nch, ns_pad):
    cs = jnp.arange(nch) * CMP_STRIDE
    ss = jnp.arange(ns_pad) * SEL_BLOCK
    return ((cs[:, None] < ss[None] + SEL_BLOCK) & (cs[:, None] + CMP_BLOCK > ss[None])).astype(F32)


def _nsa_prompt_kernel(qut_ref, qrt_ref, gt_ref, kc_ref, vct_ref, c2st_ref, ks_ref, kw_ref, vst_ref, vwt_ref, o_ref, *,
                       qb, kc):
    s0 = pl.program_id(1) * qb
    nkc = vst_ref.shape[0]
    nch = kc_ref.shape[0]
    ns = c2st_ref.shape[0]
    n_sel = min(SEL_TOPN, ns)
    hg, hd, groups = NSA_GROUP, HEAD_DIM, range(NSA_KV_HEADS)
    sel_shift = SEL_BLOCK.bit_length() - 1
    pos_q = s0 + _iota((1, qb), 1)
    pos_l = jnp.concatenate([pos_q] * hg, axis=1)
    gates = _sigmoid(gt_ref[...])
    zeros = jnp.zeros((hd, hg * qb), F32)

    def group_q(ref, g):
        x = jnp.concatenate([ref[(g * hg + h) * hd:(g * hg + h + 1) * hd, :] for h in range(hg)], axis=1)
        parts = [zeros] * NSA_KV_HEADS
        parts[g] = x.astype(F32) * _QK_SCALE
        return jnp.concatenate(parts, axis=0).astype(BF16)

    qzu = [group_q(qut_ref, g) for g in groups]
    qzr = [group_q(qrt_ref, g) for g in groups]
    cend = _iota((nch, 1), 0) * CMP_STRIDE + (CMP_BLOCK - 1)
    kcb = kc_ref[...]
    p_c = [_masked_softmax(_mm(kcb, qzu[g]), cend <= pos_l, axis=0) for g in groups]
    o_c = [_mm(vct_ref[g * hd:(g + 1) * hd, :], p_c[g]) for g in groups]
    jsel = _iota((ns, 1), 0)
    jsel_f = jsel.astype(F32)
    jq = lax.shift_right_logical(pos_q, sel_shift)
    forced = (jsel == 0) | (jsel == jq) | (jsel == jq - 1)
    sel_b = []
    for g in groups:
        p_sum = p_c[g][:, 0:qb]
        for h in range(1, hg):
            p_sum = p_sum + p_c[g][:, h * qb:(h + 1) * qb]
        score = jnp.where(jsel <= jq, jnp.where(forced, BIG, _mm(c2st_ref[...], p_sum)), -1.0)
        sel = jnp.zeros((ns, qb), F32)
        for _ in range(n_sel):
            m, _, pick = _take_first_max(score, jsel_f, float(ns), 0)
            sel = jnp.where(pick & (m >= 0.0), 1.0, sel)
            score = jnp.where(pick, -2.0, score)
        sel_b.append(sel.astype(BF16))
    blk_lane = _iota((1, ns), 1)

    def sel_step(c, carry):
        k0 = pl.multiple_of(c * kc, kc)
        kpos = k0 + _iota((kc, 1), 0)
        expand = (lax.shift_right_logical(kpos, sel_shift) == blk_lane).astype(BF16)
        causal = kpos <= pos_q
        kk = ks_ref[pl.ds(k0, kc), :]
        s = [_mm(kk, qzr[g]) for g in groups]
        bias = [jnp.where(causal & (jnp.dot(expand, sel_b[g], preferred_element_type=F32) > 0.5), 0.0, NEG)
                for g in groups]
        s = [s[g] + jnp.concatenate([bias[g]] * hg, axis=1) for g in groups]
        m_new = [jnp.maximum(carry[3 * g], jnp.max(s[g], axis=0, keepdims=True)) for g in groups]
        p = [jnp.exp(s[g] - jnp.maximum(m_new[g], 0.5 * NEG)) for g in groups]
        pv = [_mm(vst_ref[c, g * hd:(g + 1) * hd, :], p[g]) for g in groups]
        out = []
        for g in groups:
            m_i, l_i, acc = carry[3 * g:3 * g + 3]
            alpha = jnp.exp(m_i - m_new[g])
            out += [m_new[g], alpha * l_i + jnp.sum(p[g], axis=0, keepdims=True), alpha * acc + pv[g]]
        return tuple(out)

    init = (jnp.full((1, hg * qb), NEG, F32), jnp.zeros((1, hg * qb), F32), zeros) * NSA_KV_HEADS
    fin = lax.fori_loop(0, s0 // kc + 1, sel_step, init)
    o_s = [fin[3 * g + 2] / jnp.where(fin[3 * g + 1] > 0, fin[3 * g + 1], 1.0) for g in groups]
    nwc = WINDOW // kc + 1
    c_lo = jnp.minimum(jnp.maximum(s0 - WINDOW, 0) // kc, nkc - nwc)
    wstart = pl.multiple_of(c_lo * kc, kc)
    wpos = wstart + _iota((nwc * kc, 1), 0)
    mask_w = (wpos <= pos_l) & (wpos > pos_l - WINDOW)
    kwb = kw_ref[pl.ds(wstart, nwc * kc), :]
    p_w = [_masked_softmax(_mm(kwb, qzr[g]), mask_w, axis=0) for g in groups]
    o_w = []
    for g in groups:
        acc = _mm(vwt_ref[c_lo, g * hd:(g + 1) * hd, :], p_w[g][0:kc])
        for i in range(1, nwc):
            acc = acc + _mm(vwt_ref[c_lo + i, g * hd:(g + 1) * hd, :], p_w[g][i * kc:(i + 1) * kc])
        o_w.append(acc)
    rows = []
    for g in groups:
        for h in range(hg):
            hh, lanes = g * hg + h, slice(h * qb, (h + 1) * qb)
            rows.append(gates[3 * hh:3 * hh + 1] * o_c[g][:, lanes] + gates[3 * hh + 1:3 * hh + 2] * o_s[g][:, lanes]
                        + gates[3 * hh + 2:3 * hh + 3] * o_w[g][:, lanes])
    o_ref[...] = jnp.transpose(jnp.concatenate(rows, axis=0)).astype(o_ref.dtype)


def _nsa_prompt(p, kcv, vct, b, t):
    qb, kc = NSA_QB, TOKEN_TILE
    nwc = WINDOW // kc + 1
    assert t % kc == 0 and kc % qb == 0 and t >= nwc * kc and t % SEL_BLOCK == 0 and WINDOW % kc == 0
    nkc = t // kc
    nch = kcv.shape[1]
    ns = t // SEL_BLOCK
    c2st = _cmp_to_sel(nch, ns).T.astype(BF16)
    katt = p["katt"].reshape(b, t, 4 * NSA_KV)
    qsub = kc // qb
    q_spec = lambda w: pl.BlockSpec((None, w, qb), lambda i, j: (i * nkc + j // qsub, 0, j % qsub))
    chunks = lambda a: a.reshape(b, nkc, NSA_KV, kc)
    chunk_spec = pl.BlockSpec((None, nkc, NSA_KV, kc), lambda i, j: (i, 0, 0, 0))
    return pl.pallas_call(
        functools.partial(_nsa_prompt_kernel, qb=qb, kc=kc),
        grid=(b, t // qb),
        in_specs=[q_spec(NSA_Q), q_spec(NSA_Q), q_spec(GATE_PAD),
                  pl.BlockSpec((None, nch, NSA_KV), lambda i, j: (i, 0, 0)),
                  pl.BlockSpec((None, NSA_KV, nch), lambda i, j: (i, 0, 0)),
                  pl.BlockSpec((ns, nch), lambda i, j: (0, 0)),
                  pl.BlockSpec((None, t, NSA_KV), lambda i, j: (i, 0, 0)),
                  pl.BlockSpec((None, t, NSA_KV), lambda i, j: (i, 0, 2)),
                  chunk_spec, chunk_spec],
        out_specs=pl.BlockSpec((None, qb, NSA_Q), lambda i, j: (i, j, 0)),
        out_shape=jax.ShapeDtypeStruct((b, t, NSA_Q), BF16),
        compiler_params=_cparams(("parallel", "arbitrary"), 48),
        name="nsa_prompt",
    )(p["qut"], p["qrt"], p["gt"], kcv, vct, c2st, katt, katt, chunks(p["vst"]), chunks(p["vwt"])).reshape(b * t, NSA_Q)


def _moba_prompt_kernel(qt_ref, km_ref, k_ref, vt_ref, o_ref, *, blk):
    jq = pl.program_id(2)
    nb = km_ref.shape[0]
    n_top = min(MOBA_TOPK, nb)
    hpl = LANES // HEAD_DIM
    n_heads = qt_ref.shape[0] // HEAD_DIM
    jb = _iota((nb, 1), 0)
    jb_f = jb.astype(F32)
    row_head = lax.shift_right_logical(_iota((LANES, 1), 0), HEAD_DIM.bit_length() - 1)
    tile = lambda hh: slice(hh // hpl * LANES, (hh // hpl + 1) * LANES)
    qz, sels = [], []
    for hh in range(n_heads):
        qt = qt_ref[tile(hh), :].astype(F32) * _QK_SCALE
        qz.append(jnp.where(row_head == hh % hpl, qt, 0.0).astype(BF16))
        score = jnp.where(jb < jq, _mm(km_ref[:, tile(hh)], qz[hh]), NEG)
        sel = jnp.zeros((nb, blk), F32)
        for _ in range(n_top):
            m, _, pick = _take_first_max(score, jb_f, float(nb), 0)
            sel = jnp.where(pick & (m > 0.5 * NEG), 1.0, sel)
            score = jnp.where(pick, 3.0 * NEG, score)
        sels.append(sel)

    def attend(j, carry, mask_fn):
        k0 = pl.multiple_of(j * blk, blk)
        heads = range(n_heads)
        s = [_mm(k_ref[pl.ds(k0, blk), tile(hh)], qz[hh]) for hh in heads]
        m_new, m_sub = zip(*[mask_fn(hh, s[hh], carry[3 * hh]) for hh in heads])
        p = [jnp.exp(s[hh] - m_sub[hh]) for hh in heads]
        pv = [_mm(vt_ref[j, hh * HEAD_DIM:(hh + 1) * HEAD_DIM, :], p[hh]) for hh in heads]
        out = []
        for hh in heads:
            m_i, l_i, acc = carry[3 * hh:3 * hh + 3]
            alpha = jnp.exp(m_i - m_new[hh])
            out += [m_new[hh], alpha * l_i + jnp.sum(p[hh], axis=0, keepdims=True), alpha * acc + pv[hh]]
        return tuple(out)

    def past_mask(j):
        def fn(hh, s, m_i):
            picked = jnp.sum(jnp.where(jb == j, sels[hh], 0.0), axis=0, keepdims=True) > 0.5
            m_new = jnp.maximum(m_i, jnp.where(picked, jnp.max(s, axis=0, keepdims=True), NEG))
            return m_new, jnp.where(picked, m_new, BIG)
        return fn

    init = (jnp.full((1, blk), NEG, F32), jnp.zeros((1, blk), F32), jnp.zeros((HEAD_DIM, blk), F32)) * n_heads
    carry = lax.fori_loop(0, jq, lambda j, c: attend(j, c, past_mask(j)), init)
    causal = _iota((blk, 1), 0) <= _iota((1, blk), 1)

    def own_mask(hh, s, m_i):
        m_new = jnp.maximum(m_i, jnp.max(jnp.where(causal, s, NEG), axis=0, keepdims=True))
        return m_new, jnp.where(causal, m_new, BIG)

    final = attend(jq, carry, own_mask)
    outs = [final[3 * hh + 2] / final[3 * hh + 1] for hh in range(n_heads)]
    o_ref[...] = jnp.transpose(jnp.concatenate(outs, axis=0)).astype(o_ref.dtype)


def _moba_prompt(p, b, t):
    blk = MOBA_BLOCK
    assert t % blk == 0
    nb = t // blk
    w = MOBA_HEADS_PER_STEP * HEAD_DIM
    nstep = MOBA_W // w
    qt = p["qbt"]
    vt = p["vbt"].reshape(b, nb, MOBA_W, blk)
    kv = p["kvb"].reshape(b, t, MOBA_ROW)
    km = p["kmean"].reshape(b, nb, MOBA_W)
    return pl.pallas_call(
        functools.partial(_moba_prompt_kernel, blk=blk),
        grid=(b, nstep, nb),
        in_specs=[pl.BlockSpec((None, w, blk), lambda i, hp, j: (i * nb + j, hp, 0)),
                  pl.BlockSpec((None, nb, w), lambda i, hp, j: (i, 0, hp)),
                  pl.BlockSpec((None, t, w), lambda i, hp, j: (i, 0, hp)),
                  pl.BlockSpec((None, nb, w, blk), lambda i, hp, j: (i, 0, hp, 0))],
        out_specs=pl.BlockSpec((None, blk, w), lambda i, hp, j: (i, j, hp)),
        out_shape=jax.ShapeDtypeStruct((b, t, MOBA_W), BF16),
        compiler_params=_cparams(("parallel", "parallel", "arbitrary"), 56),
        name="moba_prompt",
    )(qt, km, kv, vt).reshape(b * t, MOBA_W)


def _group_rows(n_rows=NSA_HEADS):
    return _iota((n_rows, 1), 0) < NSA_GROUP


def _nsa_dec_cmp_kernel(qu_ref, kcv_ref, c2s_ref, oc_ref, idx_ref, *, pos):
    assert NSA_KV_HEADS == 2
    q = qu_ref[...]
    nch = kcv_ref.shape[0]
    ns = c2s_ref.shape[1]
    n_sel = min(SEL_TOPN, -(-(pos + 1) // SEL_BLOCK))
    scale = HEAD_DIM ** -0.5
    g0 = _group_rows()
    hd = HEAD_DIM
    kcv = kcv_ref[...]
    s = jnp.where(g0, _mm_nt(q, kcv[:, 0:hd]), _mm_nt(q, kcv[:, hd:2 * hd])) * scale
    cend = _iota((1, nch), 1) * CMP_STRIDE + (CMP_BLOCK - 1)
    p = _masked_softmax(s, cend <= pos)
    oc_ref[...] = jnp.where(g0, _mm(p, kcv[:, NSA_KV:NSA_KV + hd]), _mm(p, kcv[:, NSA_KV + hd:NSA_KV + 2 * hd]))
    p0 = jnp.sum(jnp.where(g0, p, 0.0), axis=0, keepdims=True)
    p1 = jnp.sum(jnp.where(g0, 0.0, p), axis=0, keepdims=True)
    imp = _mm(jnp.where(g0, p0, p1), c2s_ref[...])
    jsel = _iota((1, ns), 1)
    jsel_f = jsel.astype(F32)
    jq = pos // SEL_BLOCK
    forced = (jsel == 0) | (jsel == jq) | (jsel == jq - 1)
    score = jnp.where(jsel <= jq, jnp.where(forced, BIG, imp), -1.0)
    lane = _iota((1, LANES), 1)
    idx = jnp.full((NSA_HEADS, LANES), -1, I32)
    for it in range(n_sel):
        m, first, pick = _take_first_max(score, jsel_f, float(ns), 1)
        idx = jnp.where(lane == it, jnp.where(m >= 0.0, first, -1.0).astype(I32), idx)
        score = jnp.where(pick, -2.0, score)
    idx_ref[...] = idx


def _attend_with_new(q, kts, vts, valids, k_new, v_new, new_valid, scale):
    s = [_mm(q, kt) * scale for kt in kts]
    qf = q.astype(MXU_DTYPE).astype(F32)
    s_new = jnp.sum(qf * k_new.astype(MXU_DTYPE).astype(F32), axis=1, keepdims=True) * scale
    m = jnp.where(new_valid, s_new, NEG)
    for sj, vj in zip(s, valids):
        m = jnp.maximum(m, jnp.max(jnp.where(vj > 0.5, sj, NEG), axis=1, keepdims=True))
    p = [jnp.where(vj > 0.5, jnp.exp(sj - m), 0.0) for sj, vj in zip(s, valids)]
    p_new = jnp.where(new_valid, jnp.exp(s_new - m), 0.0)
    d = p_new
    for pj in p:
        d = d + jnp.sum(pj, axis=1, keepdims=True)
    d = jnp.where(d > 0, d, 1.0)
    o = (p_new / d).astype(MXU_DTYPE).astype(F32) * v_new.astype(MXU_DTYPE).astype(F32)
    for pj, vt in zip(p, vts):
        o = o + _mm_nt(pj / d, vt)
    return o


def _nsa_dec_att_kernel(idx_ref, pt_ref, qr_ref, gate_ref, oc_ref, new_ref, win_ref, *rest, pos, n_sel, past_blocks):
    del pt_ref
    blk_refs, o_ref = rest[:-1], rest[-1]
    b = pl.program_id(0)
    q = qr_ref[...]
    scale = HEAD_DIM ** -0.5
    hd = HEAD_DIM
    g0 = _group_rows()
    new = new_ref[...]
    n_win = win_ref.shape[-1]
    wpos = pos - n_win + _iota((1, n_win), 1)
    valid_w = jnp.where((wpos > pos - WINDOW) & (wpos >= 0), 1.0, 0.0)
    page = blk_refs[0].shape[-1]
    bpp = page // SEL_BLOCK
    blk_of_lane = lax.shift_right_logical(_iota((1, page), 1), SEL_BLOCK.bit_length() - 1)
    o_s, o_w = [], []
    for g in range(NSA_KV_HEADS):
        glo = g * hd
        kts, vts, valids = [], [], []
        new_valid = jnp.zeros((1, 1), F32)
        for j in range(n_sel):
            bj = idx_ref[b, g, j]
            kts.append(blk_refs[2 * (g * n_sel + j)][...])
            vts.append(blk_refs[2 * (g * n_sel + j) + 1][...])
            in_past = jnp.where((bj >= 0) & (bj < past_blocks), 1.0, 0.0)
            valids.append(jnp.where(blk_of_lane == bj % bpp, in_past, 0.0))
            new_valid = jnp.maximum(new_valid, jnp.where(bj == past_blocks, 1.0, 0.0))
        o_s.append(_attend_with_new(q, kts, vts, valids, new[:, glo:glo + hd],
                                    new[:, NSA_KV + glo:NSA_KV + glo + hd], new_valid > 0.5, scale))
        o_w.append(_attend_with_new(q, [win_ref[0, g]], [win_ref[1, g]], [valid_w],
                                    new[:, 2 * NSA_KV + glo:2 * NSA_KV + glo + hd],
                                    new[:, 3 * NSA_KV + glo:3 * NSA_KV + glo + hd], True, scale))
    gs = _sigmoid(gate_ref[...])
    o_ref[...] = (gs[:, 0:1] * oc_ref[...] + gs[:, 1:2] * jnp.where(g0, o_s[0], o_s[1])
                  + gs[:, 2:3] * jnp.where(g0, o_w[0], o_w[1]))


def _nsa_decode_select(p, kcv, pos):
    db = kcv.shape[0]
    nch = kcv.shape[1]
    ns = pos // SEL_BLOCK + 1
    ns_pad = -(-ns // LANES) * LANES
    n_sel = min(SEL_TOPN, ns)
    c2s = _cmp_to_sel(nch, ns_pad).astype(BF16)
    per_q = lambda w: pl.BlockSpec((None, NSA_HEADS, w), lambda i: (i, 0, 0))
    o_c, idx = pl.pallas_call(
        functools.partial(_nsa_dec_cmp_kernel, pos=pos),
        grid=(db,),
        in_specs=[per_q(HEAD_DIM), pl.BlockSpec((None, nch, 2 * NSA_KV), lambda i: (i, 0, 0)),
                  pl.BlockSpec((nch, ns_pad), lambda i: (0, 0))],
        out_specs=[per_q(HEAD_DIM), per_q(LANES)],
        out_shape=[jax.ShapeDtypeStruct((db, NSA_HEADS, HEAD_DIM), F32),
                   jax.ShapeDtypeStruct((db, NSA_HEADS, LANES), I32)],
        compiler_params=_cparams(("parallel",)),
        name="nsa_dec_cmp",
    )(p["qu"].reshape(db, NSA_HEADS, HEAD_DIM), kcv, c2s)
    return o_c, idx[:, ::NSA_GROUP, :n_sel]


def _nsa_decode(p, kcv, cache_t, win_t, page_table, pos):
    db, n_pages = page_table.shape
    page = cache_t.shape[-1]
    assert pos == n_pages * page and pos % SEL_BLOCK == 0 and page % SEL_BLOCK == 0
    past_blocks = pos // SEL_BLOCK
    o_c, sel_idx = _nsa_decode_select(p, kcv, pos)
    n_sel = sel_idx.shape[2]
    heads3 = lambda a: a.reshape(db, NSA_HEADS, HEAD_DIM)
    bpp = page // SEL_BLOCK

    def blk_map(i, ix, pt, g, j, part):
        bj = jnp.clip(ix[i, g, j], 0, past_blocks - 1)
        return (pt[i, bj // bpp], part, g, 0, 0)

    k_sel, v_sel = 2, 3
    blk_specs = [pl.BlockSpec((None, None, None, HEAD_DIM, page), functools.partial(blk_map, g=g, j=j, part=part))
                 for g in range(NSA_KV_HEADS) for j in range(n_sel) for part in (k_sel, v_sel)]
    per_q2 = lambda w: pl.BlockSpec((None, NSA_HEADS, w), lambda i, ix, pt: (i, 0, 0))
    n_win = win_t.shape[-1]
    gate3 = p["gate"][:, :3 * NSA_HEADS].reshape(db, NSA_HEADS, 3)
    o = pl.pallas_call(
        functools.partial(_nsa_dec_att_kernel, pos=pos, n_sel=n_sel, past_blocks=past_blocks),
        grid_spec=pltpu.PrefetchScalarGridSpec(
            num_scalar_prefetch=2,
            grid=(db,),
            in_specs=[per_q2(HEAD_DIM), per_q2(3), per_q2(HEAD_DIM),
                      pl.BlockSpec((None, 1, 4 * NSA_KV), lambda i, ix, pt: (i, 0, 0)),
                      pl.BlockSpec((None, 2, NSA_KV_HEADS, HEAD_DIM, n_win), lambda i, ix, pt: (i, 0, 0, 0, 0))]
            + blk_specs,
            out_specs=per_q2(HEAD_DIM),
        ),
        out_shape=jax.ShapeDtypeStruct((db, NSA_HEADS, HEAD_DIM), F32),
        compiler_params=_cparams(("arbitrary",)),
        name="nsa_dec_att",
    )(sel_idx, page_table, heads3(p["qr"]), gate3, o_c, p["katt"].reshape(db, 1, 4 * NSA_KV), win_t,
      *([cache_t] * len(blk_specs)))
    return o.reshape(db, NSA_Q)


def _moba_dec_mean_kernel(*refs, n_src, ppb):
    x_refs, o_ref = refs[1:1 + n_src], refs[1 + n_src]
    j = pl.program_id(1)
    nb = o_ref.shape[1]
    page = x_refs[0].shape[-1]
    bps = n_src // ppb

    @pl.when(j == 0)
    def _():
        o_ref[...] = jnp.zeros(o_ref.shape, F32)

    lane = _iota((1, nb), 1)
    acc = o_ref[...]
    for i in range(bps):
        tot = x_refs[i * ppb][...]
        for r in x_refs[i * ppb + 1:(i + 1) * ppb]:
            tot = tot + r[...]
        col = jnp.sum(tot.reshape(MOBA_W, page), axis=1, keepdims=True) * (1.0 / (ppb * page))
        acc = jnp.where(lane == j * bps + i, col, acc)
    o_ref[...] = acc


def _moba_dec_gate_kernel(q_ref, km_ref, idx_ref, *, jq):
    q = q_ref[...]
    nb = km_ref.shape[1]
    n_top = min(MOBA_TOPK, nb)
    head_of_lane = lax.shift_right_logical(_iota((MOBA_HEADS, MOBA_W), 1), HEAD_DIM.bit_length() - 1)
    qbd = jnp.where(head_of_lane == _iota((MOBA_HEADS, MOBA_W), 0),
                    jnp.broadcast_to(q.astype(F32), (MOBA_HEADS, MOBA_W)), 0.0)
    jb = _iota((1, nb), 1)
    jb_f = jb.astype(F32)
    score = jnp.where(jb < jq, _mm(qbd, km_ref[...]), NEG)
    lane = _iota((1, LANES), 1)
    idx = jnp.full((MOBA_HEADS, LANES), -1, I32)
    for it in range(n_top):
        m, first, pick = _take_first_max(score, jb_f, float(nb), 1)
        idx = jnp.where(lane == it, jnp.where(m > 0.5 * NEG, first, -1.0).astype(I32), idx)
        score = jnp.where(pick, 3.0 * NEG, score)
    idx_ref[...] = idx


def _moba_dec_att_kernel(idx_ref, pt_ref, q_ref, kn_ref, vn_ref, *rest, n_top, ppb):
    del pt_ref
    src, o_ref = rest[:-1], rest[-1]
    b, hp = pl.program_id(0), pl.program_id(1)
    hpl = LANES // HEAD_DIM
    scale = HEAD_DIM ** -0.5
    rows = 8
    page = src[0].shape[-1]
    q_all = jnp.broadcast_to(q_ref[...].astype(F32), (rows, LANES))
    k_new = kn_ref[...]
    v_new = vn_ref[...]
    outs = []
    for hh in range(hpl):
        lanes = slice(hh * HEAD_DIM, (hh + 1) * HEAD_DIM)
        kts, vts, valids = [], [], []
        for t in range(n_top):
            ok = jnp.where(idx_ref[b, hp * hpl + hh, t] >= 0, 1.0, 0.0)
            for pg in range(ppb):
                base = 2 * ((hh * n_top + t) * ppb + pg)
                kts.append(src[base][...])
                vts.append(src[base + 1][...])
                valids.append(jnp.full((1, page), 1.0, F32) * ok)
        o = _attend_with_new(q_all[:, lanes], kts, vts, valids, k_new[:, lanes], v_new[:, lanes], True, scale)
        outs.append(o[0:1])
    o_ref[...] = jnp.concatenate(outs, axis=1)


def _moba_decode(p, cache_t, page_table, pos):
    db, n_pages = page_table.shape
    page = cache_t.shape[-1]
    assert MOBA_BLOCK % page == 0 and pos % MOBA_BLOCK == 0 and pos == n_pages * page
    ppb = MOBA_BLOCK // page
    nb = pos // MOBA_BLOCK
    assert nb >= MOBA_TOPK
    pps = PAGES_PER_STEP
    page_specs = [pl.BlockSpec((None, None, MOBA_HEADS, HEAD_DIM, page), functools.partial(
        lambda i, j, pt, pg: (pt[i, j * pps + pg], 0, 0, 0, 0), pg=pg)) for pg in range(pps)]
    kmean_t = pl.pallas_call(
        functools.partial(_moba_dec_mean_kernel, n_src=pps, ppb=ppb),
        grid_spec=pltpu.PrefetchScalarGridSpec(
            num_scalar_prefetch=1,
            grid=(db, n_pages // pps),
            in_specs=page_specs,
            out_specs=pl.BlockSpec((None, MOBA_W, nb), lambda i, j, pt: (i, 0, 0)),
        ),
        out_shape=jax.ShapeDtypeStruct((db, MOBA_W, nb), F32),
        compiler_params=_cparams(("parallel", "arbitrary")),
        name="moba_dec_mean",
    )(page_table, *([cache_t] * pps))
    idx = pl.pallas_call(
        functools.partial(_moba_dec_gate_kernel, jq=nb),
        grid=(db,),
        in_specs=[pl.BlockSpec((None, 1, MOBA_W), lambda i: (i, 0, 0)),
                  pl.BlockSpec((None, MOBA_W, nb), lambda i: (i, 0, 0))],
        out_specs=pl.BlockSpec((None, MOBA_HEADS, LANES), lambda i: (i, 0, 0)),
        out_shape=jax.ShapeDtypeStruct((db, MOBA_HEADS, LANES), I32),
        compiler_params=_cparams(("parallel",)),
        name="moba_dec_gate",
    )(p["qb"].reshape(db, 1, MOBA_W), kmean_t)
    n_top = min(MOBA_TOPK, nb)
    top_idx = idx[:, :, :n_top]
    hpl = LANES // HEAD_DIM
    npair = MOBA_HEADS // hpl

    def src_map(i, hp, ix, pt, hh, t, pg, kv):
        bj = jnp.clip(ix[i, hp * hpl + hh, t], 0, nb - 1)
        return (pt[i, bj * ppb + pg], kv, hp * hpl + hh, 0, 0)

    src_specs = [pl.BlockSpec((None, None, None, HEAD_DIM, page), functools.partial(src_map, hh=hh, t=t, pg=pg, kv=kv))
                 for hh in range(hpl) for t in range(n_top) for pg in range(ppb) for kv in range(2)]
    pair = lambda off: pl.BlockSpec((None, None, 1, LANES), lambda i, hp, ix, pt: (i, off + hp, 0, 0))
    o = pl.pallas_call(
        functools.partial(_moba_dec_att_kernel, n_top=n_top, ppb=ppb),
        grid_spec=pltpu.PrefetchScalarGridSpec(
            num_scalar_prefetch=2,
            grid=(db, npair),
            in_specs=[pair(0), pair(0), pair(npair)] + src_specs,
            out_specs=pair(0),
        ),
        out_shape=jax.ShapeDtypeStruct((db, npair, 1, LANES), F32),
        compiler_params=_cparams(("arbitrary", "arbitrary")),
        name="moba_dec_att",
    )(top_idx, page_table, p["qb"].reshape(db, npair, 1, LANES), p["kvb"].reshape(db, 2 * npair, 1, LANES),
      p["kvb"].reshape(db, 2 * npair, 1, LANES), *([cache_t] * len(src_specs)))
    return o.reshape(db, MOBA_W)


def _route_t(s_t, b_t):
    n_e, n_tok = s_t.shape
    per = n_e // N_GROUPS
    biased = s_t + b_t
    sub_f = _iota((per, 1), 0).astype(F32)
    gscore = []
    for g in range(N_GROUPS):
        x = biased[g * per:(g + 1) * per]
        m1, _, pick = _take_first_max(x, sub_f, float(per), 0)
        gscore.append(m1 + jnp.max(jnp.where(pick, NEG, x), axis=0, keepdims=True))
    gs = jnp.concatenate(gscore, axis=0)
    g_f = _iota((N_GROUPS, 1), 0).astype(F32)
    gmask = jnp.zeros((N_GROUPS, n_tok), F32)
    for _ in range(TOPK_GROUPS):
        _, _, pick = _take_first_max(gs, g_f, float(N_GROUPS), 0)
        gmask = jnp.where(pick, 1.0, gmask)
        gs = jnp.where(pick, NEG, gs)
    masked = jnp.concatenate([jnp.where(gmask[g:g + 1] > 0.5, biased[g * per:(g + 1) * per], NEG)
                              for g in range(N_GROUPS)], axis=0)
    e_f = _iota((n_e, 1), 0).astype(F32)
    ids, ws = [], []
    for _ in range(TOP_K):
        _, first, pick = _take_first_max(masked, e_f, float(n_e), 0)
        ids.append(first)
        ws.append(jnp.sum(jnp.where(pick, s_t, 0.0), axis=0, keepdims=True))
        masked = jnp.where(pick, 3.0 * NEG, masked)
    w = jnp.concatenate(ws, axis=0)
    w = w / jnp.sum(w, axis=0, keepdims=True) * ROUTED_SCALE
    return jnp.concatenate(ids, axis=0).astype(I32), w


def _merge_kernel(x_ref, oa_ref, ob_ref, sc1_ref, sh1_ref, g1_ref, sc2_ref, sh2_ref, wmg_ref, wa_ref, wb_ref,
                  wo_ref, lg_ref, lb_ref, wr_ref, br_ref, x1_ref, h_ref, hp_ref, ti_ref, tw_ref, *, alpha):
    x = x_ref[...]
    d = x.shape[1]
    u = x * (1.0 + sc1_ref[...]) + sh1_ref[...]
    mg = _mm(u, wmg_ref[...])
    y_a = _mm(oa_ref[...], wa_ref[...])
    y_b = _mm(ob_ref[...], wb_ref[...])
    mix = _mm(_sigmoid(mg[:, :d]) * y_a + _sigmoid(mg[:, d:]) * y_b, wo_ref[...])
    x1 = _layer_norm(alpha * x + g1_ref[...] * mix, lg_ref[...], lb_ref[...])
    x1_ref[...] = x1
    h = x1 * (1.0 + sc2_ref[...]) + sh2_ref[...]
    h_ref[...] = h
    hp_ref[...] = _pack_halves(h)
    s_t = _sigmoid(_mm_nt(wr_ref[...], h))
    ti_ref[...], tw_ref[...] = _route_t(s_t, br_ref[...])


def _merge(x, o_a, o_b, mods, w, *, tm, per_token_mod, rows_per_batch, alpha):
    n, d = x.shape
    nt = n // tm
    tiles_per_batch = rows_per_batch // tm
    row = lambda i: (i, 0)
    if per_token_mod:
        mod_spec = pl.BlockSpec((tm, d), row)
    else:
        mod_spec = pl.BlockSpec((None, 1, d), lambda i: (i // tiles_per_batch, 0, 0))
    full = lambda a: pl.BlockSpec(a.shape, lambda i: (0,) * a.ndim)
    ws = [w["w_mg"], w["w_nsa_out"], w["w_moba_out"], w["w_o"], w["ln1_g"], w["ln1_b"], w["w_router_t"], w["b_router"]]
    return pl.pallas_call(
        functools.partial(_merge_kernel, alpha=alpha),
        grid=(nt,),
        in_specs=[pl.BlockSpec((tm, d), row), pl.BlockSpec((tm, NSA_Q), row), pl.BlockSpec((tm, MOBA_W), row)]
        + [mod_spec] * 5 + [full(a) for a in ws],
        out_specs=[pl.BlockSpec((tm, d), row), pl.BlockSpec((tm, d), row), pl.BlockSpec((tm, d // 2), row),
                   pl.BlockSpec((TOP_K, tm), lambda i: (0, i)), pl.BlockSpec((TOP_K, tm), lambda i: (0, i))],
        out_shape=[jax.ShapeDtypeStruct((n, d), F32), jax.ShapeDtypeStruct((n, d), F32),
                   jax.ShapeDtypeStruct((n, d // 2), jnp.uint32),
                   jax.ShapeDtypeStruct((TOP_K, n), I32), jax.ShapeDtypeStruct((TOP_K, n), F32)],
        compiler_params=_cparams(("parallel",), 48),
        name="merge",
    )(x, o_a, o_b, *mods, *ws)


def _pack_halves(x):
    half = x.shape[1] // 2
    bits = pltpu.bitcast(x.astype(BF16).astype(F32), jnp.uint32)
    return bits[:, :half] | lax.shift_right_logical(bits[:, half:], jnp.uint32(16))


def _unpack_halves(p):
    hi = pltpu.bitcast(p & jnp.uint32(0xFFFF0000), F32)
    lo = pltpu.bitcast(lax.shift_left(p, jnp.uint32(16)), F32)
    return jnp.concatenate([hi, lo], axis=1)


def _rank_kernel(ti_ref, rank_ref, cnt_ref, base_ref):
    @pl.when(pl.program_id(0) == 0)
    def _():
        base_ref[...] = jnp.zeros(base_ref.shape, F32)

    ti = ti_ref[...]
    k, tr = ti.shape
    e_iota = _iota((N_EXPERTS, 1), 0)
    hit = lambda j: e_iota == ti[j:j + 1, :]
    onehot = jnp.zeros((N_EXPERTS, tr), F32)
    for j in range(k):
        onehot = onehot + jnp.where(hit(j), 1.0, 0.0)
    tri = (_iota((tr, 1), 0) <= _iota((1, tr), 1)).astype(BF16)
    before = jnp.dot(onehot.astype(BF16), tri, preferred_element_type=F32) - onehot + base_ref[...]
    rank_ref[...] = jnp.concatenate([jnp.sum(jnp.where(hit(j), before, 0.0), axis=0, keepdims=True)
                                     for j in range(k)], axis=0).astype(I32)
    total = base_ref[...] + jnp.sum(onehot, axis=1, keepdims=True)
    base_ref[...] = total
    cnt_ref[...] = jnp.broadcast_to(total, cnt_ref.shape).astype(I32)


def _slot_kernel(ti_ref, rank_ref, start_ref, slot_ref):
    ti = ti_ref[...]
    e_iota = _iota((N_EXPERTS, 1), 0)
    start = start_ref[...]
    rows = [jnp.sum(jnp.where(e_iota == ti[j:j + 1, :], start, 0.0), axis=0, keepdims=True) for j in range(ti.shape[0])]
    slot_ref[...] = rank_ref[...] + jnp.concatenate(rows, axis=0).astype(I32)


def _row_copies(n_tok, n_choice, make_copy):
    def start(t, c):
        for j in range(n_choice):
            make_copy(t, j).start()
        return c

    def wait(t, c):
        for j in range(n_choice):
            make_copy(t, j).wait()
        return c

    return (lambda: lax.fori_loop(0, n_tok, start, 0)), (lambda: lax.fori_loop(0, n_tok, wait, 0))


def _scatter_rows_kernel(slot_ref, h_ref, xs_ref, sem):
    k, td = slot_ref.shape
    copy = lambda t, j: pltpu.make_async_copy(h_ref.at[pl.ds(t, 1), :], xs_ref.at[pl.ds(slot_ref[j, t], 1), :], sem)
    start, wait = _row_copies(td, k, copy)
    start()
    wait()


def _expert_kernel(be_ref, nu_ref, x_ref, wg_ref, wu_ref, wd_ref, y_ref):
    del be_ref

    @pl.when(pl.program_id(0) < nu_ref[0])
    def _():
        x = _unpack_halves(x_ref[...])
        y_ref[...] = _mm(_silu(_mm(x, wg_ref[...])) * _mm(x, wu_ref[...]), wd_ref[...])


def _token_tile(n, pref):
    return pref if n % pref == 0 else n


def _routed_experts(hp, top_i, w):
    n, dp = hp.shape
    d = 2 * dp
    k = top_i.shape[0]
    blk = min(EXPERT_BLK, max(8, (k * n) // N_EXPERTS))
    nblk = -(-(k * n) // blk) + N_EXPERTS
    tr = _token_tile(n, RANK_TILE)
    tiles = lambda t: pl.BlockSpec((k, t), lambda i: (0, i))
    rank, counts = pl.pallas_call(
        _rank_kernel,
        grid=(n // tr,),
        in_specs=[tiles(tr)],
        out_specs=[tiles(tr), pl.BlockSpec((N_EXPERTS, LANES), lambda i: (0, 0))],
        out_shape=[jax.ShapeDtypeStruct((k, n), I32), jax.ShapeDtypeStruct((N_EXPERTS, LANES), I32)],
        scratch_shapes=[pltpu.VMEM((N_EXPERTS, 1), F32)],
        compiler_params=_cparams(("arbitrary",)),
        name="moe_rank",
    )(top_i)
    counts = counts[:, 0]
    padded = (counts + blk - 1) // blk * blk
    end_pad = jnp.cumsum(padded)
    start_pad = end_pad - padded
    blk_e = jnp.minimum(jnp.sum(end_pad[None, :] <= (jnp.arange(nblk, dtype=I32) * blk)[:, None], axis=1),
                        N_EXPERTS - 1).astype(I32)
    n_used = (end_pad[-1] // blk).astype(I32).reshape(1)
    slot = pl.pallas_call(
        _slot_kernel,
        grid=(n // tr,),
        in_specs=[tiles(tr), tiles(tr), pl.BlockSpec((N_EXPERTS, 1), lambda i: (0, 0))],
        out_specs=tiles(tr),
        out_shape=jax.ShapeDtypeStruct((k, n), I32),
        compiler_params=_cparams(("parallel",)),
        name="moe_slot",
    )(top_i, rank, start_pad.astype(F32).reshape(N_EXPERTS, 1))
    td = _token_tile(n, ROW_COPY_TILE)
    x_sorted = pl.pallas_call(
        _scatter_rows_kernel,
        grid=(n // td,),
        in_specs=[pl.BlockSpec((k, td), lambda i: (0, i), memory_space=pltpu.SMEM),
                  pl.BlockSpec((td, dp), lambda i: (i, 0))],
        out_specs=pl.BlockSpec(memory_space=pl.ANY),
        out_shape=jax.ShapeDtypeStruct((nblk * blk, dp), jnp.uint32),
        scratch_shapes=[pltpu.SemaphoreType.DMA],
        compiler_params=_cparams(("arbitrary",)),
        name="moe_scatter",
    )(slot, hp)
    de = w["w_exp_gate"].shape[2]
    y = pl.pallas_call(
        _expert_kernel,
        grid_spec=pltpu.PrefetchScalarGridSpec(
            num_scalar_prefetch=2,
            grid=(nblk,),
            in_specs=[pl.BlockSpec((blk, dp), lambda i, be, nu: (i, 0)),
                      pl.BlockSpec((None, d, de), lambda i, be, nu: (be[i], 0, 0)),
                      pl.BlockSpec((None, d, de), lambda i, be, nu: (be[i], 0, 0)),
                      pl.BlockSpec((None, de, d), lambda i, be, nu: (be[i], 0, 0))],
            out_specs=pl.BlockSpec((blk, d), lambda i, be, nu: (i, 0)),
        ),
        out_shape=jax.ShapeDtypeStruct((nblk * blk, d), F32),
        compiler_params=_cparams(("arbitrary",), 48),
        name="experts",
    )(blk_e, n_used, x_sorted, w["w_exp_gate"], w["w_exp_up"], w["w_exp_down"])
    return y, slot


def _final_kernel(slot_ref, x1_ref, h_ref, tw_ref, g2_ref, wg_ref, wu_ref, wd_ref, lg_ref, lb_ref, y_ref, o_ref,
                  buf_ref, sem, *, alpha):
    k, tc = slot_ref.shape
    copy = lambda t, j: pltpu.make_async_copy(y_ref.at[pl.ds(slot_ref[j, t], 1), :],
                                              buf_ref.at[j, pl.ds(t, 1), :], sem)
    start, wait = _row_copies(tc, k, copy)
    start()
    h = h_ref[...]
    shared = _mm(_silu(_mm(h, wg_ref[...])) * _mm(h, wu_ref[...]), wd_ref[...])
    wait()
    side = max(tc, LANES)
    tw = tw_ref[...]
    if tc < side:
        tw = jnp.concatenate([tw, jnp.zeros((k, side - tc), F32)], axis=1)
    tw_t = jnp.transpose(jnp.concatenate([tw, jnp.zeros((side - k, side), F32)], axis=0))[0:tc]
    routed = tw_t[:, 0:1] * buf_ref[0]
    for j in range(1, k):
        routed = routed + tw_t[:, j:j + 1] * buf_ref[j]
    o_ref[...] = _layer_norm(alpha * x1_ref[...] + g2_ref[...] * (routed + shared), lg_ref[...], lb_ref[...])


def _final(x1, h, y_sorted, slot, top_w, g2, w, *, per_token_mod, rows_per_batch, alpha):
    n, d = x1.shape
    k = slot.shape[0]
    tc = _token_tile(n, ROW_COPY_TILE)
    tiles_per_batch = rows_per_batch // tc
    tile = pl.BlockSpec((tc, d), lambda i: (i, 0))
    mod_spec = tile if per_token_mod else pl.BlockSpec((None, 1, d), lambda i: (i // tiles_per_batch, 0, 0))
    full = lambda a: pl.BlockSpec(a.shape, lambda i: (0,) * a.ndim)
    ws = [w["w_sh_gate"], w["w_sh_up"], w["w_sh_down"], w["ln2_g"], w["ln2_b"]]
    return pl.pallas_call(
        functools.partial(_final_kernel, alpha=alpha),
        grid=(n // tc,),
        in_specs=[pl.BlockSpec((k, tc), lambda i: (0, i), memory_space=pltpu.SMEM), tile, tile,
                  pl.BlockSpec((k, tc), lambda i: (0, i)), mod_spec] + [full(a) for a in ws]
        + [pl.BlockSpec(memory_space=pl.ANY)],
        out_specs=tile,
        out_shape=jax.ShapeDtypeStruct((n, d), F32),
        scratch_shapes=[pltpu.VMEM((k, tc, d), F32), pltpu.SemaphoreType.DMA],
        compiler_params=_cparams(("arbitrary",), 40),
        name="final",
    )(slot, x1, h, top_w, g2, *ws, y_sorted)


def _prep_weights(lp):
    (w_ada, b_ada, w_in, cmp_k_w1, cmp_k_w2, cmp_k_pe, cmp_v_w1, cmp_v_w2, cmp_v_pe, w_nsa_out, w_moba_out, w_o,
     ln1_g, ln1_b, w_router, b_router, w_exp_gate, w_exp_up, w_exp_down, w_sh_gate, w_sh_up, w_sh_down,
     ln2_g, ln2_b) = lp
    c = lambda a: a.astype(MXU_DTYPE)
    row = lambda a: a.reshape(1, -1)
    w_r, w_mg = _reorder_w_in(w_in)
    wk, w2k, pek = _cmp_weights(cmp_k_w1, cmp_k_w2, cmp_k_pe)
    wv, w2v, pev = _cmp_weights(cmp_v_w1, cmp_v_w2, cmp_v_pe)
    return dict(w_ada=w_ada, b_ada=b_ada, w_r=w_r, w_mg=w_mg, cmp=(wk, w2k, pek, wv, w2v, pev),
                w_nsa_out=c(w_nsa_out), w_moba_out=c(w_moba_out), w_o=c(w_o), ln1_g=row(ln1_g), ln1_b=row(ln1_b),
                w_router_t=c(w_router.T), b_router=b_router.reshape(-1, 1),
                w_exp_gate=c(w_exp_gate), w_exp_up=c(w_exp_up), w_exp_down=c(w_exp_down),
                w_sh_gate=c(w_sh_gate), w_sh_up=c(w_sh_up), w_sh_down=c(w_sh_down), ln2_g=row(ln2_g), ln2_b=row(ln2_b))


def _token_tail(x, o_a, o_b, mods, w, *, tm, per_token_mod, rows_per_batch, alpha):
    sc1, sh1, g1, sh2, sc2, g2 = mods
    kw = dict(tm=tm, per_token_mod=per_token_mod, rows_per_batch=rows_per_batch, alpha=alpha)
    x1, h, hp, top_i, top_w = _merge(x, o_a, o_b, (sc1, sh1, g1, sc2, sh2), w, **kw)
    y_sorted, slot = _routed_experts(hp, top_i, w)
    return _final(x1, h, y_sorted, slot, top_w, g2, w, per_token_mod=per_token_mod, rows_per_batch=rows_per_batch,
                  alpha=alpha)


def _layer(xp, xs, c_all, cache_nsa_l, cache_moba_l, win_state_l, page_table, w, alpha):
    b, t, d = xp.shape
    db, ts, _ = xs.shape
    assert ts == 1
    page = cache_nsa_l.shape[1]
    pos = page_table.shape[1] * page
    assert win_state_l.shape[1] == WINDOW and t >= WINDOW
    mod = _ada(c_all, w["w_ada"], w["b_ada"])
    pieces = [mod[:, i * d:(i + 1) * d] for i in range(6)]
    mods_p = [m[:b].reshape(b, 1, d) for m in pieces]
    mods_s = [m[b:b + db] for m in pieces]
    order = lambda m: (m[1], m[0], m[2], m[3], m[4], m[5])
    mods_p, mods_s = order(mods_p), order(mods_s)

    tm = TOKEN_TILE
    xp2 = xp.reshape(b * t, d)
    pp = _inproj(xp2, mods_p[0], mods_p[1], w["w_r"], _rope_tables(jnp.arange(t)), tm=tm, per_token_mod=False,
                 rows_per_batch=t, prompt=True)
    kcv_p, vct_p = _cmp_prompt(pp["kcvc"].reshape(b, t, 2 * NSA_KV), w["cmp"])
    oa_p = _nsa_prompt(pp, kcv_p, vct_p, b, t)
    ob_p = _moba_prompt(pp, b, t)
    yp = _token_tail(xp2, oa_p, ob_p, mods_p, w, tm=tm, per_token_mod=False, rows_per_batch=t, alpha=alpha)

    xs2 = xs.reshape(db, d)
    ps = _inproj(xs2, mods_s[0], mods_s[1], w["w_r"], _rope_tables(jnp.full((db,), pos)), tm=db, per_token_mod=True,
                 rows_per_batch=db, prompt=False)
    token_minor = lambda a: jnp.transpose(a, (0, 2, 3, 4, 1))
    kcv_s, _ = _cmp_decode(token_minor(cache_nsa_l), page_table, w["cmp"])
    oa_s = _nsa_decode(ps, kcv_s, token_minor(cache_nsa_l), token_minor(win_state_l), page_table, pos)
    ob_s = _moba_decode(ps, token_minor(cache_moba_l), page_table, pos)
    ys = _token_tail(xs2, oa_s, ob_s, mods_s, w, tm=db, per_token_mod=True, rows_per_batch=db, alpha=alpha)

    g, hd = NSA_KV_HEADS, HEAD_DIM
    token_major = lambda a, parts, heads: jnp.transpose(a.reshape(b, parts, heads, hd, a.shape[-1]), (0, 4, 1, 2, 3))
    win_p = token_major(pp["wint"][:, :, t - WINDOW:], 2, g)
    win_s = jnp.concatenate([win_state_l[:, 1:], ps["win"].reshape(db, 1, 2, g, hd)], axis=1)
    return (yp.reshape(b, t, d), ys.reshape(db, 1, d),
            token_major(pp["nsat"], 4, g), ps["nsa"].reshape(db, 1, 4, g, hd),
            token_major(pp["mobat"], 2, MOBA_HEADS), ps["moba"].reshape(db, 1, 2, MOBA_HEADS, hd), win_p, win_s)


def kernel(x_prompt, x_sample, cache_nsa, cache_moba, state_nsa_win, page_table, c_prompt, c_sample, w_ada, b_ada,
           w_in, cmp_k_w1, cmp_k_w2, cmp_k_pe, cmp_v_w1, cmp_v_w2, cmp_v_pe, w_nsa_out, w_moba_out, w_o, ln1_g,
           ln1_b, w_router, b_router, w_exp_gate, w_exp_up, w_exp_down, w_sh_gate, w_sh_up, w_sh_down, ln2_g,
           ln2_b):
    params = (w_ada, b_ada, w_in, cmp_k_w1, cmp_k_w2, cmp_k_pe, cmp_v_w1, cmp_v_w2, cmp_v_pe, w_nsa_out, w_moba_out,
              w_o, ln1_g, ln1_b, w_router, b_router, w_exp_gate, w_exp_up, w_exp_down, w_sh_gate, w_sh_up,
              w_sh_down, ln2_g, ln2_b)
    depth = w_ada.shape[0]
    alpha = (2 * depth) ** 0.25
    b, db = x_prompt.shape[0], x_sample.shape[0]
    rows = -(-(b + db) // 8) * 8
    c_all = jnp.pad(jnp.concatenate([c_prompt, c_sample], axis=0), ((0, rows - b - db), (0, 0)))
    xp, xs = x_prompt, x_sample
    outs = [[] for _ in range(6)]
    for l in range(depth):
        w = _prep_weights([p[l] for p in params])
        res = _layer(xp, xs, c_all, cache_nsa[l], cache_moba[l], state_nsa_win[l], page_table, w, alpha)
        xp, xs = res[0], res[1]
        for acc, r in zip(outs, res[2:]):
            acc.append(r)
    return (xp, xs) + tuple(jnp.stack(o) for o in outs)
```

```python
import functools

import jax
import jax.numpy as jnp
from jax import lax
from jax.experimental import pallas as pl
from jax.experimental.pallas import tpu as pltpu

F32 = jnp.float32
BF16 = jnp.bfloat16
I32 = jnp.int32
MXU_DTYPE = jnp.bfloat16

HEAD_DIM = 64
ROPE_DIMS = HEAD_DIM // 4
ROPE_HALF = ROPE_DIMS // 2
ROPE_THETA = 500000.0
NSA_HEADS = 8
NSA_KV_HEADS = 2
NSA_GROUP = NSA_HEADS // NSA_KV_HEADS
CMP_BLOCK = 32
CMP_STRIDE = 16
CMP_RATIO = CMP_BLOCK // CMP_STRIDE
SEL_BLOCK = 64
SEL_TOPN = 16
WINDOW = 512
MOBA_HEADS = 8
MOBA_BLOCK = 256
MOBA_TOPK = 3
N_EXPERTS = 64
TOP_K = 8
N_GROUPS = 8
TOPK_GROUPS = 4
ROUTED_SCALE = 2.5
LN_EPS = 1e-5

LANES = 128
NSA_Q = NSA_HEADS * HEAD_DIM
NSA_KV = NSA_KV_HEADS * HEAD_DIM
MOBA_W = MOBA_HEADS * HEAD_DIM
NSA_ROW = 4 * NSA_KV
MOBA_ROW = 2 * MOBA_W
GATE_PAD = LANES

_QK_SCALE = HEAD_DIM ** -0.5
assert _QK_SCALE == 0.125
NEG = -1e30
BIG = 3e38
TOKEN_TILE = 256
NSA_QB = 256
EXPERT_BLK = 512
RANK_TILE = 512
ROW_COPY_TILE = 256
MOBA_HEADS_PER_STEP = 8
PAGES_PER_STEP = 16


def _sigmoid(x):
    return 1.0 / (1.0 + jnp.exp(-x))


def _silu(x):
    return x * _sigmoid(x)


def _mm(a, b):
    return jnp.dot(a.astype(MXU_DTYPE), b.astype(MXU_DTYPE), preferred_element_type=F32)


def _mm_nt(a, b):
    return lax.dot_general(a.astype(MXU_DTYPE), b.astype(MXU_DTYPE), (((1,), (1,)), ((), ())),
                           preferred_element_type=F32)


def _iota(shape, axis):
    return lax.broadcasted_iota(I32, shape, axis)


V7X_VMEM_MIB = 64


def _cparams(sem, vmem_mb=None):
    kw = dict(dimension_semantics=sem)
    if vmem_mb is not None:
        assert vmem_mb <= V7X_VMEM_MIB - 8
        kw["vmem_limit_bytes"] = vmem_mb << 20
    return pltpu.CompilerParams(**kw)


def _masked_softmax(s, mask, axis=-1):
    m = jnp.max(jnp.where(mask, s, NEG), axis=axis, keepdims=True)
    e = jnp.where(mask, jnp.exp(s - m), 0.0)
    d = jnp.sum(e, axis=axis, keepdims=True)
    return e / jnp.where(d > 0, d, 1.0)


def _layer_norm(z, g, b):
    mu = jnp.mean(z, axis=-1, keepdims=True)
    zc = z - mu
    var = jnp.mean(zc * zc, axis=-1, keepdims=True)
    return zc * lax.rsqrt(var + LN_EPS) * g + b


def _take_first_max(score, idx_f, n_f, axis):
    m = jnp.max(score, axis=axis, keepdims=True)
    first = jnp.min(jnp.where(score == m, idx_f, n_f), axis=axis, keepdims=True)
    return m, first, idx_f == first


def _ada_kernel(c_ref, w_ref, b_ref, o_ref):
    a = _silu(c_ref[...])
    o_ref[...] = _mm(a, w_ref[...]) + b_ref[...]


def _ada(c_all, w_ada, b_ada):
    r, d = c_all.shape
    e6 = w_ada.shape[1]
    tn = 1024
    return pl.pallas_call(
        _ada_kernel,
        grid=(e6 // tn,),
        in_specs=[pl.BlockSpec((r, d), lambda j: (0, 0)),
                  pl.BlockSpec((d, tn), lambda j: (0, j)),
                  pl.BlockSpec((1, tn), lambda j: (0, j))],
        out_specs=pl.BlockSpec((r, tn), lambda j: (0, j)),
        out_shape=jax.ShapeDtypeStruct((r, e6), F32),
        compiler_params=_cparams(("arbitrary",)),
        name="ada",
    )(c_all, w_ada, b_ada.reshape(1, e6))


_C_QA = 0
_C_KVA = _C_QA + NSA_Q
_C_QB = _C_KVA + 6 * NSA_KV
_C_KB = _C_QB + MOBA_W
_C_VB = _C_KB + MOBA_W
_C_GATE = _C_VB + MOBA_W
_C_END = _C_GATE + GATE_PAD


def _rope(x, cs, s1, s2):
    parts = []
    for j in range(x.shape[1] // LANES):
        xj = x[:, j * LANES:(j + 1) * LANES]
        parts.append(xj * cs + pltpu.roll(xj, ROPE_HALF, 1) * s1 + pltpu.roll(xj, LANES - ROPE_HALF, 1) * s2)
    return parts[0] if len(parts) == 1 else jnp.concatenate(parts, axis=1)


_TOKEN_MAJOR = dict(qu=(NSA_Q, BF16), qr=(NSA_Q, BF16), qb=(MOBA_W, BF16), nsa=(NSA_ROW, F32), kcvc=(2 * NSA_KV, F32),
                    katt=(4 * NSA_KV, BF16), win=(2 * NSA_KV, F32), moba=(MOBA_ROW, F32), kvb=(MOBA_ROW, BF16),
                    gate=(GATE_PAD, F32))
_PER_TILE_T = dict(qbt=(MOBA_W, BF16), vbt=(MOBA_W, BF16), qut=(NSA_Q, BF16), qrt=(NSA_Q, BF16), gt=(GATE_PAD, F32),
                   vst=(NSA_KV, BF16), vwt=(NSA_KV, BF16))
_PER_BATCH_T = dict(nsat=(NSA_ROW, F32), wint=(2 * NSA_KV, F32), mobat=(MOBA_ROW, F32))
_DECODE_OUTS = ("qu", "qr", "qb", "nsa", "katt", "win", "moba", "kvb", "gate")
_PROMPT_OUTS = ("kcvc", "katt", "kvb", "kmean") + tuple(_PER_TILE_T) + tuple(_PER_BATCH_T)


def _inproj_kernel(x_ref, sc_ref, sh_ref, w_ref, cs_ref, s1_ref, s2_ref, *out_refs, names):
    u = x_ref[...] * (1.0 + sc_ref[...]) + sh_ref[...]
    r = _mm(u, w_ref[...])
    cs, s1, s2 = cs_ref[...], s1_ref[...], s2_ref[...]
    rope = lambda v: _rope(v, cs, s1, s2)
    qa = r[:, _C_QA:_C_KVA]
    qa_rot = rope(qa)
    o = _C_KVA
    kc_vc = r[:, o:o + 2 * NSA_KV]
    ks = rope(r[:, o + 2 * NSA_KV:o + 3 * NSA_KV])
    vs = r[:, o + 3 * NSA_KV:o + 4 * NSA_KV]
    kw = rope(r[:, o + 4 * NSA_KV:o + 5 * NSA_KV])
    vw = r[:, o + 5 * NSA_KV:o + 6 * NSA_KV]
    qb = rope(r[:, _C_QB:_C_KB])
    kb = rope(r[:, _C_KB:_C_VB])
    vb = r[:, _C_VB:_C_GATE]
    gate = r[:, _C_GATE:_C_END]
    cat = lambda *v: jnp.concatenate(v, axis=1)
    pieces = dict(qu=lambda: qa, qr=lambda: qa_rot, qb=lambda: qb, nsa=lambda: cat(kc_vc, ks, vs),
                  kcvc=lambda: kc_vc, katt=lambda: cat(ks, vs, kw, vw), win=lambda: cat(kw, vw),
                  moba=lambda: cat(kb, vb), kvb=lambda: cat(kb, vb), gate=lambda: gate,
                  kmean=lambda: jnp.sum(kb, axis=0, keepdims=True) * (1.0 / kb.shape[0]),
                  qbt=lambda: qb.T, vbt=lambda: vb.T, qut=lambda: qa.T, qrt=lambda: qa_rot.T, gt=lambda: gate.T,
                  vst=lambda: vs.T, vwt=lambda: vw.T, nsat=lambda: cat(kc_vc, ks, vs).T, wint=lambda: cat(kw, vw).T,
                  mobat=lambda: cat(kb, vb).T)
    for name, ref in zip(names, out_refs):
        ref[...] = pieces[name]().astype(ref.dtype)


def _inproj(x, sc, sh, w_r, rope_tabs, *, tm, per_token_mod, rows_per_batch, prompt):
    n, d = x.shape
    nt = n // tm
    tiles_per_batch = rows_per_batch // tm
    row = lambda i: (i, 0)
    if per_token_mod:
        mod_spec = pl.BlockSpec((tm, d), row)
    else:
        mod_spec = pl.BlockSpec((None, 1, d), lambda i: (i // tiles_per_batch, 0, 0))
    tab_spec = pl.BlockSpec((tm, LANES), lambda i: (i % tiles_per_batch, 0))
    names = _PROMPT_OUTS if prompt else _DECODE_OUTS
    out_shape, out_specs = [], []
    for name in names:
        if name in _TOKEN_MAJOR:
            w, dt = _TOKEN_MAJOR[name]
            out_shape.append(jax.ShapeDtypeStruct((n, w), dt))
            out_specs.append(pl.BlockSpec((tm, w), row))
        elif name in _PER_TILE_T:
            w, dt = _PER_TILE_T[name]
            out_shape.append(jax.ShapeDtypeStruct((nt, w, tm), dt))
            out_specs.append(pl.BlockSpec((None, w, tm), lambda i: (i, 0, 0)))
        elif name in _PER_BATCH_T:
            w, dt = _PER_BATCH_T[name]
            out_shape.append(jax.ShapeDtypeStruct((n // rows_per_batch, w, rows_per_batch), dt))
            out_specs.append(pl.BlockSpec((None, w, tm), lambda i: (i // tiles_per_batch, 0, i % tiles_per_batch)))
        else:
            assert name == "kmean" and tm == MOBA_BLOCK
            out_shape.append(jax.ShapeDtypeStruct((nt, 1, MOBA_W), F32))
            out_specs.append(pl.BlockSpec((None, 1, MOBA_W), lambda i: (i, 0, 0)))
    res = pl.pallas_call(
        functools.partial(_inproj_kernel, names=names),
        grid=(nt,),
        in_specs=[pl.BlockSpec((tm, d), row), mod_spec, mod_spec,
                  pl.BlockSpec(w_r.shape, lambda i: (0, 0)), tab_spec, tab_spec, tab_spec],
        out_specs=out_specs,
        out_shape=out_shape,
        compiler_params=_cparams(("parallel",), 48),
        name="inproj",
    )(x, sc, sh, w_r, *rope_tabs)
    return dict(zip(names, res))


def _rope_tables(pos):
    inv = ROPE_THETA ** (-jnp.arange(ROPE_HALF, dtype=F32) / ROPE_HALF)
    ang = pos.astype(F32)[:, None] * inv
    cos, sin = jnp.cos(ang), jnp.sin(ang)
    rows = pos.shape[0]
    one = jnp.ones((rows, HEAD_DIM - ROPE_DIMS), F32)
    zero = jnp.zeros((rows, HEAD_DIM - ROPE_DIMS), F32)
    zh = jnp.zeros((rows, ROPE_HALF), F32)
    cs = jnp.concatenate([cos, cos, one], axis=1)
    s1 = jnp.concatenate([zh, sin, zero], axis=1)
    s2 = jnp.concatenate([-sin, zh, zero], axis=1)
    rep = LANES // HEAD_DIM
    return tuple(jnp.tile(t, (1, rep)) for t in (cs, s1, s2))


def _reorder_w_in(w_in):
    d = w_in.shape[0]
    sizes = (NSA_Q, 6 * NSA_KV, 3 * NSA_HEADS, MOBA_W, MOBA_W, MOBA_W, d, d)
    offs = [0]
    for s in sizes:
        offs.append(offs[-1] + s)
    q_a, kv_a, gate, q_b, k_b, v_b, mg_a, mg_b = [w_in[:, offs[i]:offs[i + 1]] for i in range(8)]
    gate = jnp.pad(gate, ((0, 0), (0, GATE_PAD - 3 * NSA_HEADS)))
    w_r = jnp.concatenate([q_a, kv_a, q_b, k_b, v_b, gate], axis=1).astype(MXU_DTYPE)
    w_mg = jnp.concatenate([mg_a, mg_b], axis=1).astype(MXU_DTYPE)
    return w_r, w_mg


def _cmp1_kernel(x_ref, wk_ref, wv_ref, pek_ref, pev_ref, o_ref):
    def gather(off):
        return jnp.concatenate([x_ref[:, s * 2 * NSA_KV + off:s * 2 * NSA_KV + off + NSA_KV]
                                for s in range(CMP_STRIDE)], axis=1)

    xk = gather(0)
    xv = gather(NSA_KV)
    outs = []
    for r in range(CMP_RATIO):
        outs.append(_mm(xk + pek_ref[r], wk_ref[r]))
        outs.append(_mm(xv + pev_ref[r], wv_ref[r]))
    o_ref[...] = jnp.concatenate(outs, axis=1)


def _cmp2_kernel(a_ref, w2k_ref, w2v_ref, o_ref, vt_ref):
    a = a_ref[...]
    n = a.shape[0]
    w = NSA_KV
    hk = a[:, 0:w] + pltpu.roll(a[:, 2 * w:3 * w], n - 1, 0)
    hv = a[:, w:2 * w] + pltpu.roll(a[:, 3 * w:4 * w], n - 1, 0)
    vc = _mm(_silu(hv), w2v_ref[...])
    o_ref[...] = jnp.concatenate([_mm(_silu(hk), w2k_ref[...]), vc], axis=1)
    vt_ref[...] = jnp.transpose(vc)


def _blockdiag(w, reps):
    k, n = w.shape
    out = jnp.zeros((reps * k, reps * n), w.dtype)
    for g in range(reps):
        out = out.at[g * k:(g + 1) * k, g * n:(g + 1) * n].set(w)
    return out


def _cmp_weights(w1, w2, pe):
    g = NSA_KV_HEADS
    w1s = jnp.stack([jnp.concatenate([_blockdiag(w1[r, s], g) for s in range(CMP_STRIDE)], axis=0)
                     for r in range(CMP_RATIO)]).astype(MXU_DTYPE)
    pes = jnp.tile(pe[:, :, None, :], (1, 1, g, 1)).reshape(CMP_RATIO, 1, CMP_STRIDE * NSA_KV)
    return w1s, _blockdiag(w2, g).astype(MXU_DTYPE), pes


def _cmp_stage2(a, w2k, w2v):
    b, nch, _ = a.shape
    return pl.pallas_call(
        _cmp2_kernel,
        grid=(b,),
        in_specs=[pl.BlockSpec((None, nch, 4 * NSA_KV), lambda i: (i, 0, 0)),
                  pl.BlockSpec(w2k.shape, lambda i: (0, 0)), pl.BlockSpec(w2v.shape, lambda i: (0, 0))],
        out_specs=[pl.BlockSpec((None, nch, 2 * NSA_KV), lambda i: (i, 0, 0)),
                   pl.BlockSpec((None, NSA_KV, nch), lambda i: (i, 0, 0))],
        out_shape=[jax.ShapeDtypeStruct((b, nch, 2 * NSA_KV), F32), jax.ShapeDtypeStruct((b, NSA_KV, nch), F32)],
        compiler_params=_cparams(("parallel",)),
        name="cmp2",
    )(a, w2k, w2v)


def _cmp_prompt(kcvc, cw):
    b, t, roww = kcvc.shape
    nch = t // CMP_STRIDE
    tc = min(128, nch)
    x = kcvc.reshape(b, nch, CMP_STRIDE * roww)
    wk, w2k, pek, wv, w2v, pev = cw
    full = lambda a: pl.BlockSpec(a.shape, lambda i, j: (0,) * a.ndim)
    a = pl.pallas_call(
        _cmp1_kernel,
        grid=(b, nch // tc),
        in_specs=[pl.BlockSpec((None, tc, CMP_STRIDE * roww), lambda i, j: (i, j, 0)),
                  full(wk), full(wv), full(pek), full(pev)],
        out_specs=pl.BlockSpec((None, tc, 4 * NSA_KV), lambda i, j: (i, j, 0)),
        out_shape=jax.ShapeDtypeStruct((b, nch, 4 * NSA_KV), F32),
        compiler_params=_cparams(("parallel", "parallel"), 48),
        name="cmp1_prompt",
    )(x, wk, wv, pek, pev)
    return _cmp_stage2(a, w2k, w2v)


def _cmp1_paged_kernel(*refs, n_pages):
    page_refs = refs[1:1 + 2 * n_pages]
    wk_ref, wv_ref, pek_ref, pev_ref, o_ref, xk_scr, xv_scr = refs[1 + 2 * n_pages:]
    page = page_refs[0].shape[-1]
    for p in range(n_pages):
        for part, scr in ((0, xk_scr), (1, xv_scr)):
            scr[p * page:(p + 1) * page, :] = jnp.transpose(page_refs[2 * p + part][...].reshape(NSA_KV, page))
    n_chunk = n_pages * page // CMP_STRIDE

    def gather(scr):
        return jnp.concatenate([scr[pl.ds(s, n_chunk, stride=CMP_STRIDE), :] for s in range(CMP_STRIDE)], axis=1)

    xk, xv = gather(xk_scr), gather(xv_scr)
    outs = []
    for r in range(CMP_RATIO):
        outs.append(_mm(xk + pek_ref[r], wk_ref[r]))
        outs.append(_mm(xv + pev_ref[r], wv_ref[r]))
    o_ref[...] = jnp.concatenate(outs, axis=1)


def _cmp_decode(cache_t, page_table, cw):
    page = cache_t.shape[-1]
    db, n_pages = page_table.shape
    cpp = page // CMP_STRIDE
    pps = PAGES_PER_STEP
    wk, w2k, pek, wv, w2v, pev = cw
    nch = n_pages * cpp
    full = lambda a: pl.BlockSpec(a.shape, lambda i, j, pt: (0,) * a.ndim)
    k_cmp, v_cmp = 0, 1
    page_specs = [pl.BlockSpec((None, None, NSA_KV_HEADS, HEAD_DIM, page), functools.partial(
        lambda i, j, pt, p, part: (pt[i, j * pps + p], part, 0, 0, 0), p=p, part=part))
        for p in range(pps) for part in (k_cmp, v_cmp)]
    a = pl.pallas_call(
        functools.partial(_cmp1_paged_kernel, n_pages=pps),
        grid_spec=pltpu.PrefetchScalarGridSpec(
            num_scalar_prefetch=1,
            grid=(db, n_pages // pps),
            in_specs=page_specs + [full(wk), full(wv), full(pek), full(pev)],
            out_specs=pl.BlockSpec((None, pps * cpp, 4 * NSA_KV), lambda i, j, pt: (i, j, 0)),
            scratch_shapes=[pltpu.VMEM((pps * page, NSA_KV), F32), pltpu.VMEM((pps * page, NSA_KV), F32)],
        ),
        out_shape=jax.ShapeDtypeStruct((db, nch, 4 * NSA_KV), F32),
        compiler_params=_cparams(("parallel", "parallel"), 48),
        name="cmp1_decode",
    )(page_table, *([cache_t] * len(page_specs)), wk, wv, pek, pev)
    return _cmp_stage2(a, w2k, w2v)


def _cmp_to_sel(nch, ns_pad):
    cs = jnp.arange(nch) * CMP_STRIDE
    ss = jnp.arange(ns_pad) * SEL_BLOCK
    return ((cs[:, None] < ss[None] + SEL_BLOCK) & (cs[:, None] + CMP_BLOCK > ss[None])).astype(F32)


def _nsa_prompt_kernel(qut_ref, qrt_ref, gt_ref, kc_ref, vct_ref, c2st_ref, ks_ref, kw_ref, vst_ref, vwt_ref, o_ref, *,
                       qb, kc):
    s0 = pl.program_id(1) * qb
    nkc = vst_ref.shape[0]
    nch = kc_ref.shape[0]
    ns = c2st_ref.shape[0]
    n_sel = min(SEL_TOPN, ns)
    hg, hd, groups = NSA_GROUP, HEAD_DIM, range(NSA_KV_HEADS)
    sel_shift = SEL_BLOCK.bit_length() - 1
    pos_q = s0 + _iota((1, qb), 1)
    pos_l = jnp.concatenate([pos_q] * hg, axis=1)
    gates = _sigmoid(gt_ref[...])
    zeros = jnp.zeros((hd, hg * qb), F32)

    def group_q(ref, g):
        x = jnp.concatenate([ref[(g * hg + h) * hd:(g * hg + h + 1) * hd, :] for h in range(hg)], axis=1)
        parts = [zeros] * NSA_KV_HEADS
        parts[g] = x.astype(F32) * _QK_SCALE
        return jnp.concatenate(parts, axis=0).astype(BF16)

    qzu = [group_q(qut_ref, g) for g in groups]
    qzr = [group_q(qrt_ref, g) for g in groups]
    cend = _iota((nch, 1), 0) * CMP_STRIDE + (CMP_BLOCK - 1)
    kcb = kc_ref[...]
    p_c = [_masked_softmax(_mm(kcb, qzu[g]), cend <= pos_l, axis=0) for g in groups]
    o_c = [_mm(vct_ref[g * hd:(g + 1) * hd, :], p_c[g]) for g in groups]
    jsel = _iota((ns, 1), 0)
    jsel_f = jsel.astype(F32)
    jq = lax.shift_right_logical(pos_q, sel_shift)
    forced = (jsel == 0) | (jsel == jq) | (jsel == jq - 1)
    sel_b = []
    for g in groups:
        p_sum = p_c[g][:, 0:qb]
        for h in range(1, hg):
            p_sum = p_sum + p_c[g][:, h * qb:(h + 1) * qb]
        score = jnp.where(jsel <= jq, jnp.where(forced, BIG, _mm(c2st_ref[...], p_sum)), -1.0)
        sel = jnp.zeros((ns, qb), F32)
        for _ in range(n_sel):
            m, _, pick = _take_first_max(score, jsel_f, float(ns), 0)
            sel = jnp.where(pick & (m >= 0.0), 1.0, sel)
            score = jnp.where(pick, -2.0, score)
        sel_b.append(sel.astype(BF16))
    blk_lane = _iota((1, ns), 1)

    def sel_step(c, carry):
        k0 = pl.multiple_of(c * kc, kc)
        kpos = k0 + _iota((kc, 1), 0)
        expand = (lax.shift_right_logical(kpos, sel_shift) == blk_lane).astype(BF16)
        causal = kpos <= pos_q
        kk = ks_ref[pl.ds(k0, kc), :]
        s = [_mm(kk, qzr[g]) for g in groups]
        bias = [jnp.where(causal & (jnp.dot(expand, sel_b[g], preferred_element_type=F32) > 0.5), 0.0, NEG)
                for g in groups]
        s = [s[g] + jnp.concatenate([bias[g]] * hg, axis=1) for g in groups]
        m_new = [jnp.maximum(carry[3 * g], jnp.max(s[g], axis=0, keepdims=True)) for g in groups]
        p = [jnp.exp(s[g] - jnp.maximum(m_new[g], 0.5 * NEG)) for g in groups]
        pv = [_mm(vst_ref[c, g * hd:(g + 1) * hd, :], p[g]) for g in groups]
        out = []
        for g in groups:
            m_i, l_i, acc = carry[3 * g:3 * g + 3]
            alpha = jnp.exp(m_i - m_new[g])
            out += [m_new[g], alpha * l_i + jnp.sum(p[g], axis=0, keepdims=True), alpha * acc + pv[g]]
        return tuple(out)

    init = (jnp.full((1, hg * qb), NEG, F32), jnp.zeros((1, hg * qb), F32), zeros) * NSA_KV_HEADS
    fin = lax.fori_loop(0, s0 // kc + 1, sel_step, init)
    o_s = [fin[3 * g + 2] / jnp.where(fin[3 * g + 1] > 0, fin[3 * g + 1], 1.0) for g in groups]
    nwc = WINDOW // kc + 1
    c_lo = jnp.minimum(jnp.maximum(s0 - WINDOW, 0) // kc, nkc - nwc)
    wstart = pl.multiple_of(c_lo * kc, kc)
    wpos = wstart + _iota((nwc * kc, 1), 0)
    mask_w = (wpos <= pos_l) & (wpos > pos_l - WINDOW)
    kwb = kw_ref[pl.ds(wstart, nwc * kc), :]
    p_w = [_masked_softmax(_mm(kwb, qzr[g]), mask_w, axis=0) for g in groups]
    o_w = []
    for g in groups:
        acc = _mm(vwt_ref[c_lo, g * hd:(g + 1) * hd, :], p_w[g][0:kc])
        for i in range(1, nwc):
            acc = acc + _mm(vwt_ref[c_lo + i, g * hd:(g + 1) * hd, :], p_w[g][i * kc:(i + 1) * kc])
        o_w.append(acc)
    rows = []
    for g in groups:
        for h in range(hg):
            hh, lanes = g * hg + h, slice(h * qb, (h + 1) * qb)
            rows.append(gates[3 * hh:3 * hh + 1] * o_c[g][:, lanes] + gates[3 * hh + 1:3 * hh + 2] * o_s[g][:, lanes]
                        + gates[3 * hh + 2:3 * hh + 3] * o_w[g][:, lanes])
    o_ref[...] = jnp.transpose(jnp.concatenate(rows, axis=0)).astype(o_ref.dtype)


def _nsa_prompt(p, kcv, vct, b, t):
    qb, kc = NSA_QB, TOKEN_TILE
    nwc = WINDOW // kc + 1
    assert t % kc == 0 and kc % qb == 0 and t >= nwc * kc and t % SEL_BLOCK == 0 and WINDOW % kc == 0
    nkc = t // kc
    nch = kcv.shape[1]
    ns = t // SEL_BLOCK
    c2st = _cmp_to_sel(nch, ns).T.astype(BF16)
    katt = p["katt"].reshape(b, t, 4 * NSA_KV)
    qsub = kc // qb
    q_spec = lambda w: pl.BlockSpec((None, w, qb), lambda i, j: (i * nkc + j // qsub, 0, j % qsub))
    chunks = lambda a: a.reshape(b, nkc, NSA_KV, kc)
    chunk_spec = pl.BlockSpec((None, nkc, NSA_KV, kc), lambda i, j: (i, 0, 0, 0))
    return pl.pallas_call(
        functools.partial(_nsa_prompt_kernel, qb=qb, kc=kc),
        grid=(b, t // qb),
        in_specs=[q_spec(NSA_Q), q_spec(NSA_Q), q_spec(GATE_PAD),
                  pl.BlockSpec((None, nch, NSA_KV), lambda i, j: (i, 0, 0)),
                  pl.BlockSpec((None, NSA_KV, nch), lambda i, j: (i, 0, 0)),
                  pl.BlockSpec((ns, nch), lambda i, j: (0, 0)),
                  pl.BlockSpec((None, t, NSA_KV), lambda i, j: (i, 0, 0)),
                  pl.BlockSpec((None, t, NSA_KV), lambda i, j: (i, 0, 2)),
                  chunk_spec, chunk_spec],
        out_specs=pl.BlockSpec((None, qb, NSA_Q), lambda i, j: (i, j, 0)),
        out_shape=jax.ShapeDtypeStruct((b, t, NSA_Q), BF16),
        compiler_params=_cparams(("parallel", "arbitrary"), 48),
        name="nsa_prompt",
    )(p["qut"], p["qrt"], p["gt"], kcv, vct, c2st, katt, katt, chunks(p["vst"]), chunks(p["vwt"])).reshape(b * t, NSA_Q)


def _moba_prompt_kernel(qt_ref, km_ref, k_ref, vt_ref, o_ref, *, blk):
    jq = pl.program_id(2)
    nb = km_ref.shape[0]
    n_top = min(MOBA_TOPK, nb)
    hpl = LANES // HEAD_DIM
    n_heads = qt_ref.shape[0] // HEAD_DIM
    jb = _iota((nb, 1), 0)
    jb_f = jb.astype(F32)
    row_head = lax.shift_right_logical(_iota((LANES, 1), 0), HEAD_DIM.bit_length() - 1)
    tile = lambda hh: slice(hh // hpl * LANES, (hh // hpl + 1) * LANES)
    qz, sels = [], []
    for hh in range(n_heads):
        qt = qt_ref[tile(hh), :].astype(F32) * _QK_SCALE
        qz.append(jnp.where(row_head == hh % hpl, qt, 0.0).astype(BF16))
        score = jnp.where(jb < jq, _mm(km_ref[:, tile(hh)], qz[hh]), NEG)
        sel = jnp.zeros((nb, blk), F32)
        for _ in range(n_top):
            m, _, pick = _take_first_max(score, jb_f, float(nb), 0)
            sel = jnp.where(pick & (m > 0.5 * NEG), 1.0, sel)
            score = jnp.where(pick, 3.0 * NEG, score)
        sels.append(sel)

    def attend(j, carry, mask_fn):
        k0 = pl.multiple_of(j * blk, blk)
        heads = range(n_heads)
        s = [_mm(k_ref[pl.ds(k0, blk), tile(hh)], qz[hh]) for hh in heads]
        m_new, m_sub = zip(*[mask_fn(hh, s[hh], carry[3 * hh]) for hh in heads])
        p = [jnp.exp(s[hh] - m_sub[hh]) for hh in heads]
        pv = [_mm(vt_ref[j, hh * HEAD_DIM:(hh + 1) * HEAD_DIM, :], p[hh]) for hh in heads]
        out = []
        for hh in heads:
            m_i, l_i, acc = carry[3 * hh:3 * hh + 3]
            alpha = jnp.exp(m_i - m_new[hh])
            out += [m_new[hh], alpha * l_i + jnp.sum(p[hh], axis=0, keepdims=True), alpha * acc + pv[hh]]
        return tuple(out)

    def past_mask(j):
        def fn(hh, s, m_i):
            picked = jnp.sum(jnp.where(jb == j, sels[hh], 0.0), axis=0, keepdims=True) > 0.5
            m_new = jnp.maximum(m_i, jnp.where(picked, jnp.max(s, axis=0, keepdims=True), NEG))
            return m_new, jnp.where(picked, m_new, BIG)
        return fn

    init = (jnp.full((1, blk), NEG, F32), jnp.zeros((1, blk), F32), jnp.zeros((HEAD_DIM, blk), F32)) * n_heads
    carry = lax.fori_loop(0, jq, lambda j, c: attend(j, c, past_mask(j)), init)
    causal = _iota((blk, 1), 0) <= _iota((1, blk), 1)

    def own_mask(hh, s, m_i):
        m_new = jnp.maximum(m_i, jnp.max(jnp.where(causal, s, NEG), axis=0, keepdims=True))
        return m_new, jnp.where(causal, m_new, BIG)

    final = attend(jq, carry, own_mask)
    outs = [final[3 * hh + 2] / final[3 * hh + 1] for hh in range(n_heads)]
    o_ref[...] = jnp.transpose(jnp.concatenate(outs, axis=0)).astype(o_ref.dtype)


def _moba_prompt(p, b, t):
    blk = MOBA_BLOCK
    assert t % blk == 0
    nb = t // blk
    w = MOBA_HEADS_PER_STEP * HEAD_DIM
    nstep = MOBA_W // w
    qt = p["qbt"]
    vt = p["vbt"].reshape(b, nb, MOBA_W, blk)
    kv = p["kvb"].reshape(b, t, MOBA_ROW)
    km = p["kmean"].reshape(b, nb, MOBA_W)
    return pl.pallas_call(
        functools.partial(_moba_prompt_kernel, blk=blk),
        grid=(b, nstep, nb),
        in_specs=[pl.BlockSpec((None, w, blk), lambda i, hp, j: (i * nb + j, hp, 0)),
                  pl.BlockSpec((None, nb, w), lambda i, hp, j: (i, 0, hp)),
                  pl.BlockSpec((None, t, w), lambda i, hp, j: (i, 0, hp)),
                  pl.BlockSpec((None, nb, w, blk), lambda i, hp, j: (i, 0, hp, 0))],
        out_specs=pl.BlockSpec((None, blk, w), lambda i, hp, j: (i, j, hp)),
        out_shape=jax.ShapeDtypeStruct((b, t, MOBA_W), BF16),
        compiler_params=_cparams(("parallel", "parallel", "arbitrary"), 56),
        name="moba_prompt",
    )(qt, km, kv, vt).reshape(b * t, MOBA_W)


def _group_rows(n_rows=NSA_HEADS):
    return _iota((n_rows, 1), 0) < NSA_GROUP


def _nsa_dec_cmp_kernel(qu_ref, kcv_ref, c2s_ref, oc_ref, idx_ref, *, pos):
    assert NSA_KV_HEADS == 2
    q = qu_ref[...]
    nch = kcv_ref.shape[0]
    ns = c2s_ref.shape[1]
    n_sel = min(SEL_TOPN, -(-(pos + 1) // SEL_BLOCK))
    scale = HEAD_DIM ** -0.5
    g0 = _group_rows()
    hd = HEAD_DIM
    kcv = kcv_ref[...]
    s = jnp.where(g0, _mm_nt(q, kcv[:, 0:hd]), _mm_nt(q, kcv[:, hd:2 * hd])) * scale
    cend = _iota((1, nch), 1) * CMP_STRIDE + (CMP_BLOCK - 1)
    p = _masked_softmax(s, cend <= pos)
    oc_ref[...] = jnp.where(g0, _mm(p, kcv[:, NSA_KV:NSA_KV + hd]), _mm(p, kcv[:, NSA_KV + hd:NSA_KV + 2 * hd]))
    p0 = jnp.sum(jnp.where(g0, p, 0.0), axis=0, keepdims=True)
    p1 = jnp.sum(jnp.where(g0, 0.0, p), axis=0, keepdims=True)
    imp = _mm(jnp.where(g0, p0, p1), c2s_ref[...])
    jsel = _iota((1, ns), 1)
    jsel_f = jsel.astype(F32)
    jq = pos // SEL_BLOCK
    forced = (jsel == 0) | (jsel == jq) | (jsel == jq - 1)
    score = jnp.where(jsel <= jq, jnp.where(forced, BIG, imp), -1.0)
    lane = _iota((1, LANES), 1)
    idx = jnp.full((NSA_HEADS, LANES), -1, I32)
    for it in range(n_sel):
        m, first, pick = _take_first_max(score, jsel_f, float(ns), 1)
        idx = jnp.where(lane == it, jnp.where(m >= 0.0, first, -1.0).astype(I32), idx)
        score = jnp.where(pick, -2.0, score)
    idx_ref[...] = idx


def _attend_with_new(q, kts, vts, valids, k_new, v_new, new_valid, scale):
    s = [_mm(q, kt) * scale for kt in kts]
    qf = q.astype(MXU_DTYPE).astype(F32)
    s_new = jnp.sum(qf * k_new.astype(MXU_DTYPE).astype(F32), axis=1, keepdims=True) * scale
    m = jnp.where(new_valid, s_new, NEG)
    for sj, vj in zip(s, valids):
        m = jnp.maximum(m, jnp.max(jnp.where(vj > 0.5, sj, NEG), axis=1, keepdims=True))
    p = [jnp.where(vj > 0.5, jnp.exp(sj - m), 0.0) for sj, vj in zip(s, valids)]
    p_new = jnp.where(new_valid, jnp.exp(s_new - m), 0.0)
    d = p_new
    for pj in p:
        d = d + jnp.sum(pj, axis=1, keepdims=True)
    d = jnp.where(d > 0, d, 1.0)
    o = (p_new / d).astype(MXU_DTYPE).astype(F32) * v_new.astype(MXU_DTYPE).astype(F32)
    for pj, vt in zip(p, vts):
        o = o + _mm_nt(pj / d, vt)
    return o


def _nsa_dec_att_kernel(idx_ref, pt_ref, qr_ref, gate_ref, oc_ref, new_ref, win_ref, *rest, pos, n_sel, past_blocks):
    del pt_ref
    blk_refs, o_ref = rest[:-1], rest[-1]
    b = pl.program_id(0)
    q = qr_ref[...]
    scale = HEAD_DIM ** -0.5
    hd = HEAD_DIM
    g0 = _group_rows()
    new = new_ref[...]
    n_win = win_ref.shape[-1]
    wpos = pos - n_win + _iota((1, n_win), 1)
    valid_w = jnp.where((wpos > pos - WINDOW) & (wpos >= 0), 1.0, 0.0)
    page = blk_refs[0].shape[-1]
    bpp = page // SEL_BLOCK
    blk_of_lane = lax.shift_right_logical(_iota((1, page), 1), SEL_BLOCK.bit_length() - 1)
    o_s, o_w = [], []
    for g in range(NSA_KV_HEADS):
        glo = g * hd
        kts, vts, valids = [], [], []
        new_valid = jnp.zeros((1, 1), F32)
        for j in range(n_sel):
            bj = idx_ref[b, g, j]
            kts.append(blk_refs[2 * (g * n_sel + j)][...])
            vts.append(blk_refs[2 * (g * n_sel + j) + 1][...])
            in_past = jnp.where((bj >= 0) & (bj < past_blocks), 1.0, 0.0)
            valids.append(jnp.where(blk_of_lane == bj % bpp, in_past, 0.0))
            new_valid = jnp.maximum(new_valid, jnp.where(bj == past_blocks, 1.0, 0.0))
        o_s.append(_attend_with_new(q, kts, vts, valids, new[:, glo:glo + hd],
                                    new[:, NSA_KV + glo:NSA_KV + glo + hd], new_valid > 0.5, scale))
        o_w.append(_attend_with_new(q, [win_ref[0, g]], [win_ref[1, g]], [valid_w],
                                    new[:, 2 * NSA_KV + glo:2 * NSA_KV + glo + hd],
                                    new[:, 3 * NSA_KV + glo:3 * NSA_KV + glo + hd], True, scale))
    gs = _sigmoid(gate_ref[...])
    o_ref[...] = (gs[:, 0:1] * oc_ref[...] + gs[:, 1:2] * jnp.where(g0, o_s[0], o_s[1])
                  + gs[:, 2:3] * jnp.where(g0, o_w[0], o_w[1]))


def _nsa_decode_select(p, kcv, pos):
    db = kcv.shape[0]
    nch = kcv.shape[1]
    ns = pos // SEL_BLOCK + 1
    ns_pad = -(-ns // LANES) * LANES
    n_sel = min(SEL_TOPN, ns)
    c2s = _cmp_to_sel(nch, ns_pad).astype(BF16)
    per_q = lambda w: pl.BlockSpec((None, NSA_HEADS, w), lambda i: (i, 0, 0))
    o_c, idx = pl.pallas_call(
        functools.partial(_nsa_dec_cmp_kernel, pos=pos),
        grid=(db,),
        in_specs=[per_q(HEAD_DIM), pl.BlockSpec((None, nch, 2 * NSA_KV), lambda i: (i, 0, 0)),
                  pl.BlockSpec((nch, ns_pad), lambda i: (0, 0))],
        out_specs=[per_q(HEAD_DIM), per_q(LANES)],
        out_shape=[jax.ShapeDtypeStruct((db, NSA_HEADS, HEAD_DIM), F32),
                   jax.ShapeDtypeStruct((db, NSA_HEADS, LANES), I32)],
        compiler_params=_cparams(("parallel",)),
        name="nsa_dec_cmp",
    )(p["qu"].reshape(db, NSA_HEADS, HEAD_DIM), kcv, c2s)
    return o_c, idx[:, ::NSA_GROUP, :n_sel]


def _nsa_decode(p, kcv, cache_t, win_t, page_table, pos):
    db, n_pages = page_table.shape
    page = cache_t.shape[-1]
    assert pos == n_pages * page and pos % SEL_BLOCK == 0 and page % SEL_BLOCK == 0
    past_blocks = pos // SEL_BLOCK
    o_c, sel_idx = _nsa_decode_select(p, kcv, pos)
    n_sel = sel_idx.shape[2]
    heads3 = lambda a: a.reshape(db, NSA_HEADS, HEAD_DIM)
    bpp = page // SEL_BLOCK

    def blk_map(i, ix, pt, g, j, part):
        bj = jnp.clip(ix[i, g, j], 0, past_blocks - 1)
        return (pt[i, bj // bpp], part, g, 0, 0)

    k_sel, v_sel = 2, 3
    blk_specs = [pl.BlockSpec((None, None, None, HEAD_DIM, page), functools.partial(blk_map, g=g, j=j, part=part))
                 for g in range(NSA_KV_HEADS) for j in range(n_sel) for part in (k_sel, v_sel)]
    per_q2 = lambda w: pl.BlockSpec((None, NSA_HEADS, w), lambda i, ix, pt: (i, 0, 0))
    n_win = win_t.shape[-1]
    gate3 = p["gate"][:, :3 * NSA_HEADS].reshape(db, NSA_HEADS, 3)
    o = pl.pallas_call(
        functools.partial(_nsa_dec_att_kernel, pos=pos, n_sel=n_sel, past_blocks=past_blocks),
        grid_spec=pltpu.PrefetchScalarGridSpec(
            num_scalar_prefetch=2,
            grid=(db,),
            in_specs=[per_q2(HEAD_DIM), per_q2(3), per_q2(HEAD_DIM),
                      pl.BlockSpec((None, 1, 4 * NSA_KV), lambda i, ix, pt: (i, 0, 0)),
                      pl.BlockSpec((None, 2, NSA_KV_HEADS, HEAD_DIM, n_win), lambda i, ix, pt: (i, 0, 0, 0, 0))]
            + blk_specs,
            out_specs=per_q2(HEAD_DIM),
        ),
        out_shape=jax.ShapeDtypeStruct((db, NSA_HEADS, HEAD_DIM), F32),
        compiler_params=_cparams(("arbitrary",)),
        name="nsa_dec_att",
    )(sel_idx, page_table, heads3(p["qr"]), gate3, o_c, p["katt"].reshape(db, 1, 4 * NSA_KV), win_t,
      *([cache_t] * len(blk_specs)))
    return o.reshape(db, NSA_Q)


def _moba_dec_mean_kernel(*refs, n_src, ppb):
    x_refs, o_ref = refs[1:1 + n_src], refs[1 + n_src]
    j = pl.program_id(1)
    nb = o_ref.shape[1]
    page = x_refs[0].shape[-1]
    bps = n_src // ppb

    @pl.when(j == 0)
    def _():
        o_ref[...] = jnp.zeros(o_ref.shape, F32)

    lane = _iota((1, nb), 1)
    acc = o_ref[...]
    for i in range(bps):
        tot = x_refs[i * ppb][...]
        for r in x_refs[i * ppb + 1:(i + 1) * ppb]:
            tot = tot + r[...]
        col = jnp.sum(tot.reshape(MOBA_W, page), axis=1, keepdims=True) * (1.0 / (ppb * page))
        acc = jnp.where(lane == j * bps + i, col, acc)
    o_ref[...] = acc


def _moba_dec_gate_kernel(q_ref, km_ref, idx_ref, *, jq):
    q = q_ref[...]
    nb = km_ref.shape[1]
    n_top = min(MOBA_TOPK, nb)
    head_of_lane = lax.shift_right_logical(_iota((MOBA_HEADS, MOBA_W), 1), HEAD_DIM.bit_length() - 1)
    qbd = jnp.where(head_of_lane == _iota((MOBA_HEADS, MOBA_W), 0),
                    jnp.broadcast_to(q.astype(F32), (MOBA_HEADS, MOBA_W)), 0.0)
    jb = _iota((1, nb), 1)
    jb_f = jb.astype(F32)
    score = jnp.where(jb < jq, _mm(qbd, km_ref[...]), NEG)
    lane = _iota((1, LANES), 1)
    idx = jnp.full((MOBA_HEADS, LANES), -1, I32)
    for it in range(n_top):
        m, first, pick = _take_first_max(score, jb_f, float(nb), 1)
        idx = jnp.where(lane == it, jnp.where(m > 0.5 * NEG, first, -1.0).astype(I32), idx)
        score = jnp.where(pick, 3.0 * NEG, score)
    idx_ref[...] = idx


def _moba_dec_att_kernel(idx_ref, pt_ref, q_ref, kn_ref, vn_ref, *rest, n_top, ppb):
    del pt_ref
    src, o_ref = rest[:-1], rest[-1]
    b, hp = pl.program_id(0), pl.program_id(1)
    hpl = LANES // HEAD_DIM
    scale = HEAD_DIM ** -0.5
    rows = 8
    page = src[0].shape[-1]
    q_all = jnp.broadcast_to(q_ref[...].astype(F32), (rows, LANES))
    k_new = kn_ref[...]
    v_new = vn_ref[...]
    outs = []
    for hh in range(hpl):
        lanes = slice(hh * HEAD_DIM, (hh + 1) * HEAD_DIM)
        kts, vts, valids = [], [], []
        for t in range(n_top):
            ok = jnp.where(idx_ref[b, hp * hpl + hh, t] >= 0, 1.0, 0.0)
            for pg in range(ppb):
                base = 2 * ((hh * n_top + t) * ppb + pg)
                kts.append(src[base][...])
                vts.append(src[base + 1][...])
                valids.append(jnp.full((1, page), 1.0, F32) * ok)
        o = _attend_with_new(q_all[:, lanes], kts, vts, valids, k_new[:, lanes], v_new[:, lanes], True, scale)
        outs.append(o[0:1])
    o_ref[...] = jnp.concatenate(outs, axis=1)


def _moba_decode(p, cache_t, page_table, pos):
    db, n_pages = page_table.shape
    page = cache_t.shape[-1]
    assert MOBA_BLOCK % page == 0 and pos % MOBA_BLOCK == 0 and pos == n_pages * page
    ppb = MOBA_BLOCK // page
    nb = pos // MOBA_BLOCK
    assert nb >= MOBA_TOPK
    pps = PAGES_PER_STEP
    page_specs = [pl.BlockSpec((None, None, MOBA_HEADS, HEAD_DIM, page), functools.partial(
        lambda i, j, pt, pg: (pt[i, j * pps + pg], 0, 0, 0, 0), pg=pg)) for pg in range(pps)]
    kmean_t = pl.pallas_call(
        functools.partial(_moba_dec_mean_kernel, n_src=pps, ppb=ppb),
        grid_spec=pltpu.PrefetchScalarGridSpec(
            num_scalar_prefetch=1,
            grid=(db, n_pages // pps),
            in_specs=page_specs,
            out_specs=pl.BlockSpec((None, MOBA_W, nb), lambda i, j, pt: (i, 0, 0)),
        ),
        out_shape=jax.ShapeDtypeStruct((db, MOBA_W, nb), F32),
        compiler_params=_cparams(("parallel", "arbitrary")),
        name="moba_dec_mean",
    )(page_table, *([cache_t] * pps))
    idx = pl.pallas_call(
        functools.partial(_moba_dec_gate_kernel, jq=nb),
        grid=(db,),
        in_specs=[pl.BlockSpec((None, 1, MOBA_W), lambda i: (i, 0, 0)),
                  pl.BlockSpec((None, MOBA_W, nb), lambda i: (i, 0, 0))],
        out_specs=pl.BlockSpec((None, MOBA_HEADS, LANES), lambda i: (i, 0, 0)),
        out_shape=jax.ShapeDtypeStruct((db, MOBA_HEADS, LANES), I32),
        compiler_params=_cparams(("parallel",)),
        name="moba_dec_gate",
    )(p["qb"].reshape(db, 1, MOBA_W), kmean_t)
    n_top = min(MOBA_TOPK, nb)
    top_idx = idx[:, :, :n_top]
    hpl = LANES // HEAD_DIM
    npair = MOBA_HEADS // hpl

    def src_map(i, hp, ix, pt, hh, t, pg, kv):
        bj = jnp.clip(ix[i, hp * hpl + hh, t], 0, nb - 1)
        return (pt[i, bj * ppb + pg], kv, hp * hpl + hh, 0, 0)

    src_specs = [pl.BlockSpec((None, None, None, HEAD_DIM, page), functools.partial(src_map, hh=hh, t=t, pg=pg, kv=kv))
                 for hh in range(hpl) for t in range(n_top) for pg in range(ppb) for kv in range(2)]
    pair = lambda off: pl.BlockSpec((None, None, 1, LANES), lambda i, hp, ix, pt: (i, off + hp, 0, 0))
    o = pl.pallas_call(
        functools.partial(_moba_dec_att_kernel, n_top=n_top, ppb=ppb),
        grid_spec=pltpu.PrefetchScalarGridSpec(
            num_scalar_prefetch=2,
            grid=(db, npair),
            in_specs=[pair(0), pair(0), pair(npair)] + src_specs,
            out_specs=pair(0),
        ),
        out_shape=jax.ShapeDtypeStruct((db, npair, 1, LANES), F32),
        compiler_params=_cparams(("arbitrary", "arbitrary")),
        name="moba_dec_att",
    )(top_idx, page_table, p["qb"].reshape(db, npair, 1, LANES), p["kvb"].reshape(db, 2 * npair, 1, LANES),
      p["kvb"].reshape(db, 2 * npair, 1, LANES), *([cache_t] * len(src_specs)))
    return o.reshape(db, MOBA_W)


def _route_t(s_t, b_t):
    n_e, n_tok = s_t.shape
    per = n_e // N_GROUPS
    biased = s_t + b_t
    sub_f = _iota((per, 1), 0).astype(F32)
    gscore = []
    for g in range(N_GROUPS):
        x = biased[g * per:(g + 1) * per]
        m1, _, pick = _take_first_max(x, sub_f, float(per), 0)
        gscore.append(m1 + jnp.max(jnp.where(pick, NEG, x), axis=0, keepdims=True))
    gs = jnp.concatenate(gscore, axis=0)
    g_f = _iota((N_GROUPS, 1), 0).astype(F32)
    gmask = jnp.zeros((N_GROUPS, n_tok), F32)
    for _ in range(TOPK_GROUPS):
        _, _, pick = _take_first_max(gs, g_f, float(N_GROUPS), 0)
        gmask = jnp.where(pick, 1.0, gmask)
        gs = jnp.where(pick, NEG, gs)
    masked = jnp.concatenate([jnp.where(gmask[g:g + 1] > 0.5, biased[g * per:(g + 1) * per], NEG)
                              for g in range(N_GROUPS)], axis=0)
    e_f = _iota((n_e, 1), 0).astype(F32)
    ids, ws = [], []
    for _ in range(TOP_K):
        _, first, pick = _take_first_max(masked, e_f, float(n_e), 0)
        ids.append(first)
        ws.append(jnp.sum(jnp.where(pick, s_t, 0.0), axis=0, keepdims=True))
        masked = jnp.where(pick, 3.0 * NEG, masked)
    w = jnp.concatenate(ws, axis=0)
    w = w / jnp.sum(w, axis=0, keepdims=True) * ROUTED_SCALE
    return jnp.concatenate(ids, axis=0).astype(I32), w


def _merge_kernel(x_ref, oa_ref, ob_ref, sc1_ref, sh1_ref, g1_ref, sc2_ref, sh2_ref, wmg_ref, wa_ref, wb_ref,
                  wo_ref, lg_ref, lb_ref, wr_ref, br_ref, x1_ref, h_ref, hp_ref, ti_ref, tw_ref, *, alpha):
    x = x_ref[...]
    d = x.shape[1]
    u = x * (1.0 + sc1_ref[...]) + sh1_ref[...]
    mg = _mm(u, wmg_ref[...])
    y_a = _mm(oa_ref[...], wa_ref[...])
    y_b = _mm(ob_ref[...], wb_ref[...])
    mix = _mm(_sigmoid(mg[:, :d]) * y_a + _sigmoid(mg[:, d:]) * y_b, wo_ref[...])
    x1 = _layer_norm(alpha * x + g1_ref[...] * mix, lg_ref[...], lb_ref[...])
    x1_ref[...] = x1
    h = x1 * (1.0 + sc2_ref[...]) + sh2_ref[...]
    h_ref[...] = h
    hp_ref[...] = _pack_halves(h)
    s_t = _sigmoid(_mm_nt(wr_ref[...], h))
    ti_ref[...], tw_ref[...] = _route_t(s_t, br_ref[...])


def _merge(x, o_a, o_b, mods, w, *, tm, per_token_mod, rows_per_batch, alpha):
    n, d = x.shape
    nt = n // tm
    tiles_per_batch = rows_per_batch // tm
    row = lambda i: (i, 0)
    if per_token_mod:
        mod_spec = pl.BlockSpec((tm, d), row)
    else:
        mod_spec = pl.BlockSpec((None, 1, d), lambda i: (i // tiles_per_batch, 0, 0))
    full = lambda a: pl.BlockSpec(a.shape, lambda i: (0,) * a.ndim)
    ws = [w["w_mg"], w["w_nsa_out"], w["w_moba_out"], w["w_o"], w["ln1_g"], w["ln1_b"], w["w_router_t"], w["b_router"]]
    return pl.pallas_call(
        functools.partial(_merge_kernel, alpha=alpha),
        grid=(nt,),
        in_specs=[pl.BlockSpec((tm, d), row), pl.BlockSpec((tm, NSA_Q), row), pl.BlockSpec((tm, MOBA_W), row)]
        + [mod_spec] * 5 + [full(a) for a in ws],
        out_specs=[pl.BlockSpec((tm, d), row), pl.BlockSpec((tm, d), row), pl.BlockSpec((tm, d // 2), row),
                   pl.BlockSpec((TOP_K, tm), lambda i: (0, i)), pl.BlockSpec((TOP_K, tm), lambda i: (0, i))],
        out_shape=[jax.ShapeDtypeStruct((n, d), F32), jax.ShapeDtypeStruct((n, d), F32),
                   jax.ShapeDtypeStruct((n, d // 2), jnp.uint32),
                   jax.ShapeDtypeStruct((TOP_K, n), I32), jax.ShapeDtypeStruct((TOP_K, n), F32)],
        compiler_params=_cparams(("parallel",), 48),
        name="merge",
    )(x, o_a, o_b, *mods, *ws)


def _pack_halves(x):
    half = x.shape[1] // 2
    bits = pltpu.bitcast(x.astype(BF16).astype(F32), jnp.uint32)
    return bits[:, :half] | lax.shift_right_logical(bits[:, half:], jnp.uint32(16))


def _unpack_halves(p):
    hi = pltpu.bitcast(p & jnp.uint32(0xFFFF0000), F32)
    lo = pltpu.bitcast(lax.shift_left(p, jnp.uint32(16)), F32)
    return jnp.concatenate([hi, lo], axis=1)


def _rank_kernel(ti_ref, rank_ref, cnt_ref, base_ref):
    @pl.when(pl.program_id(0) == 0)
    def _():
        base_ref[...] = jnp.zeros(base_ref.shape, F32)

    ti = ti_ref[...]
    k, tr = ti.shape
    e_iota = _iota((N_EXPERTS, 1), 0)
    hit = lambda j: e_iota == ti[j:j + 1, :]
    onehot = jnp.zeros((N_EXPERTS, tr), F32)
    for j in range(k):
        onehot = onehot + jnp.where(hit(j), 1.0, 0.0)
    tri = (_iota((tr, 1), 0) <= _iota((1, tr), 1)).astype(BF16)
    before = jnp.dot(onehot.astype(BF16), tri, preferred_element_type=F32) - onehot + base_ref[...]
    rank_ref[...] = jnp.concatenate([jnp.sum(jnp.where(hit(j), before, 0.0), axis=0, keepdims=True)
                                     for j in range(k)], axis=0).astype(I32)
    total = base_ref[...] + jnp.sum(onehot, axis=1, keepdims=True)
    base_ref[...] = total
    cnt_ref[...] = jnp.broadcast_to(total, cnt_ref.shape).astype(I32)


def _slot_kernel(ti_ref, rank_ref, start_ref, slot_ref):
    ti = ti_ref[...]
    e_iota = _iota((N_EXPERTS, 1), 0)
    start = start_ref[...]
    rows = [jnp.sum(jnp.where(e_iota == ti[j:j + 1, :], start, 0.0), axis=0, keepdims=True) for j in range(ti.shape[0])]
    slot_ref[...] = rank_ref[...] + jnp.concatenate(rows, axis=0).astype(I32)


def _row_copies(n_tok, n_choice, make_copy):
    def start(t, c):
        for j in range(n_choice):
            make_copy(t, j).start()
        return c

    def wait(t, c):
        for j in range(n_choice):
            make_copy(t, j).wait()
        return c

    return (lambda: lax.fori_loop(0, n_tok, start, 0)), (lambda: lax.fori_loop(0, n_tok, wait, 0))


def _scatter_rows_kernel(slot_ref, h_ref, zeros_ref, xs_ref, sem):
    del zeros_ref
    k, td = slot_ref.shape
    copy = lambda t, j: pltpu.make_async_copy(h_ref.at[pl.ds(t, 1), :], xs_ref.at[pl.ds(slot_ref[j, t], 1), :], sem)
    start, wait = _row_copies(td, k, copy)
    start()
    wait()


def _expert_kernel(be_ref, nu_ref, x_ref, wg_ref, wu_ref, wd_ref, y_ref):
    del be_ref

    @pl.when(pl.program_id(0) < nu_ref[0])
    def _():
        x = _unpack_halves(x_ref[...])
        y_ref[...] = _mm(_silu(_mm(x, wg_ref[...])) * _mm(x, wu_ref[...]), wd_ref[...])


def _token_tile(n, pref):
    return pref if n % pref == 0 else n


def _routed_experts(hp, top_i, w):
    n, dp = hp.shape
    d = 2 * dp
    k = top_i.shape[0]
    blk = min(EXPERT_BLK, max(8, (k * n) // N_EXPERTS))
    nblk = -(-(k * n) // blk) + N_EXPERTS
    tr = _token_tile(n, RANK_TILE)
    tiles = lambda t: pl.BlockSpec((k, t), lambda i: (0, i))
    rank, counts = pl.pallas_call(
        _rank_kernel,
        grid=(n // tr,),
        in_specs=[tiles(tr)],
        out_specs=[tiles(tr), pl.BlockSpec((N_EXPERTS, LANES), lambda i: (0, 0))],
        out_shape=[jax.ShapeDtypeStruct((k, n), I32), jax.ShapeDtypeStruct((N_EXPERTS, LANES), I32)],
        scratch_shapes=[pltpu.VMEM((N_EXPERTS, 1), F32)],
        compiler_params=_cparams(("arbitrary",)),
        name="moe_rank",
    )(top_i)
    counts = counts[:, 0]
    padded = (counts + blk - 1) // blk * blk
    end_pad = jnp.cumsum(padded)
    start_pad = end_pad - padded
    blk_e = jnp.minimum(jnp.sum(end_pad[None, :] <= (jnp.arange(nblk, dtype=I32) * blk)[:, None], axis=1),
                        N_EXPERTS - 1).astype(I32)
    n_used = (end_pad[-1] // blk).astype(I32).reshape(1)
    slot = pl.pallas_call(
        _slot_kernel,
        grid=(n // tr,),
        in_specs=[tiles(tr), tiles(tr), pl.BlockSpec((N_EXPERTS, 1), lambda i: (0, 0))],
        out_specs=tiles(tr),
        out_shape=jax.ShapeDtypeStruct((k, n), I32),
        compiler_params=_cparams(("parallel",)),
        name="moe_slot",
    )(top_i, rank, start_pad.astype(F32).reshape(N_EXPERTS, 1))
    td = _token_tile(n, ROW_COPY_TILE)
    x_sorted = pl.pallas_call(
        _scatter_rows_kernel,
        grid=(n // td,),
        in_specs=[pl.BlockSpec((k, td), lambda i: (0, i), memory_space=pltpu.SMEM),
                  pl.BlockSpec((td, dp), lambda i: (i, 0)), pl.BlockSpec(memory_space=pl.ANY)],
        out_specs=pl.BlockSpec(memory_space=pl.ANY),
        out_shape=jax.ShapeDtypeStruct((nblk * blk, dp), jnp.uint32),
        input_output_aliases={2: 0},
        scratch_shapes=[pltpu.SemaphoreType.DMA],
        compiler_params=_cparams(("arbitrary",)),
        name="moe_scatter",
    )(slot, hp, jnp.zeros((nblk * blk, dp), jnp.uint32))
    de = w["w_exp_gate"].shape[2]
    y = pl.pallas_call(
        _expert_kernel,
        grid_spec=pltpu.PrefetchScalarGridSpec(
            num_scalar_prefetch=2,
            grid=(nblk,),
            in_specs=[pl.BlockSpec((blk, dp), lambda i, be, nu: (i, 0)),
                      pl.BlockSpec((None, d, de), lambda i, be, nu: (be[i], 0, 0)),
                      pl.BlockSpec((None, d, de), lambda i, be, nu: (be[i], 0, 0)),
                      pl.BlockSpec((None, de, d), lambda i, be, nu: (be[i], 0, 0))],
            out_specs=pl.BlockSpec((blk, d), lambda i, be, nu: (i, 0)),
        ),
        out_shape=jax.ShapeDtypeStruct((nblk * blk, d), F32),
        compiler_params=_cparams(("arbitrary",), 48),
        name="experts",
    )(blk_e, n_used, x_sorted, w["w_exp_gate"], w["w_exp_up"], w["w_exp_down"])
    return y, slot


def _final_kernel(slot_ref, x1_ref, h_ref, tw_ref, g2_ref, wg_ref, wu_ref, wd_ref, lg_ref, lb_ref, y_ref, o_ref,
                  buf_ref, sem, *, alpha):
    k, tc = slot_ref.shape
    copy = lambda t, j: pltpu.make_async_copy(y_ref.at[pl.ds(slot_ref[j, t], 1), :],
                                              buf_ref.at[j, pl.ds(t, 1), :], sem)
    start, wait = _row_copies(tc, k, copy)
    start()
    h = h_ref[...]
    shared = _mm(_silu(_mm(h, wg_ref[...])) * _mm(h, wu_ref[...]), wd_ref[...])
    wait()
    side = max(tc, LANES)
    tw = tw_ref[...]
    if tc < side:
        tw = jnp.concatenate([tw, jnp.zeros((k, side - tc), F32)], axis=1)
    tw_t = jnp.transpose(jnp.concatenate([tw, jnp.zeros((side - k, side), F32)], axis=0))[0:tc]
    routed = tw_t[:, 0:1] * buf_ref[0]
    for j in range(1, k):
        routed = routed + tw_t[:, j:j + 1] * buf_ref[j]
    o_ref[...] = _layer_norm(alpha * x1_ref[...] + g2_ref[...] * (routed + shared), lg_ref[...], lb_ref[...])


def _final(x1, h, y_sorted, slot, top_w, g2, w, *, per_token_mod, rows_per_batch, alpha):
    n, d = x1.shape
    k = slot.shape[0]
    tc = _token_tile(n, ROW_COPY_TILE)
    tiles_per_batch = rows_per_batch // tc
    tile = pl.BlockSpec((tc, d), lambda i: (i, 0))
    mod_spec = tile if per_token_mod else pl.BlockSpec((None, 1, d), lambda i: (i // tiles_per_batch, 0, 0))
    full = lambda a: pl.BlockSpec(a.shape, lambda i: (0,) * a.ndim)
    ws = [w["w_sh_gate"], w["w_sh_up"], w["w_sh_down"], w["ln2_g"], w["ln2_b"]]
    return pl.pallas_call(
        functools.partial(_final_kernel, alpha=alpha),
        grid=(n // tc,),
        in_specs=[pl.BlockSpec((k, tc), lambda i: (0, i), memory_space=pltpu.SMEM), tile, tile,
                  pl.BlockSpec((k, tc), lambda i: (0, i)), mod_spec] + [full(a) for a in ws]
        + [pl.BlockSpec(memory_space=pl.ANY)],
        out_specs=tile,
        out_shape=jax.ShapeDtypeStruct((n, d), F32),
        scratch_shapes=[pltpu.VMEM((k, tc, d), F32), pltpu.SemaphoreType.DMA],
        compiler_params=_cparams(("arbitrary",), 40),
        name="final",
    )(slot, x1, h, top_w, g2, *ws, y_sorted)


def _prep_weights(lp):
    (w_ada, b_ada, w_in, cmp_k_w1, cmp_k_w2, cmp_k_pe, cmp_v_w1, cmp_v_w2, cmp_v_pe, w_nsa_out, w_moba_out, w_o,
     ln1_g, ln1_b, w_router, b_router, w_exp_gate, w_exp_up, w_exp_down, w_sh_gate, w_sh_up, w_sh_down,
     ln2_g, ln2_b) = lp
    c = lambda a: a.astype(MXU_DTYPE)
    row = lambda a: a.reshape(1, -1)
    w_r, w_mg = _reorder_w_in(w_in)
    wk, w2k, pek = _cmp_weights(cmp_k_w1, cmp_k_w2, cmp_k_pe)
    wv, w2v, pev = _cmp_weights(cmp_v_w1, cmp_v_w2, cmp_v_pe)
    return dict(w_ada=w_ada, b_ada=b_ada, w_r=w_r, w_mg=w_mg, cmp=(wk, w2k, pek, wv, w2v, pev),
                w_nsa_out=c(w_nsa_out), w_moba_out=c(w_moba_out), w_o=c(w_o), ln1_g=row(ln1_g), ln1_b=row(ln1_b),
                w_router_t=c(w_router.T), b_router=b_router.reshape(-1, 1),
                w_exp_gate=c(w_exp_gate), w_exp_up=c(w_exp_up), w_exp_down=c(w_exp_down),
                w_sh_gate=c(w_sh_gate), w_sh_up=c(w_sh_up), w_sh_down=c(w_sh_down), ln2_g=row(ln2_g), ln2_b=row(ln2_b))


def _token_tail(x, o_a, o_b, mods, w, *, tm, per_token_mod, rows_per_batch, alpha):
    sc1, sh1, g1, sh2, sc2, g2 = mods
    kw = dict(tm=tm, per_token_mod=per_token_mod, rows_per_batch=rows_per_batch, alpha=alpha)
    x1, h, hp, top_i, top_w = _merge(x, o_a, o_b, (sc1, sh1, g1, sc2, sh2), w, **kw)
    y_sorted, slot = _routed_experts(hp, top_i, w)
    return _final(x1, h, y_sorted, slot, top_w, g2, w, per_token_mod=per_token_mod, rows_per_batch=rows_per_batch,
                  alpha=alpha)


def _layer(xp, xs, c_all, cache_nsa_l, cache_moba_l, win_state_l, page_table, w, alpha):
    b, t, d = xp.shape
    db, ts, _ = xs.shape
    assert ts == 1
    page = cache_nsa_l.shape[1]
    pos = page_table.shape[1] * page
    assert win_state_l.shape[1] == WINDOW and t >= WINDOW
    mod = _ada(c_all, w["w_ada"], w["b_ada"])
    pieces = [mod[:, i * d:(i + 1) * d] for i in range(6)]
    mods_p = [m[:b].reshape(b, 1, d) for m in pieces]
    mods_s = [m[b:b + db] for m in pieces]
    order = lambda m: (m[1], m[0], m[2], m[3], m[4], m[5])
    mods_p, mods_s = order(mods_p), order(mods_s)

    tm = TOKEN_TILE
    xp2 = xp.reshape(b * t, d)
    pp = _inproj(xp2, mods_p[0], mods_p[1], w["w_r"], _rope_tables(jnp.arange(t)), tm=tm, per_token_mod=False,
                 rows_per_batch=t, prompt=True)
    kcv_p, vct_p = _cmp_prompt(pp["kcvc"].reshape(b, t, 2 * NSA_KV), w["cmp"])
    oa_p = _nsa_prompt(pp, kcv_p, vct_p, b, t)
    ob_p = _moba_prompt(pp, b, t)
    yp = _token_tail(xp2, oa_p, ob_p, mods_p, w, tm=tm, per_token_mod=False, rows_per_batch=t, alpha=alpha)

    xs2 = xs.reshape(db, d)
    ps = _inproj(xs2, mods_s[0], mods_s[1], w["w_r"], _rope_tables(jnp.full((db,), pos)), tm=db, per_token_mod=True,
                 rows_per_batch=db, prompt=False)
    token_minor = lambda a: jnp.transpose(a, (0, 2, 3, 4, 1))
    kcv_s, _ = _cmp_decode(token_minor(cache_nsa_l), page_table, w["cmp"])
    oa_s = _nsa_decode(ps, kcv_s, token_minor(cache_nsa_l), token_minor(win_state_l), page_table, pos)
    ob_s = _moba_decode(ps, token_minor(cache_moba_l), page_table, pos)
    ys = _token_tail(xs2, oa_s, ob_s, mods_s, w, tm=db, per_token_mod=True, rows_per_batch=db, alpha=alpha)

    g, hd = NSA_KV_HEADS, HEAD_DIM
    token_major = lambda a, parts, heads: jnp.transpose(a.reshape(b, parts, heads, hd, a.shape[-1]), (0, 4, 1, 2, 3))
    win_p = token_major(pp["wint"][:, :, t - WINDOW:], 2, g)
    win_s = jnp.concatenate([win_state_l[:, 1:], ps["win"].reshape(db, 1, 2, g, hd)], axis=1)
    return (yp.reshape(b, t, d), ys.reshape(db, 1, d),
            token_major(pp["nsat"], 4, g), ps["nsa"].reshape(db, 1, 4, g, hd),
            token_major(pp["mobat"], 2, MOBA_HEADS), ps["moba"].reshape(db, 1, 2, MOBA_HEADS, hd), win_p, win_s)


def kernel(x_prompt, x_sample, cache_nsa, cache_moba, state_nsa_win, page_table, c_prompt, c_sample, w_ada, b_ada,
           w_in, cmp_k_w1, cmp_k_w2, cmp_k_pe, cmp_v_w1, cmp_v_w2, cmp_v_pe, w_nsa_out, w_moba_out, w_o, ln1_g,
           ln1_b, w_router, b_router, w_exp_gate, w_exp_up, w_exp_down, w_sh_gate, w_sh_up, w_sh_down, ln2_g,
           ln2_b):
    params = (w_ada, b_ada, w_in, cmp_k_w1, cmp_k_w2, cmp_k_pe, cmp_v_w1, cmp_v_w2, cmp_v_pe, w_nsa_out, w_moba_out,
              w_o, ln1_g, ln1_b, w_router, b_router, w_exp_gate, w_exp_up, w_exp_down, w_sh_gate, w_sh_up,
              w_sh_down, ln2_g, ln2_b)
    depth = w_ada.shape[0]
    alpha = (2 * depth) ** 0.25
    b, db = x_prompt.shape[0], x_sample.shape[0]
    rows = -(-(b + db) // 8) * 8
    c_all = jnp.pad(jnp.concatenate([c_prompt, c_sample], axis=0), ((0, rows - b - db), (0, 0)))
    xp, xs = x_prompt, x_sample
    outs = [[] for _ in range(6)]
    for l in range(depth):
        w = _prep_weights([p[l] for p in params])
        res = _layer(xp, xs, c_all, cache_nsa[l], cache_moba[l], state_nsa_win[l], page_table, w, alpha)
        xp, xs = res[0], res[1]
        for acc, r in zip(outs, res[2:]):
            acc.append(r)
    return (xp, xs) + tuple(jnp.stack(o) for o in outs)
```
